```python
import jax, jax.numpy as jnp
from jax import lax
import numpy as np

D_MODEL = 1024
BATCH = 1
SEQ = 16384
DEPTH = 2

MEM_LEN = 256
EPS = 1e-6
BLOCK = 128
CONV_CH = D_MODEL // 2
CONV_WIDTH = 31
SB_HEADS = 8
SB_HEAD_DIM = (D_MODEL // 2) // SB_HEADS
SB_DIM = SB_HEADS * SB_HEAD_DIM
AB_IN = 2 * CONV_CH + 3 * SB_DIM
GM_DIM = D_MODEL
GM_GROUPS = 8
GM_GROUP_DIM = GM_DIM // GM_GROUPS
GM_CHUNK = 128
XA_HEADS = 4
XA_HEAD_DIM = D_MODEL // XA_HEADS
N_GROUPS = 4
EXPERTS_PER_GROUP = 4
N_EXPERTS = N_GROUPS * EXPERTS_PER_GROUP
TOP_K_IN_GROUP = 2
D_EXPERT = D_MODEL // 4
N_EVEN = (DEPTH + 1) // 2
N_ODD = DEPTH // 2

kernel_name = 'hybrid_conv_stickbreak_gmlp_hmoe'


def rmsnorm(x, g):
    xf = x.astype(jnp.float32)
    y = xf * lax.rsqrt(jnp.mean(xf * xf, axis=-1, keepdims=True) + EPS)
    return (y * g.astype(jnp.float32)).astype(x.dtype)


def layernorm(x, g, b):
    xf = x.astype(jnp.float32)
    mu = jnp.mean(xf, axis=-1, keepdims=True)
    var = jnp.mean(jnp.square(xf - mu), axis=-1, keepdims=True)
    y = (xf - mu) * lax.rsqrt(var + EPS) * g.astype(jnp.float32) + b.astype(jnp.float32)
    return y.astype(x.dtype)


def conformer_conv(a, gate, conv_w, conv_b, ln_g, ln_b):
    h = a * jax.nn.sigmoid(gate)
    h = lax.conv_general_dilated(h, conv_w[:, None, :].astype(h.dtype), window_strides=(1,),
                                 padding=[(CONV_WIDTH - 1, 0)],
                                 dimension_numbers=('NWC', 'WIO', 'NWC'),
                                 feature_group_count=CONV_CH) + conv_b
    h = layernorm(h, ln_g, ln_b)
    return jax.nn.silu(h)


def stick_breaking_attention(q, k, v):
    bsz, s_len, n_h, dh = q.shape
    nb = s_len // BLOCK
    scale = dh ** -0.5
    qb = q.reshape(bsz, nb, BLOCK, n_h, dh).transpose(1, 0, 3, 2, 4)
    kt = k.transpose(0, 2, 1, 3)
    vt = v.transpose(0, 2, 1, 3)
    kpos = jnp.arange(s_len)

    def one_block(args):
        q_blk, start = args
        z = jnp.einsum('bhqd,bhkd->bhqk', q_blk, kt, preferred_element_type=jnp.float32) * scale
        qpos = start + jnp.arange(BLOCK)
        mask = kpos[None, :] < qpos[:, None]
        log_stay = jnp.where(mask, jax.nn.log_sigmoid(-z), 0.0)
        after = lax.cumsum(log_stay, axis=3, reverse=True) - log_stay
        log_w = jnp.where(mask, jax.nn.log_sigmoid(z) + after, -jnp.inf)
        w = jnp.exp(log_w)
        return jnp.einsum('bhqk,bhkd->bhqd', w.astype(vt.dtype), vt)

    starts = jnp.arange(nb, dtype=jnp.int32) * BLOCK
    o = lax.map(one_block, (qb, starts))
    return o.transpose(1, 0, 3, 2, 4).reshape(bsz, s_len, n_h * dh)


def mixer_ab(xn, w_in, conv_w, conv_b, ln_g, ln_b, w_out):
    bsz, s_len, _ = xn.shape
    p = xn @ w_in
    a, g, q, k, v = jnp.split(p, [CONV_CH, 2 * CONV_CH, 2 * CONV_CH + SB_DIM,
                                  2 * CONV_CH + 2 * SB_DIM], axis=-1)
    conv_out = conformer_conv(a, g, conv_w, conv_b, ln_g, ln_b)
    heads = lambda t: t.reshape(bsz, s_len, SB_HEADS, SB_HEAD_DIM)
    sb_out = stick_breaking_attention(heads(q), heads(k), heads(v))
    return jnp.concatenate([conv_out, sb_out.astype(conv_out.dtype)], axis=-1) @ w_out


def mixer_c(xn, w_in, v_g, v_b, ws, bs, w_out):
    bsz, s_len, _ = xn.shape
    p = jax.nn.gelu(xn @ w_in, approximate=False)
    u, v = jnp.split(p, 2, axis=-1)
    v = layernorm(v, v_g, v_b)
    nc = s_len // GM_CHUNK
    v = v.reshape(bsz, nc, GM_CHUNK, GM_GROUPS, GM_GROUP_DIM)
    causal = jnp.tril(jnp.ones((GM_CHUNK, GM_CHUNK), dtype=ws.dtype))
    mixed = jnp.einsum('gts,bcsgd->bctgd', ws * causal[None], v) + bs.T[:, :, None]
    out = u * mixed.reshape(bsz, s_len, GM_DIM)
    return out @ w_out


def cross_attn(xn, memn, wq, wk, wv, wo):
    bsz, s_len, d = xn.shape
    m_len = memn.shape[1]
    q = (xn @ wq).reshape(bsz, s_len, XA_HEADS, XA_HEAD_DIM)
    k = (memn @ wk).reshape(bsz, m_len, XA_HEADS, XA_HEAD_DIM)
    v = (memn @ wv).reshape(bsz, m_len, XA_HEADS, XA_HEAD_DIM)
    s = jnp.einsum('bqhd,bkhd->bhqk', q, k, preferred_element_type=jnp.float32) * (XA_HEAD_DIM ** -0.5)
    pr = jax.nn.softmax(s, axis=-1).astype(v.dtype)
    o = jnp.einsum('bhqk,bkhd->bqhd', pr, v).reshape(bsz, s_len, d)
    return o @ wo


def hier_moe(xn, wg, bg, we, be, w_gate, w_up, w_down):
    bsz, s_len, d = xn.shape
    t = xn.reshape(bsz * s_len, d)
    n_tok = t.shape[0]
    lg = (t @ wg).astype(jnp.float32) + bg.astype(jnp.float32)
    pg = jax.nn.softmax(lg, axis=-1)
    pg_top, g_idx = lax.top_k(pg, 1)
    le = ((t @ we).astype(jnp.float32) + be.astype(jnp.float32)).reshape(n_tok, N_GROUPS, EXPERTS_PER_GROUP)
    idx = jnp.broadcast_to(g_idx[:, :, None], (n_tok, 1, EXPERTS_PER_GROUP))
    le_sel = jnp.take_along_axis(le, idx, axis=1)[:, 0]
    pe = jax.nn.softmax(le_sel, axis=-1)
    w2, i2 = lax.top_k(pe, TOP_K_IN_GROUP)
    w2 = w2 / jnp.sum(w2, axis=-1, keepdims=True) * pg_top
    ids = g_idx * EXPERTS_PER_GROUP + i2
    gates = jnp.sum(jax.nn.one_hot(ids, N_EXPERTS, dtype=jnp.float32) * w2[..., None], axis=1)
    h = jax.nn.silu(jnp.einsum('td,edf->tef', t, w_gate)) * jnp.einsum('td,edf->tef', t, w_up)
    y = jnp.einsum('tef,efd->td', h * gates[:, :, None].astype(h.dtype), w_down)
    return y.reshape(bsz, s_len, d)


def setup_inputs(seed: int = 0) -> dict:
    key = jax.random.key(seed)
    ks = iter(jax.random.split(key, 40))
    nrm = lambda shape, scale: jax.random.normal(next(ks), shape, jnp.float32) * scale
    gain = lambda shape: 1.0 + nrm(shape, 0.02)
    d = D_MODEL
    return {
        'x': nrm((BATCH, SEQ, d), 1.0),
        'mem': nrm((BATCH, MEM_LEN, d), 1.0),
        'mem_norm_g': gain((d,)),
        'norm_mix': gain((DEPTH, d)),
        'norm_xa': gain((DEPTH, d)),
        'norm_ffn': gain((DEPTH, d)),
        'final_norm_g': gain((d,)),
        'ab_w_in': nrm((N_EVEN, d, AB_IN), d ** -0.5),
        'ab_conv_w': nrm((N_EVEN, CONV_WIDTH, CONV_CH), CONV_WIDTH ** -0.5),
        'ab_conv_b': nrm((N_EVEN, CONV_CH), 0.02),
        'ab_cnorm_g': gain((N_EVEN, CONV_CH)),
        'ab_cnorm_b': nrm((N_EVEN, CONV_CH), 0.02),
        'ab_w_out': nrm((N_EVEN, CONV_CH + SB_DIM, d), (CONV_CH + SB_DIM) ** -0.5),
        'c_w_in': nrm((N_ODD, d, 2 * GM_DIM), d ** -0.5),
        'c_vnorm_g': gain((N_ODD, GM_DIM)),
        'c_vnorm_b': nrm((N_ODD, GM_DIM), 0.02),
        'c_ws': nrm((N_ODD, GM_GROUPS, GM_CHUNK, GM_CHUNK), GM_CHUNK ** -0.5),
        'c_bs': 1.0 + nrm((N_ODD, GM_GROUPS, GM_CHUNK), 0.1),
        'c_w_out': nrm((N_ODD, GM_DIM, d), GM_DIM ** -0.5),
        'xa_wq': nrm((DEPTH, d, d), d ** -0.5),
        'xa_wk': nrm((DEPTH, d, d), d ** -0.5),
        'xa_wv': nrm((DEPTH, d, d), d ** -0.5),
        'xa_wo': nrm((DEPTH, d, d), d ** -0.5),
        'rt_wg': nrm((DEPTH, d, N_GROUPS), d ** -0.5),
        'rt_bg': nrm((DEPTH, N_GROUPS), 0.01),
        'rt_we': nrm((DEPTH, d, N_EXPERTS), d ** -0.5),
        'rt_be': nrm((DEPTH, N_EXPERTS), 0.01),
        'ex_w_gate': nrm((DEPTH, N_EXPERTS, d, D_EXPERT), d ** -0.5),
        'ex_w_up': nrm((DEPTH, N_EXPERTS, d, D_EXPERT), d ** -0.5),
        'ex_w_down': nrm((DEPTH, N_EXPERTS, D_EXPERT, d), D_EXPERT ** -0.5),
    }


def reference(x, mem, mem_norm_g, norm_mix, norm_xa, norm_ffn, final_norm_g,
              ab_w_in, ab_conv_w, ab_conv_b, ab_cnorm_g, ab_cnorm_b, ab_w_out,
              c_w_in, c_vnorm_g, c_vnorm_b, c_ws, c_bs, c_w_out,
              xa_wq, xa_wk, xa_wv, xa_wo,
              rt_wg, rt_bg, rt_we, rt_be,
              ex_w_gate, ex_w_up, ex_w_down):
    memn = rmsnorm(mem, mem_norm_g)
    h = x
    for i in range(DEPTH):
        j = i // 2
        xn = rmsnorm(h, norm_mix[i])
        if i % 2 == 0:
            h = h + mixer_ab(xn, ab_w_in[j], ab_conv_w[j], ab_conv_b[j],
                             ab_cnorm_g[j], ab_cnorm_b[j], ab_w_out[j])
        else:
            h = h + mixer_c(xn, c_w_in[j], c_vnorm_g[j], c_vnorm_b[j],
                            c_ws[j], c_bs[j], c_w_out[j])
        h = h + cross_attn(rmsnorm(h, norm_xa[i]), memn, xa_wq[i], xa_wk[i], xa_wv[i], xa_wo[i])
        h = h + hier_moe(rmsnorm(h, norm_ffn[i]), rt_wg[i], rt_bg[i], rt_we[i], rt_be[i],
                         ex_w_gate[i], ex_w_up[i], ex_w_down[i])
    return rmsnorm(h, final_norm_g)
```

```python
import functools

import jax
import jax.numpy as jnp
from jax import lax
from jax.experimental import pallas as pl
from jax.experimental.pallas import tpu as pltpu

D_MODEL = 1024
SEQ = 16384
MEM_LEN = 256
EPS = 1e-6
CONV_CH = 512
CONV_WIDTH = 31
SB_HEADS = 8
SB_HEAD_DIM = 64
SB_DIM = 512
GM_GROUPS = 8
GM_CHUNK = 128
XA_HEADS = 4
XA_HEAD_DIM = 256
N_GROUPS = 4
EXPERTS_PER_GROUP = 4
N_EXPERTS = 16
D_EXPERT = 256

LANES = 128
ROW_TILE = 512
SB_TQ = 256
SB_TK = 256
CONV_HALO = 32
CONV_ROWS = 32
VMEM_LIMIT = 56 * 1024 * 1024

BF16 = jnp.bfloat16
F32 = jnp.float32


def _params(*semantics):
    return pltpu.CompilerParams(dimension_semantics=semantics, vmem_limit_bytes=VMEM_LIMIT)


def _dot(a, b):
    return jnp.dot(a, b, preferred_element_type=F32)


def _rmsnorm(x, g):
    return x * lax.rsqrt(jnp.mean(x * x, axis=-1, keepdims=True) + EPS) * g


def _layernorm(x, g, b):
    mu = jnp.mean(x, axis=-1, keepdims=True)
    xc = x - mu
    var = jnp.mean(xc * xc, axis=-1, keepdims=True)
    return xc * lax.rsqrt(var + EPS) * g + b


def _full(shape):
    return pl.BlockSpec(shape, lambda *_: (0,) * len(shape))


def _mem_kv_kernel(mem_ref, g_ref, wk_ref, wv_ref, k_ref, v_ref):
    memn = _rmsnorm(mem_ref[...], g_ref[...]).astype(BF16)
    k_ref[0] = _dot(memn, wk_ref[0]).astype(BF16)
    v_ref[0] = _dot(memn, wv_ref[0]).astype(BF16)


def _mem_kv(mem, g, wk, wv):
    depth = wk.shape[0]
    wspec = pl.BlockSpec((1, D_MODEL, D_MODEL), lambda i: (i, 0, 0))
    ospec = pl.BlockSpec((1, MEM_LEN, D_MODEL), lambda i: (i, 0, 0))
    return pl.pallas_call(
        _mem_kv_kernel,
        grid=(depth,),
        in_specs=[_full((MEM_LEN, D_MODEL)), _full((1, D_MODEL)), wspec, wspec],
        out_specs=[ospec, ospec],
        out_shape=[jax.ShapeDtypeStruct((depth, MEM_LEN, D_MODEL), BF16)] * 2,
        compiler_params=_params("arbitrary"),
        name="mem_kv",
    )(mem, g, wk, wv)


def _l0_inproj_kernel(x_ref, g_ref, w_ag_ref, w_q_ref, w_kt_ref, w_v_ref,
                      h_ref, q_ref, kt_ref, v_ref):
    xn = _rmsnorm(x_ref[...], g_ref[...]).astype(BF16)
    ag = _dot(xn, w_ag_ref[...])
    h_ref[...] = ag[:, :CONV_CH] * jax.nn.sigmoid(ag[:, CONV_CH:])
    q_ref[...] = (_dot(xn, w_q_ref[...]) * (SB_HEAD_DIM ** -0.5)).astype(BF16)
    kt_ref[...] = lax.dot_general(w_kt_ref[...], xn, (((1,), (1,)), ((), ())),
                                  preferred_element_type=F32).astype(BF16)
    v_ref[...] = _dot(xn, w_v_ref[...]).astype(BF16)


def _l0_inproj(x, g, w_ag, w_q, w_kt, w_v):
    tm = ROW_TILE
    row = lambda n: pl.BlockSpec((tm, n), lambda i: (i, 0))
    return pl.pallas_call(
        _l0_inproj_kernel,
        grid=(SEQ // tm,),
        in_specs=[row(D_MODEL), _full((1, D_MODEL)), _full((D_MODEL, 2 * CONV_CH)),
                  _full((D_MODEL, SB_DIM)), _full((SB_DIM, D_MODEL)), _full((D_MODEL, SB_DIM))],
        out_specs=[row(CONV_CH), row(SB_DIM), pl.BlockSpec((SB_DIM, tm), lambda i: (0, i)),
                   row(SB_DIM)],
        out_shape=[jax.ShapeDtypeStruct((SEQ, CONV_CH), F32),
                   jax.ShapeDtypeStruct((SEQ, SB_DIM), BF16),
                   jax.ShapeDtypeStruct((SB_DIM, SEQ), BF16),
                   jax.ShapeDtypeStruct((SEQ, SB_DIM), BF16)],
        compiler_params=_params("parallel"),
        name="l0_inproj",
    )(x, g, w_ag, w_q, w_kt, w_v)


def _conv_kernel(prev_ref, cur_ref, cw_ref, cb_ref, lg_ref, lb_ref, o_ref, ext_ref):
    tm = cur_ref.shape[0]
    ext_ref[0:CONV_HALO, :] = jnp.where(pl.program_id(0) > 0, prev_ref[...], 0.0)
    ext_ref[CONV_HALO:, :] = cur_ref[...]
    first_tap = CONV_HALO - (CONV_WIDTH - 1)

    def chunk(c, _):
        r0 = pl.multiple_of(c * CONV_ROWS, CONV_ROWS)
        win = ext_ref[pl.ds(r0, CONV_ROWS + CONV_HALO), :]
        acc = jnp.zeros((CONV_ROWS, CONV_CH), F32) + cb_ref[...]
        for w in range(CONV_WIDTH):
            acc = acc + win[first_tap + w:first_tap + w + CONV_ROWS, :] * cw_ref[w:w + 1, :]
        y = _layernorm(acc, lg_ref[...], lb_ref[...])
        o_ref[pl.ds(r0, CONV_ROWS), :] = (y * jax.nn.sigmoid(y)).astype(BF16)
        return 0

    lax.fori_loop(0, tm // CONV_ROWS, chunk, 0)


def _conv(h, cw, cb, lg, lb):
    tm = ROW_TILE
    halo_blocks = tm // CONV_HALO
    return pl.pallas_call(
        _conv_kernel,
        grid=(SEQ // tm,),
        in_specs=[pl.BlockSpec((CONV_HALO, CONV_CH),
                               lambda i: (jnp.maximum(i * halo_blocks - 1, 0), 0)),
                  pl.BlockSpec((tm, CONV_CH), lambda i: (i, 0)),
                  _full((CONV_WIDTH, CONV_CH)), _full((1, CONV_CH)),
                  _full((1, CONV_CH)), _full((1, CONV_CH))],
        out_specs=pl.BlockSpec((tm, CONV_CH), lambda i: (i, 0)),
        out_shape=jax.ShapeDtypeStruct((SEQ, CONV_CH), BF16),
        scratch_shapes=[pltpu.VMEM((tm + CONV_HALO, CONV_CH), F32)],
        compiler_params=_params("parallel"),
        name="conv_ln_silu",
    )(h, h, cw, cb, lg, lb)


def _sb_kernel(q_ref, kt_ref, v_ref, o_ref, acc_ref, carry_ref):
    qi = pl.program_id(1)
    tq, tk = SB_TQ, SB_TK
    lane = lax.broadcasted_iota(jnp.int32, (tq, LANES), 1)
    q2 = q_ref[...]
    zero = jnp.zeros_like(q2)
    qh = (jnp.where(lane < SB_HEAD_DIM, q2, zero), jnp.where(lane >= SB_HEAD_DIM, q2, zero))
    row = lax.broadcasted_iota(jnp.int32, (tq, tk), 0)
    col = lax.broadcasted_iota(jnp.int32, (tq, tk), 1)
    jj = lax.broadcasted_iota(jnp.int32, (tk, tk), 0)
    ss = lax.broadcasted_iota(jnp.int32, (tk, tk), 1)
    later = jnp.where(jj > ss, 1.0, 0.0).astype(BF16)

    acc_ref[...] = jnp.zeros_like(acc_ref)
    carry_ref[...] = jnp.zeros_like(carry_ref)

    def step(it, _):
        kb = qi - it
        k0 = pl.multiple_of(kb * tk, tk)
        kt = kt_ref[:, pl.ds(k0, tk)]
        vv = v_ref[pl.ds(k0, tk), :]
        valid = (col + kb * tk) < (row + qi * tq)
        for h in range(2):
            z = _dot(qh[h], kt)
            sp = jnp.maximum(z, 0.0) + jnp.log1p(jnp.exp(-jnp.abs(z)))
            log_stay = jnp.where(valid, -sp, 0.0)
            hi = log_stay.astype(BF16)
            lo = (log_stay - hi.astype(F32)).astype(BF16)
            after = _dot(hi, later) + _dot(lo, later)
            w = jnp.where(valid, jnp.exp((z - sp) + after), 0.0)
            carry = carry_ref[h]
            acc_ref[h] += jnp.exp(carry) * _dot(w.astype(BF16), vv)
            carry_ref[h] = carry + jnp.sum(log_stay, axis=1, keepdims=True)
        return 0

    lax.fori_loop(0, qi + 1, step, 0)
    o_ref[...] = jnp.where(lane < SB_HEAD_DIM, acc_ref[0], acc_ref[1]).astype(BF16)


def _sb_attention(q, kt, v):
    tq = SB_TQ
    return pl.pallas_call(
        _sb_kernel,
        grid=(SB_DIM // LANES, SEQ // tq),
        in_specs=[pl.BlockSpec((tq, LANES), lambda p, i: (i, p)),
                  pl.BlockSpec((LANES, SEQ), lambda p, i: (p, 0)),
                  pl.BlockSpec((SEQ, LANES), lambda p, i: (0, p))],
        out_specs=pl.BlockSpec((tq, LANES), lambda p, i: (i, p)),
        out_shape=jax.ShapeDtypeStruct((SEQ, SB_DIM), BF16),
        scratch_shapes=[pltpu.VMEM((2, tq, LANES), F32), pltpu.VMEM((2, tq, LANES), F32)],
        compiler_params=_params("parallel", "parallel"),
        name="sb_attention",
    )(q, kt, v)


def _router_gates(logits):
    col = lax.broadcasted_iota(jnp.int32, logits.shape, 1)
    colf = col.astype(F32)
    ninf = -jnp.inf
    first = lambda hit: jnp.min(jnp.where(hit, colf, float(LANES)), axis=1, keepdims=True)

    is_group = (col >= N_EXPERTS) & (col < N_EXPERTS + N_GROUPS)
    lg = jnp.where(is_group, logits, ninf)
    gmax = jnp.max(lg, axis=1, keepdims=True)
    pg_top = 1.0 / jnp.sum(jnp.exp(lg - gmax), axis=1, keepdims=True)
    g_idx = first(lg == gmax) - float(N_EXPERTS)

    in_group = (col < N_EXPERTS) & ((col // EXPERTS_PER_GROUP).astype(F32) == g_idx)
    le = jnp.where(in_group, logits, ninf)
    m1 = jnp.max(le, axis=1, keepdims=True)
    i1 = first(le == m1)
    le2 = jnp.where(colf == i1, ninf, le)
    m2 = jnp.max(le2, axis=1, keepdims=True)
    i2 = first(le2 == m2)
    ee = jnp.exp(le - m1)
    pe = ee / jnp.sum(ee, axis=1, keepdims=True)
    p1 = jnp.sum(jnp.where(colf == i1, pe, 0.0), axis=1, keepdims=True)
    p2 = jnp.sum(jnp.where(colf == i2, pe, 0.0), axis=1, keepdims=True)
    den = p1 + p2
    return jnp.where(colf == i1, p1 / den * pg_top,
                     jnp.where(colf == i2, p2 / den * pg_top, 0.0))


def _out_xattn_kernel(n_parts, *refs):
    h_ref = refs[0]
    part_refs = refs[1:1 + n_parts]
    w_refs = refs[1 + n_parts:1 + 2 * n_parts]
    (gx_ref, wq_ref, k_ref, v_ref, wo_ref, gf_ref, wr_hi_ref, wr_lo_ref, br_ref,
     h2_ref, xn_ref, gates_ref) = refs[1 + 2 * n_parts:]

    h1 = h_ref[...]
    for p_ref, w_ref in zip(part_refs, w_refs):
        h1 = h1 + _dot(p_ref[...], w_ref[...])

    xn = _rmsnorm(h1, gx_ref[...]).astype(BF16)
    q = (_dot(xn, wq_ref[...]) * (XA_HEAD_DIM ** -0.5)).astype(BF16)
    heads = []
    for hd in range(XA_HEADS):
        sl = slice(hd * XA_HEAD_DIM, (hd + 1) * XA_HEAD_DIM)
        s = lax.dot_general(q[:, sl], k_ref[0, :, sl], (((1,), (1,)), ((), ())),
                            preferred_element_type=F32)
        e = jnp.exp(s - jnp.max(s, axis=1, keepdims=True))
        p = e / jnp.sum(e, axis=1, keepdims=True)
        heads.append(_dot(p.astype(BF16), v_ref[0, :, sl]).astype(BF16))
    h2 = h1 + _dot(jnp.concatenate(heads, axis=1), wo_ref[...])
    h2_ref[...] = h2

    xf = _rmsnorm(h2, gf_ref[...])
    x_hi = xf.astype(BF16)
    x_lo = (xf - x_hi.astype(F32)).astype(BF16)
    xn_ref[...] = x_hi
    logits = (_dot(x_hi, wr_hi_ref[...]) + _dot(x_lo, wr_hi_ref[...])
              + _dot(x_hi, wr_lo_ref[...]) + br_ref[...])
    gates_ref[...] = _router_gates(logits)


def _out_xattn(h, parts, weights, gx, wq, k_all, v_all, wo, gf, wr_hi, wr_lo, br, layer):
    tm = ROW_TILE
    row = lambda n: pl.BlockSpec((tm, n), lambda i: (i, 0))
    kv_spec = pl.BlockSpec((1, MEM_LEN, D_MODEL), lambda i: (layer, 0, 0))
    in_specs = ([row(D_MODEL)] + [row(p.shape[1]) for p in parts]
                + [_full(w.shape) for w in weights]
                + [_full((1, D_MODEL)), _full((D_MODEL, D_MODEL)), kv_spec, kv_spec,
                   _full((D_MODEL, D_MODEL)), _full((1, D_MODEL)),
                   _full((D_MODEL, LANES)), _full((D_MODEL, LANES)), _full((1, LANES))])
    return pl.pallas_call(
        functools.partial(_out_xattn_kernel, len(parts)),
        grid=(SEQ // tm,),
        in_specs=in_specs,
        out_specs=[row(D_MODEL), row(D_MODEL), row(LANES)],
        out_shape=[jax.ShapeDtypeStruct((SEQ, D_MODEL), F32),
                   jax.ShapeDtypeStruct((SEQ, D_MODEL), BF16),
                   jax.ShapeDtypeStruct((SEQ, LANES), F32)],
        compiler_params=_params("parallel"),
        name="out_xattn_router",
    )(h, *parts, *weights, gx, wq, k_all, v_all, wo, gf, wr_hi, wr_lo, br)


def _moe_kernel(final_norm, xn_ref, gates_ref, h_ref, wg_ref, wu_ref, wd_ref, gfin_ref,
                o_ref, acc_ref):
    grp = pl.program_id(1)

    @pl.when(grp == 0)
    def _():
        acc_ref[...] = jnp.zeros_like(acc_ref)

    xn = xn_ref[...]
    gates = gates_ref[...]
    col = lax.broadcasted_iota(jnp.int32, gates.shape, 1)
    for e in range(EXPERTS_PER_GROUP):
        gate = jnp.sum(jnp.where(col == grp * EXPERTS_PER_GROUP + e, gates, 0.0),
                       axis=1, keepdims=True)
        hg = _dot(xn, wg_ref[e])
        hu = _dot(xn, wu_ref[e])
        act = (hg * jax.nn.sigmoid(hg)) * hu * gate
        acc_ref[...] += _dot(act.astype(BF16), wd_ref[e])

    @pl.when(grp == N_GROUPS - 1)
    def _():
        out = h_ref[...] + acc_ref[...]
        if final_norm:
            out = _rmsnorm(out, gfin_ref[...])
        o_ref[...] = out


def _moe(xn, gates, h, wg, wu, wd, gfin, final_norm):
    tm = ROW_TILE
    row = lambda n: pl.BlockSpec((tm, n), lambda i, g: (i, 0))
    up_spec = pl.BlockSpec((EXPERTS_PER_GROUP, D_MODEL, D_EXPERT), lambda i, g: (g, 0, 0))
    down_spec = pl.BlockSpec((EXPERTS_PER_GROUP, D_EXPERT, D_MODEL), lambda i, g: (g, 0, 0))
    return pl.pallas_call(
        functools.partial(_moe_kernel, final_norm),
        grid=(SEQ // tm, N_GROUPS),
        in_specs=[row(D_MODEL), row(LANES), row(D_MODEL), up_spec, up_spec, down_spec,
                  pl.BlockSpec((1, D_MODEL), lambda i, g: (0, 0))],
        out_specs=row(D_MODEL),
        out_shape=jax.ShapeDtypeStruct((SEQ, D_MODEL), F32),
        scratch_shapes=[pltpu.VMEM((tm, D_MODEL), F32)],
        compiler_params=_params("parallel", "arbitrary"),
        name="moe_experts",
    )(xn, gates, h, wg, wu, wd, gfin)


def _l1_mixer_kernel(x_ref, g_ref, w_in_ref, vg_ref, vb_ref, ws_ref, bs_ref, o_ref):
    tm = x_ref.shape[0]
    xn = _rmsnorm(x_ref[...], g_ref[...]).astype(BF16)
    p = _dot(xn, w_in_ref[...])
    p = 0.5 * p * (1.0 + lax.erf(p * (2.0 ** -0.5)))
    u = p[:, :D_MODEL]
    v = _layernorm(p[:, D_MODEL:], vg_ref[...], vb_ref[...]).astype(BF16)
    t_idx = lax.broadcasted_iota(jnp.int32, (GM_CHUNK, GM_CHUNK), 0)
    s_idx = lax.broadcasted_iota(jnp.int32, (GM_CHUNK, GM_CHUNK), 1)
    for g in range(GM_GROUPS):
        cols = slice(g * LANES, (g + 1) * LANES)
        wmix = jnp.where(t_idx >= s_idx, ws_ref[g], 0.0).astype(BF16)
        for c in range(tm // GM_CHUNK):
            rows = slice(c * GM_CHUNK, (c + 1) * GM_CHUNK)
            mixed = _dot(wmix, v[rows, cols]) + bs_ref[g]
            o_ref[rows, cols] = (u[rows, cols] * mixed).astype(BF16)


def _l1_mixer(x, g, w_in, vg, vb, ws, bs_b):
    tm = ROW_TILE
    return pl.pallas_call(
        _l1_mixer_kernel,
        grid=(SEQ // tm,),
        in_specs=[pl.BlockSpec((tm, D_MODEL), lambda i: (i, 0)), _full((1, D_MODEL)),
                  _full((D_MODEL, 2 * D_MODEL)), _full((1, D_MODEL)), _full((1, D_MODEL)),
                  _full((GM_GROUPS, GM_CHUNK, GM_CHUNK)), _full((GM_GROUPS, GM_CHUNK, LANES))],
        out_specs=pl.BlockSpec((tm, D_MODEL), lambda i: (i, 0)),
        out_shape=jax.ShapeDtypeStruct((SEQ, D_MODEL), BF16),
        compiler_params=_params("parallel"),
        name="l1_mixer",
    )(x, g, w_in, vg, vb, ws, bs_b)


def _router_weights(we, wg, be, bg):
    pad = LANES - N_EXPERTS - N_GROUPS
    w = jnp.concatenate([we, wg, jnp.zeros((D_MODEL, pad), F32)], axis=1)
    b = jnp.concatenate([be, bg, jnp.zeros((pad,), F32)])[None, :]
    w_hi = w.astype(BF16)
    w_lo = (w - w_hi.astype(F32)).astype(BF16)
    return w_hi, w_lo, b


def kernel(x, mem, mem_norm_g, norm_mix, norm_xa, norm_ffn, final_norm_g, ab_w_in, ab_conv_w, ab_conv_b, ab_cnorm_g, ab_cnorm_b, ab_w_out, c_w_in, c_vnorm_g, c_vnorm_b, c_ws, c_bs, c_w_out, xa_wq, xa_wk, xa_wv, xa_wo, rt_wg, rt_bg, rt_we, rt_be, ex_w_gate, ex_w_up, ex_w_down):
    assert x.shape == (1, SEQ, D_MODEL) and mem.shape == (1, MEM_LEN, D_MODEL)
    bf = lambda a: a.astype(BF16)
    r1 = lambda a: a.reshape(1, -1)
    h = x[0]

    k_all, v_all = _mem_kv(mem[0], r1(mem_norm_g), bf(xa_wk), bf(xa_wv))

    def tail(h, parts, weights, i, final_norm):
        wr_hi, wr_lo, br = _router_weights(rt_we[i], rt_wg[i], rt_be[i], rt_bg[i])
        h2, xn, gates = _out_xattn(h, parts, weights, r1(norm_xa[i]), bf(xa_wq[i]), k_all, v_all,
                                   bf(xa_wo[i]), r1(norm_ffn[i]), wr_hi, wr_lo, br, i)
        return _moe(xn, gates, h2, bf(ex_w_gate[i]), bf(ex_w_up[i]), bf(ex_w_down[i]),
                    r1(final_norm_g), final_norm)

    w_in = ab_w_in[0]
    q_lo, k_lo, v_lo = 2 * CONV_CH, 2 * CONV_CH + SB_DIM, 2 * CONV_CH + 2 * SB_DIM
    hglu, q, kt, v = _l0_inproj(h, r1(norm_mix[0]), bf(w_in[:, :q_lo]), bf(w_in[:, q_lo:k_lo]),
                                bf(w_in[:, k_lo:v_lo].T), bf(w_in[:, v_lo:]))
    conv_out = _conv(hglu, ab_conv_w[0], r1(ab_conv_b[0]), r1(ab_cnorm_g[0]), r1(ab_cnorm_b[0]))
    sb_out = _sb_attention(q, kt, v)
    w_out = bf(ab_w_out[0])
    h = tail(h, [conv_out, sb_out], [w_out[:CONV_CH], w_out[CONV_CH:]], 0, False)

    bs_b = jnp.broadcast_to(c_bs[0][:, :, None], (GM_GROUPS, GM_CHUNK, LANES))
    gated = _l1_mixer(h, r1(norm_mix[1]), bf(c_w_in[0]), r1(c_vnorm_g[0]), r1(c_vnorm_b[0]),
                      c_ws[0], bs_b)
    h = tail(h, [gated], [bf(c_w_out[0])], 1, True)
    return h[None]
```

```python
import functools

import jax
import jax.numpy as jnp
from jax import lax
from jax.experimental import pallas as pl
from jax.experimental.pallas import tpu as pltpu

D_MODEL = 1024
SEQ = 16384
MEM_LEN = 256
EPS = 1e-6
CONV_CH = 512
CONV_WIDTH = 31
SB_HEADS = 8
SB_HEAD_DIM = 64
SB_DIM = 512
GM_GROUPS = 8
GM_CHUNK = 128
XA_HEADS = 4
XA_HEAD_DIM = 256
N_GROUPS = 4
EXPERTS_PER_GROUP = 4
N_EXPERTS = 16
D_EXPERT = 256

LANES = 128
ROW_TILE = 512
SB_TQ = 256
SB_TK = 256
SB_LOG_UNDERFLOW = -104.0
CONV_HALO = 32
CONV_ROWS = 32
VMEM_LIMIT = 56 * 1024 * 1024

BF16 = jnp.bfloat16
F32 = jnp.float32


def _params(*semantics):
    return pltpu.CompilerParams(dimension_semantics=semantics, vmem_limit_bytes=VMEM_LIMIT)


def _dot(a, b):
    return jnp.dot(a, b, preferred_element_type=F32)


def _rmsnorm(x, g):
    return x * lax.rsqrt(jnp.mean(x * x, axis=-1, keepdims=True) + EPS) * g


def _layernorm(x, g, b):
    mu = jnp.mean(x, axis=-1, keepdims=True)
    xc = x - mu
    var = jnp.mean(xc * xc, axis=-1, keepdims=True)
    return xc * lax.rsqrt(var + EPS) * g + b


def _full(shape):
    return pl.BlockSpec(shape, lambda *_: (0,) * len(shape))


def _mem_kv_kernel(mem_ref, g_ref, wk_ref, wv_ref, k_ref, v_ref):
    memn = _rmsnorm(mem_ref[...], g_ref[...]).astype(BF16)
    k_ref[0] = _dot(memn, wk_ref[0]).astype(BF16)
    v_ref[0] = _dot(memn, wv_ref[0]).astype(BF16)


def _mem_kv(mem, g, wk, wv):
    depth = wk.shape[0]
    wspec = pl.BlockSpec((1, D_MODEL, D_MODEL), lambda i: (i, 0, 0))
    ospec = pl.BlockSpec((1, MEM_LEN, D_MODEL), lambda i: (i, 0, 0))
    return pl.pallas_call(
        _mem_kv_kernel,
        grid=(depth,),
        in_specs=[_full((MEM_LEN, D_MODEL)), _full((1, D_MODEL)), wspec, wspec],
        out_specs=[ospec, ospec],
        out_shape=[jax.ShapeDtypeStruct((depth, MEM_LEN, D_MODEL), BF16)] * 2,
        compiler_params=_params("arbitrary"),
        name="mem_kv",
    )(mem, g, wk, wv)


def _l0_inproj_kernel(x_ref, g_ref, w_ag_ref, w_q_ref, w_kt_ref, w_v_ref,
                      h_ref, q_ref, kt_ref, v_ref):
    xn = _rmsnorm(x_ref[...], g_ref[...]).astype(BF16)
    ag = _dot(xn, w_ag_ref[...])
    h_ref[...] = ag[:, :CONV_CH] * jax.nn.sigmoid(ag[:, CONV_CH:])
    q_ref[...] = (_dot(xn, w_q_ref[...]) * (SB_HEAD_DIM ** -0.5)).astype(BF16)
    kt_ref[...] = lax.dot_general(w_kt_ref[...], xn, (((1,), (1,)), ((), ())),
                                  preferred_element_type=F32).astype(BF16)
    v_ref[...] = _dot(xn, w_v_ref[...]).astype(BF16)


def _l0_inproj(x, g, w_ag, w_q, w_kt, w_v):
    tm = ROW_TILE
    row = lambda n: pl.BlockSpec((tm, n), lambda i: (i, 0))
    return pl.pallas_call(
        _l0_inproj_kernel,
        grid=(SEQ // tm,),
        in_specs=[row(D_MODEL), _full((1, D_MODEL)), _full((D_MODEL, 2 * CONV_CH)),
                  _full((D_MODEL, SB_DIM)), _full((SB_DIM, D_MODEL)), _full((D_MODEL, SB_DIM))],
        out_specs=[row(CONV_CH), row(SB_DIM), pl.BlockSpec((SB_DIM, tm), lambda i: (0, i)),
                   row(SB_DIM)],
        out_shape=[jax.ShapeDtypeStruct((SEQ, CONV_CH), F32),
                   jax.ShapeDtypeStruct((SEQ, SB_DIM), BF16),
                   jax.ShapeDtypeStruct((SB_DIM, SEQ), BF16),
                   jax.ShapeDtypeStruct((SEQ, SB_DIM), BF16)],
        compiler_params=_params("parallel"),
        name="l0_inproj",
    )(x, g, w_ag, w_q, w_kt, w_v)


def _conv_kernel(prev_ref, cur_ref, cw_ref, cb_ref, lg_ref, lb_ref, o_ref, ext_ref):
    tm = cur_ref.shape[0]
    ext_ref[0:CONV_HALO, :] = jnp.where(pl.program_id(0) > 0, prev_ref[...], 0.0)
    ext_ref[CONV_HALO:, :] = cur_ref[...]
    first_tap = CONV_HALO - (CONV_WIDTH - 1)

    def chunk(c, _):
        r0 = pl.multiple_of(c * CONV_ROWS, CONV_ROWS)
        win = ext_ref[pl.ds(r0, CONV_ROWS + CONV_HALO), :]
        acc = jnp.zeros((CONV_ROWS, CONV_CH), F32) + cb_ref[...]
        for w in range(CONV_WIDTH):
            acc = acc + win[first_tap + w:first_tap + w + CONV_ROWS, :] * cw_ref[w:w + 1, :]
        y = _layernorm(acc, lg_ref[...], lb_ref[...])
        o_ref[pl.ds(r0, CONV_ROWS), :] = (y * jax.nn.sigmoid(y)).astype(BF16)
        return 0

    lax.fori_loop(0, tm // CONV_ROWS, chunk, 0)


def _conv(h, cw, cb, lg, lb):
    tm = ROW_TILE
    halo_blocks = tm // CONV_HALO
    return pl.pallas_call(
        _conv_kernel,
        grid=(SEQ // tm,),
        in_specs=[pl.BlockSpec((CONV_HALO, CONV_CH),
                               lambda i: (jnp.maximum(i * halo_blocks - 1, 0), 0)),
                  pl.BlockSpec((tm, CONV_CH), lambda i: (i, 0)),
                  _full((CONV_WIDTH, CONV_CH)), _full((1, CONV_CH)),
                  _full((1, CONV_CH)), _full((1, CONV_CH))],
        out_specs=pl.BlockSpec((tm, CONV_CH), lambda i: (i, 0)),
        out_shape=jax.ShapeDtypeStruct((SEQ, CONV_CH), BF16),
        scratch_shapes=[pltpu.VMEM((tm + CONV_HALO, CONV_CH), F32)],
        compiler_params=_params("parallel"),
        name="conv_ln_silu",
    )(h, h, cw, cb, lg, lb)


def _sb_kernel(q_ref, kt_ref, v_ref, o_ref, acc_ref, carry_ref):
    qi = pl.program_id(1)
    tq, tk = SB_TQ, SB_TK
    lane = lax.broadcasted_iota(jnp.int32, (tq, LANES), 1)
    q2 = q_ref[...]
    zero = jnp.zeros_like(q2)
    qh = (jnp.where(lane < SB_HEAD_DIM, q2, zero), jnp.where(lane >= SB_HEAD_DIM, q2, zero))
    jj = lax.broadcasted_iota(jnp.int32, (tk, tk), 0)
    ss = lax.broadcasted_iota(jnp.int32, (tk, tk), 1)
    minus_later = jnp.where(jj > ss, -1.0, 0.0).astype(BF16)

    acc_ref[...] = jnp.zeros_like(acc_ref)
    carry_ref[...] = jnp.zeros_like(carry_ref)

    def block(kb, valid):
        k0 = pl.multiple_of(kb * tk, tk)
        kt = kt_ref[:, pl.ds(k0, tk)]
        vv = v_ref[pl.ds(k0, tk), :]
        for h in range(2):
            z = _dot(qh[h], kt)
            sp = jnp.maximum(z, 0.0) + jnp.log(1.0 + jnp.exp(-jnp.abs(z)))
            if valid is not None:
                sp = jnp.where(valid, sp, 0.0)
            hi = sp.astype(BF16)
            lo = (sp - hi.astype(F32)).astype(BF16)
            after = _dot(hi, minus_later) + _dot(lo, minus_later)
            w = jnp.exp((z - sp) + after)
            if valid is not None:
                w = jnp.where(valid, w, 0.0)
            carry = carry_ref[h]
            acc_ref[h] += jnp.exp(carry) * _dot(w.astype(BF16), vv)
            carry_ref[h] = carry - jnp.sum(sp, axis=1, keepdims=True)

    row = lax.broadcasted_iota(jnp.int32, (tq, tk), 0)
    col = lax.broadcasted_iota(jnp.int32, (tq, tk), 1)
    block(qi, col < row)

    def alive():
        return jnp.max(carry_ref[...]) > SB_LOG_UNDERFLOW

    def cond(state):
        kb, go = state
        return (kb >= 0) & go

    def body(state):
        kb, _ = state
        block(kb, None)
        return kb - 1, alive()

    lax.while_loop(cond, body, (qi - 1, alive()))
    o_ref[...] = jnp.where(lane < SB_HEAD_DIM, acc_ref[0], acc_ref[1]).astype(BF16)


def _sb_attention(q, kt, v):
    tq = SB_TQ
    return pl.pallas_call(
        _sb_kernel,
        grid=(SB_DIM // LANES, SEQ // tq),
        in_specs=[pl.BlockSpec((tq, LANES), lambda p, i: (i, p)),
                  pl.BlockSpec((LANES, SEQ), lambda p, i: (p, 0)),
                  pl.BlockSpec((SEQ, LANES), lambda p, i: (0, p))],
        out_specs=pl.BlockSpec((tq, LANES), lambda p, i: (i, p)),
        out_shape=jax.ShapeDtypeStruct((SEQ, SB_DIM), BF16),
        scratch_shapes=[pltpu.VMEM((2, tq, LANES), F32), pltpu.VMEM((2, tq, LANES), F32)],
        compiler_params=_params("parallel", "parallel"),
        name="sb_attention",
    )(q, kt, v)


def _router_gates(logits):
    col = lax.broadcasted_iota(jnp.int32, logits.shape, 1)
    colf = col.astype(F32)
    ninf = -jnp.inf
    first = lambda hit: jnp.min(jnp.where(hit, colf, float(LANES)), axis=1, keepdims=True)

    is_group = (col >= N_EXPERTS) & (col < N_EXPERTS + N_GROUPS)
    lg = jnp.where(is_group, logits, ninf)
    gmax = jnp.max(lg, axis=1, keepdims=True)
    pg_top = 1.0 / jnp.sum(jnp.exp(lg - gmax), axis=1, keepdims=True)
    g_idx = first(lg == gmax) - float(N_EXPERTS)

    in_group = (col < N_EXPERTS) & ((col // EXPERTS_PER_GROUP).astype(F32) == g_idx)
    le = jnp.where(in_group, logits, ninf)
    m1 = jnp.max(le, axis=1, keepdims=True)
    i1 = first(le == m1)
    le2 = jnp.where(colf == i1, ninf, le)
    m2 = jnp.max(le2, axis=1, keepdims=True)
    i2 = first(le2 == m2)
    ee = jnp.exp(le - m1)
    pe = ee / jnp.sum(ee, axis=1, keepdims=True)
    p1 = jnp.sum(jnp.where(colf == i1, pe, 0.0), axis=1, keepdims=True)
    p2 = jnp.sum(jnp.where(colf == i2, pe, 0.0), axis=1, keepdims=True)
    den = p1 + p2
    return jnp.where(colf == i1, p1 / den * pg_top,
                     jnp.where(colf == i2, p2 / den * pg_top, 0.0))


def _out_xattn_kernel(n_parts, *refs):
    h_ref = refs[0]
    part_refs = refs[1:1 + n_parts]
    w_refs = refs[1 + n_parts:1 + 2 * n_parts]
    (gx_ref, wq_ref, k_ref, v_ref, wo_ref, gf_ref, wr_hi_ref, wr_lo_ref, br_ref,
     h2_ref, xn_ref, gates_ref) = refs[1 + 2 * n_parts:]

    h1 = h_ref[...]
    for p_ref, w_ref in zip(part_refs, w_refs):
        h1 = h1 + _dot(p_ref[...], w_ref[...])

    xn = _rmsnorm(h1, gx_ref[...]).astype(BF16)
    q = (_dot(xn, wq_ref[...]) * (XA_HEAD_DIM ** -0.5)).astype(BF16)
    heads = []
    for hd in range(XA_HEADS):
        sl = slice(hd * XA_HEAD_DIM, (hd + 1) * XA_HEAD_DIM)
        s = lax.dot_general(q[:, sl], k_ref[0, :, sl], (((1,), (1,)), ((), ())),
                            preferred_element_type=F32)
        e = jnp.exp(s - jnp.max(s, axis=1, keepdims=True))
        p = e / jnp.sum(e, axis=1, keepdims=True)
        heads.append(_dot(p.astype(BF16), v_ref[0, :, sl]).astype(BF16))
    h2 = h1 + _dot(jnp.concatenate(heads, axis=1), wo_ref[...])
    h2_ref[...] = h2

    xf = _rmsnorm(h2, gf_ref[...])
    x_hi = xf.astype(BF16)
    x_lo = (xf - x_hi.astype(F32)).astype(BF16)
    xn_ref[...] = x_hi
    logits = (_dot(x_hi, wr_hi_ref[...]) + _dot(x_lo, wr_hi_ref[...])
              + _dot(x_hi, wr_lo_ref[...]) + br_ref[...])
    gates_ref[...] = _router_gates(logits)


def _out_xattn(h, parts, weights, gx, wq, k_all, v_all, wo, gf, wr_hi, wr_lo, br, layer):
    tm = ROW_TILE
    row = lambda n: pl.BlockSpec((tm, n), lambda i: (i, 0))
    kv_spec = pl.BlockSpec((1, MEM_LEN, D_MODEL), lambda i: (layer, 0, 0))
    in_specs = ([row(D_MODEL)] + [row(p.shape[1]) for p in parts]
                + [_full(w.shape) for w in weights]
                + [_full((1, D_MODEL)), _full((D_MODEL, D_MODEL)), kv_spec, kv_spec,
                   _full((D_MODEL, D_MODEL)), _full((1, D_MODEL)),
                   _full((D_MODEL, LANES)), _full((D_MODEL, LANES)), _full((1, LANES))])
    return pl.pallas_call(
        functools.partial(_out_xattn_kernel, len(parts)),
        grid=(SEQ // tm,),
        in_specs=in_specs,
        out_specs=[row(D_MODEL), row(D_MODEL), row(LANES)],
        out_shape=[jax.ShapeDtypeStruct((SEQ, D_MODEL), F32),
                   jax.ShapeDtypeStruct((SEQ, D_MODEL), BF16),
                   jax.ShapeDtypeStruct((SEQ, LANES), F32)],
        compiler_params=_params("parallel"),
        name="out_xattn_router",
    )(h, *parts, *weights, gx, wq, k_all, v_all, wo, gf, wr_hi, wr_lo, br)


def _moe_kernel(final_norm, xn_ref, gates_ref, h_ref, wg_ref, wu_ref, wd_ref, gfin_ref,
                o_ref, acc_ref):
    grp = pl.program_id(1)

    @pl.when(grp == 0)
    def _():
        acc_ref[...] = jnp.zeros_like(acc_ref)

    xn = xn_ref[...]
    gates = gates_ref[...]
    col = lax.broadcasted_iota(jnp.int32, gates.shape, 1)
    for e in range(EXPERTS_PER_GROUP):
        gate = jnp.sum(jnp.where(col == grp * EXPERTS_PER_GROUP + e, gates, 0.0),
                       axis=1, keepdims=True)
        hg = _dot(xn, wg_ref[e])
        hu = _dot(xn, wu_ref[e])
        act = (hg * jax.nn.sigmoid(hg)) * hu * gate
        acc_ref[...] += _dot(act.astype(BF16), wd_ref[e])

    @pl.when(grp == N_GROUPS - 1)
    def _():
        out = h_ref[...] + acc_ref[...]
        if final_norm:
            out = _rmsnorm(out, gfin_ref[...])
        o_ref[...] = out


def _moe(xn, gates, h, wg, wu, wd, gfin, final_norm):
    tm = ROW_TILE
    row = lambda n: pl.BlockSpec((tm, n), lambda i, g: (i, 0))
    up_spec = pl.BlockSpec((EXPERTS_PER_GROUP, D_MODEL, D_EXPERT), lambda i, g: (g, 0, 0))
    down_spec = pl.BlockSpec((EXPERTS_PER_GROUP, D_EXPERT, D_MODEL), lambda i, g: (g, 0, 0))
    return pl.pallas_call(
        functools.partial(_moe_kernel, final_norm),
        grid=(SEQ // tm, N_GROUPS),
        in_specs=[row(D_MODEL), row(LANES), row(D_MODEL), up_spec, up_spec, down_spec,
                  pl.BlockSpec((1, D_MODEL), lambda i, g: (0, 0))],
        out_specs=row(D_MODEL),
        out_shape=jax.ShapeDtypeStruct((SEQ, D_MODEL), F32),
        scratch_shapes=[pltpu.VMEM((tm, D_MODEL), F32)],
        compiler_params=_params("parallel", "arbitrary"),
        name="moe_experts",
    )(xn, gates, h, wg, wu, wd, gfin)


def _l1_mixer_kernel(x_ref, g_ref, w_in_ref, vg_ref, vb_ref, ws_ref, bs_ref, o_ref):
    tm = x_ref.shape[0]
    xn = _rmsnorm(x_ref[...], g_ref[...]).astype(BF16)
    p = _dot(xn, w_in_ref[...])
    p = 0.5 * p * (1.0 + lax.erf(p * (2.0 ** -0.5)))
    u = p[:, :D_MODEL]
    v = _layernorm(p[:, D_MODEL:], vg_ref[...], vb_ref[...]).astype(BF16)
    t_idx = lax.broadcasted_iota(jnp.int32, (GM_CHUNK, GM_CHUNK), 0)
    s_idx = lax.broadcasted_iota(jnp.int32, (GM_CHUNK, GM_CHUNK), 1)
    for g in range(GM_GROUPS):
        cols = slice(g * LANES, (g + 1) * LANES)
        wmix = jnp.where(t_idx >= s_idx, ws_ref[g], 0.0).astype(BF16)
        for c in range(tm // GM_CHUNK):
            rows = slice(c * GM_CHUNK, (c + 1) * GM_CHUNK)
            mixed = _dot(wmix, v[rows, cols]) + bs_ref[g]
            o_ref[rows, cols] = (u[rows, cols] * mixed).astype(BF16)


def _l1_mixer(x, g, w_in, vg, vb, ws, bs_b):
    tm = ROW_TILE
    return pl.pallas_call(
        _l1_mixer_kernel,
        grid=(SEQ // tm,),
        in_specs=[pl.BlockSpec((tm, D_MODEL), lambda i: (i, 0)), _full((1, D_MODEL)),
                  _full((D_MODEL, 2 * D_MODEL)), _full((1, D_MODEL)), _full((1, D_MODEL)),
                  _full((GM_GROUPS, GM_CHUNK, GM_CHUNK)), _full((GM_GROUPS, GM_CHUNK, LANES))],
        out_specs=pl.BlockSpec((tm, D_MODEL), lambda i: (i, 0)),
        out_shape=jax.ShapeDtypeStruct((SEQ, D_MODEL), BF16),
        compiler_params=_params("parallel"),
        name="l1_mixer",
    )(x, g, w_in, vg, vb, ws, bs_b)


def _router_weights(we, wg, be, bg):
    pad = LANES - N_EXPERTS - N_GROUPS
    w = jnp.concatenate([we, wg, jnp.zeros((D_MODEL, pad), F32)], axis=1)
    b = jnp.concatenate([be, bg, jnp.zeros((pad,), F32)])[None, :]
    w_hi = w.astype(BF16)
    w_lo = (w - w_hi.astype(F32)).astype(BF16)
    return w_hi, w_lo, b


def kernel(x, mem, mem_norm_g, norm_mix, norm_xa, norm_ffn, final_norm_g, ab_w_in, ab_conv_w, ab_conv_b, ab_cnorm_g, ab_cnorm_b, ab_w_out, c_w_in, c_vnorm_g, c_vnorm_b, c_ws, c_bs, c_w_out, xa_wq, xa_wk, xa_wv, xa_wo, rt_wg, rt_bg, rt_we, rt_be, ex_w_gate, ex_w_up, ex_w_down):
    assert x.shape == (1, SEQ, D_MODEL) and mem.shape == (1, MEM_LEN, D_MODEL)
    bf = lambda a: a.astype(BF16)
    r1 = lambda a: a.reshape(1, -1)
    h = x[0]

    k_all, v_all = _mem_kv(mem[0], r1(mem_norm_g), bf(xa_wk), bf(xa_wv))

    def tail(h, parts, weights, i, final_norm):
        wr_hi, wr_lo, br = _router_weights(rt_we[i], rt_wg[i], rt_be[i], rt_bg[i])
        h2, xn, gates = _out_xattn(h, parts, weights, r1(norm_xa[i]), bf(xa_wq[i]), k_all, v_all,
                                   bf(xa_wo[i]), r1(norm_ffn[i]), wr_hi, wr_lo, br, i)
        return _moe(xn, gates, h2, bf(ex_w_gate[i]), bf(ex_w_up[i]), bf(ex_w_down[i]),
                    r1(final_norm_g), final_norm)

    w_in = ab_w_in[0]
    q_lo, k_lo, v_lo = 2 * CONV_CH, 2 * CONV_CH + SB_DIM, 2 * CONV_CH + 2 * SB_DIM
    hglu, q, kt, v = _l0_inproj(h, r1(norm_mix[0]), bf(w_in[:, :q_lo]), bf(w_in[:, q_lo:k_lo]),
                                bf(w_in[:, k_lo:v_lo].T), bf(w_in[:, v_lo:]))
    conv_out = _conv(hglu, ab_conv_w[0], r1(ab_conv_b[0]), r1(ab_cnorm_g[0]), r1(ab_cnorm_b[0]))
    sb_out = _sb_attention(q, kt, v)
    w_out = bf(ab_w_out[0])
    h = tail(h, [conv_out, sb_out], [w_out[:CONV_CH], w_out[CONV_CH:]], 0, False)

    bs_b = jnp.broadcast_to(c_bs[0][:, :, None], (GM_GROUPS, GM_CHUNK, LANES))
    gated = _l1_mixer(h, r1(norm_mix[1]), bf(c_w_in[0]), r1(c_vnorm_g[0]), r1(c_vnorm_b[0]),
                      c_ws[0], bs_b)
    h = tail(h, [gated], [bf(c_w_out[0])], 1, True)
    return h[None]
```

```python
import functools

import jax
import jax.numpy as jnp
from jax import lax
from jax.experimental import pallas as pl
from jax.experimental.pallas import tpu as pltpu

D_MODEL = 1024
SEQ = 16384
MEM_LEN = 256
EPS = 1e-6
CONV_CH = 512
CONV_WIDTH = 31
SB_HEADS = 8
SB_HEAD_DIM = 64
SB_DIM = 512
GM_GROUPS = 8
GM_CHUNK = 128
XA_HEADS = 4
XA_HEAD_DIM = 256
N_GROUPS = 4
EXPERTS_PER_GROUP = 4
N_EXPERTS = 16
D_EXPERT = 256

LANES = 128
SUBLANES = 8
ROW_TILE = 512
SB_TQ = 256
SB_TK = 256
SB_LOG_UNDERFLOW = -104.0
CONV_HALO = 32
CONV_ROWS = 64
VMEM_LIMIT = 56 * 1024 * 1024

BF16 = jnp.bfloat16
F32 = jnp.float32


def _params(*semantics):
    return pltpu.CompilerParams(dimension_semantics=semantics, vmem_limit_bytes=VMEM_LIMIT)


def _dot(a, b):
    return jnp.dot(a, b, preferred_element_type=F32)


def _rmsnorm(x, g):
    return x * lax.rsqrt(jnp.mean(x * x, axis=-1, keepdims=True) + EPS) * g


def _layernorm(x, g, b):
    mu = jnp.mean(x, axis=-1, keepdims=True)
    xc = x - mu
    var = jnp.mean(xc * xc, axis=-1, keepdims=True)
    return xc * lax.rsqrt(var + EPS) * g + b


def _full(shape):
    return pl.BlockSpec(shape, lambda *_: (0,) * len(shape))


def _mem_kv_kernel(mem_ref, g_ref, wk_ref, wv_ref, k_ref, v_ref):
    memn = _rmsnorm(mem_ref[...], g_ref[...]).astype(BF16)
    k_ref[0] = _dot(memn, wk_ref[0]).astype(BF16)
    v_ref[0] = _dot(memn, wv_ref[0]).astype(BF16)


def _mem_kv(mem, g, wk, wv):
    depth = wk.shape[0]
    wspec = pl.BlockSpec((1, D_MODEL, D_MODEL), lambda i: (i, 0, 0))
    ospec = pl.BlockSpec((1, MEM_LEN, D_MODEL), lambda i: (i, 0, 0))
    return pl.pallas_call(
        _mem_kv_kernel,
        grid=(depth,),
        in_specs=[_full((MEM_LEN, D_MODEL)), _full((1, D_MODEL)), wspec, wspec],
        out_specs=[ospec, ospec],
        out_shape=[jax.ShapeDtypeStruct((depth, MEM_LEN, D_MODEL), BF16)] * 2,
        compiler_params=_params("arbitrary"),
        name="mem_kv",
    )(mem, g, wk, wv)


def _l0_inproj_kernel(x_ref, g_ref, w_ag_ref, w_q_ref, w_kt_ref, w_v_ref,
                      h_ref, q_ref, kt_ref, v_ref):
    xn = _rmsnorm(x_ref[...], g_ref[...]).astype(BF16)
    ag = _dot(xn, w_ag_ref[...])
    h_ref[...] = ag[:, :CONV_CH] * jax.nn.sigmoid(ag[:, CONV_CH:])
    q_ref[...] = (_dot(xn, w_q_ref[...]) * (SB_HEAD_DIM ** -0.5)).astype(BF16)
    kt_ref[...] = lax.dot_general(w_kt_ref[...], xn, (((1,), (1,)), ((), ())),
                                  preferred_element_type=F32).astype(BF16)
    v_ref[...] = _dot(xn, w_v_ref[...]).astype(BF16)


def _l0_inproj(x, g, w_ag, w_q, w_kt, w_v):
    tm = ROW_TILE
    row = lambda n: pl.BlockSpec((tm, n), lambda i: (i, 0))
    return pl.pallas_call(
        _l0_inproj_kernel,
        grid=(SEQ // tm,),
        in_specs=[row(D_MODEL), _full((1, D_MODEL)), _full((D_MODEL, 2 * CONV_CH)),
                  _full((D_MODEL, SB_DIM)), _full((SB_DIM, D_MODEL)), _full((D_MODEL, SB_DIM))],
        out_specs=[row(CONV_CH), row(SB_DIM), pl.BlockSpec((SB_DIM, tm), lambda i: (0, i)),
                   row(SB_DIM)],
        out_shape=[jax.ShapeDtypeStruct((SEQ, CONV_CH), F32),
                   jax.ShapeDtypeStruct((SEQ, SB_DIM), BF16),
                   jax.ShapeDtypeStruct((SB_DIM, SEQ), BF16),
                   jax.ShapeDtypeStruct((SEQ, SB_DIM), BF16)],
        compiler_params=_params("parallel"),
        name="l0_inproj",
    )(x, g, w_ag, w_q, w_kt, w_v)


def _conv_kernel(prev_ref, cur_ref, cw_ref, cb_ref, lg_ref, lb_ref, o_ref, ext_ref, y_ref):
    tm = cur_ref.shape[0]
    ext_ref[0:CONV_HALO, :] = jnp.where(pl.program_id(0) > 0, prev_ref[...], 0.0)
    ext_ref[CONV_HALO:, :] = cur_ref[...]
    first_tap = CONV_HALO - (CONV_WIDTH - 1)

    def chunk(c, _):
        r0 = pl.multiple_of(c * CONV_ROWS, CONV_ROWS)
        for j in range(CONV_CH // LANES):
            lanes = slice(j * LANES, (j + 1) * LANES)
            win = ext_ref[pl.ds(r0, CONV_ROWS + CONV_HALO), lanes]
            acc = jnp.zeros((CONV_ROWS, LANES), F32) + cb_ref[:, lanes]
            for sub in range(SUBLANES):
                offsets = [o for o in range(first_tap, first_tap + CONV_WIDTH)
                           if o % SUBLANES == sub]
                shifted = pltpu.roll(win, win.shape[0] - sub, axis=0) if sub else win
                for o in offsets:
                    tap = cw_ref[o - first_tap:o - first_tap + 1, lanes]
                    acc = acc + shifted[o - sub:o - sub + CONV_ROWS, :] * tap
            y_ref[pl.ds(r0, CONV_ROWS), lanes] = acc
        return 0

    lax.fori_loop(0, tm // CONV_ROWS, chunk, 0)
    y = _layernorm(y_ref[...], lg_ref[...], lb_ref[...])
    o_ref[...] = (y * jax.nn.sigmoid(y)).astype(BF16)


def _conv(h, cw, cb, lg, lb):
    tm = ROW_TILE
    halo_blocks = tm // CONV_HALO
    return pl.pallas_call(
        _conv_kernel,
        grid=(SEQ // tm,),
        in_specs=[pl.BlockSpec((CONV_HALO, CONV_CH),
                               lambda i: (jnp.maximum(i * halo_blocks - 1, 0), 0)),
                  pl.BlockSpec((tm, CONV_CH), lambda i: (i, 0)),
                  _full((CONV_WIDTH, CONV_CH)), _full((1, CONV_CH)),
                  _full((1, CONV_CH)), _full((1, CONV_CH))],
        out_specs=pl.BlockSpec((tm, CONV_CH), lambda i: (i, 0)),
        out_shape=jax.ShapeDtypeStruct((SEQ, CONV_CH), BF16),
        scratch_shapes=[pltpu.VMEM((tm + CONV_HALO, CONV_CH), F32),
                        pltpu.VMEM((tm, CONV_CH), F32)],
        compiler_params=_params("parallel"),
        name="conv_ln_silu",
    )(h, h, cw, cb, lg, lb)


def _sb_kernel(q_ref, kt_ref, v_ref, o_ref, acc_ref, carry_ref):
    qi = pl.program_id(1)
    tq, tk = SB_TQ, SB_TK
    lane = lax.broadcasted_iota(jnp.int32, (tq, LANES), 1)
    q2 = q_ref[...]
    zero = jnp.zeros_like(q2)
    qh = (jnp.where(lane < SB_HEAD_DIM, q2, zero), jnp.where(lane >= SB_HEAD_DIM, q2, zero))
    jj = lax.broadcasted_iota(jnp.int32, (tk, tk), 0)
    ss = lax.broadcasted_iota(jnp.int32, (tk, tk), 1)
    minus_later = jnp.where(jj > ss, -1.0, 0.0).astype(BF16)

    def block(h, kb, valid):
        k0 = pl.multiple_of(kb * tk, tk)
        z = _dot(qh[h], kt_ref[:, pl.ds(k0, tk)])
        sp = jnp.maximum(z, 0.0) + jnp.log(1.0 + jnp.exp(-jnp.abs(z)))
        if valid is not None:
            sp = jnp.where(valid, sp, 0.0)
        hi = sp.astype(BF16)
        lo = (sp - hi.astype(F32)).astype(BF16)
        after = _dot(hi, minus_later) + _dot(lo, minus_later)
        w = jnp.exp((z - sp) + after)
        if valid is not None:
            w = jnp.where(valid, w, 0.0)
        return _dot(w.astype(BF16), v_ref[pl.ds(k0, tk), :]), jnp.sum(sp, axis=1, keepdims=True)

    row = lax.broadcasted_iota(jnp.int32, (tq, tk), 0)
    col = lax.broadcasted_iota(jnp.int32, (tq, tk), 1)
    has_prev = qi > 0
    for h in range(2):
        pv_diag, sp_diag = block(h, qi, col < row)
        pv_prev, sp_prev = block(h, jnp.maximum(qi - 1, 0), None)
        acc_ref[h] = pv_diag + jnp.where(has_prev, jnp.exp(-sp_diag), 0.0) * pv_prev
        carry_ref[h] = jnp.broadcast_to(-(sp_diag + sp_prev), (tq, LANES))

    def alive():
        return jnp.max(carry_ref[...]) > SB_LOG_UNDERFLOW

    def cond(state):
        kb, go = state
        return (kb >= 0) & go

    def body(state):
        kb, _ = state
        for h in range(2):
            pv, sp_sum = block(h, kb, None)
            carry = carry_ref[h]
            acc_ref[h] += jnp.exp(carry) * pv
            carry_ref[h] = carry - sp_sum
        return kb - 1, alive()

    lax.while_loop(cond, body, (qi - 2, alive()))
    o_ref[...] = jnp.where(lane < SB_HEAD_DIM, acc_ref[0], acc_ref[1]).astype(BF16)


def _sb_attention(q, kt, v):
    tq = SB_TQ
    return pl.pallas_call(
        _sb_kernel,
        grid=(SB_DIM // LANES, SEQ // tq),
        in_specs=[pl.BlockSpec((tq, LANES), lambda p, i: (i, p)),
                  pl.BlockSpec((LANES, SEQ), lambda p, i: (p, 0)),
                  pl.BlockSpec((SEQ, LANES), lambda p, i: (0, p))],
        out_specs=pl.BlockSpec((tq, LANES), lambda p, i: (i, p)),
        out_shape=jax.ShapeDtypeStruct((SEQ, SB_DIM), BF16),
        scratch_shapes=[pltpu.VMEM((2, tq, LANES), F32), pltpu.VMEM((2, tq, LANES), F32)],
        compiler_params=_params("parallel", "parallel"),
        name="sb_attention",
    )(q, kt, v)


def _router_gates(logits):
    col = lax.broadcasted_iota(jnp.int32, logits.shape, 1)
    colf = col.astype(F32)
    ninf = -jnp.inf
    first = lambda hit: jnp.min(jnp.where(hit, colf, float(LANES)), axis=1, keepdims=True)

    is_group = (col >= N_EXPERTS) & (col < N_EXPERTS + N_GROUPS)
    lg = jnp.where(is_group, logits, ninf)
    gmax = jnp.max(lg, axis=1, keepdims=True)
    pg_top = 1.0 / jnp.sum(jnp.exp(lg - gmax), axis=1, keepdims=True)
    g_idx = first(lg == gmax) - float(N_EXPERTS)

    in_group = (col < N_EXPERTS) & ((col // EXPERTS_PER_GROUP).astype(F32) == g_idx)
    le = jnp.where(in_group, logits, ninf)
    m1 = jnp.max(le, axis=1, keepdims=True)
    i1 = first(le == m1)
    le2 = jnp.where(colf == i1, ninf, le)
    m2 = jnp.max(le2, axis=1, keepdims=True)
    i2 = first(le2 == m2)
    ee = jnp.exp(le - m1)
    pe = ee / jnp.sum(ee, axis=1, keepdims=True)
    p1 = jnp.sum(jnp.where(colf == i1, pe, 0.0), axis=1, keepdims=True)
    p2 = jnp.sum(jnp.where(colf == i2, pe, 0.0), axis=1, keepdims=True)
    den = p1 + p2
    return jnp.where(colf == i1, p1 / den * pg_top,
                     jnp.where(colf == i2, p2 / den * pg_top, 0.0))


def _out_xattn_kernel(n_parts, *refs):
    h_ref = refs[0]
    part_refs = refs[1:1 + n_parts]
    w_refs = refs[1 + n_parts:1 + 2 * n_parts]
    (gx_ref, wq_ref, k_ref, v_ref, wo_ref, gf_ref, wr_hi_ref, wr_lo_ref, br_ref,
     h2_ref, xn_ref, gates_ref) = refs[1 + 2 * n_parts:]

    h1 = h_ref[...]
    for p_ref, w_ref in zip(part_refs, w_refs):
        h1 = h1 + _dot(p_ref[...], w_ref[...])

    xn = _rmsnorm(h1, gx_ref[...]).astype(BF16)
    q = (_dot(xn, wq_ref[...]) * (XA_HEAD_DIM ** -0.5)).astype(BF16)
    heads = []
    for hd in range(XA_HEADS):
        sl = slice(hd * XA_HEAD_DIM, (hd + 1) * XA_HEAD_DIM)
        s = lax.dot_general(q[:, sl], k_ref[0, :, sl], (((1,), (1,)), ((), ())),
                            preferred_element_type=F32)
        e = jnp.exp(s - jnp.max(s, axis=1, keepdims=True))
        p = e / jnp.sum(e, axis=1, keepdims=True)
        heads.append(_dot(p.astype(BF16), v_ref[0, :, sl]).astype(BF16))
    h2 = h1 + _dot(jnp.concatenate(heads, axis=1), wo_ref[...])
    h2_ref[...] = h2

    xf = _rmsnorm(h2, gf_ref[...])
    x_hi = xf.astype(BF16)
    x_lo = (xf - x_hi.astype(F32)).astype(BF16)
    xn_ref[...] = x_hi
    logits = (_dot(x_hi, wr_hi_ref[...]) + _dot(x_lo, wr_hi_ref[...])
              + _dot(x_hi, wr_lo_ref[...]) + br_ref[...])
    gates_ref[...] = _router_gates(logits)


def _out_xattn(h, parts, weights, gx, wq, k_all, v_all, wo, gf, wr_hi, wr_lo, br, layer):
    tm = ROW_TILE
    row = lambda n: pl.BlockSpec((tm, n), lambda i: (i, 0))
    kv_spec = pl.BlockSpec((1, MEM_LEN, D_MODEL), lambda i: (layer, 0, 0))
    in_specs = ([row(D_MODEL)] + [row(p.shape[1]) for p in parts]
                + [_full(w.shape) for w in weights]
                + [_full((1, D_MODEL)), _full((D_MODEL, D_MODEL)), kv_spec, kv_spec,
                   _full((D_MODEL, D_MODEL)), _full((1, D_MODEL)),
                   _full((D_MODEL, LANES)), _full((D_MODEL, LANES)), _full((1, LANES))])
    return pl.pallas_call(
        functools.partial(_out_xattn_kernel, len(parts)),
        grid=(SEQ // tm,),
        in_specs=in_specs,
        out_specs=[row(D_MODEL), row(D_MODEL), row(LANES)],
        out_shape=[jax.ShapeDtypeStruct((SEQ, D_MODEL), F32),
                   jax.ShapeDtypeStruct((SEQ, D_MODEL), BF16),
                   jax.ShapeDtypeStruct((SEQ, LANES), F32)],
        compiler_params=_params("parallel"),
        name="out_xattn_router",
    )(h, *parts, *weights, gx, wq, k_all, v_all, wo, gf, wr_hi, wr_lo, br)


def _moe_kernel(final_norm, xn_ref, gates_ref, h_ref, wg_ref, wu_ref, wd_ref, gfin_ref,
                o_ref, acc_ref):
    grp = pl.program_id(1)

    @pl.when(grp == 0)
    def _():
        acc_ref[...] = jnp.zeros_like(acc_ref)

    xn = xn_ref[...]
    gates = gates_ref[...]
    col = lax.broadcasted_iota(jnp.int32, gates.shape, 1)
    for e in range(EXPERTS_PER_GROUP):
        gate = jnp.sum(jnp.where(col == grp * EXPERTS_PER_GROUP + e, gates, 0.0),
                       axis=1, keepdims=True)
        hg = _dot(xn, wg_ref[e])
        hu = _dot(xn, wu_ref[e])
        act = (hg * jax.nn.sigmoid(hg)) * hu * gate
        acc_ref[...] += _dot(act.astype(BF16), wd_ref[e])

    @pl.when(grp == N_GROUPS - 1)
    def _():
        out = h_ref[...] + acc_ref[...]
        if final_norm:
            out = _rmsnorm(out, gfin_ref[...])
        o_ref[...] = out


def _moe(xn, gates, h, wg, wu, wd, gfin, final_norm):
    tm = ROW_TILE
    row = lambda n: pl.BlockSpec((tm, n), lambda i, g: (i, 0))
    up_spec = pl.BlockSpec((EXPERTS_PER_GROUP, D_MODEL, D_EXPERT), lambda i, g: (g, 0, 0))
    down_spec = pl.BlockSpec((EXPERTS_PER_GROUP, D_EXPERT, D_MODEL), lambda i, g: (g, 0, 0))
    return pl.pallas_call(
        functools.partial(_moe_kernel, final_norm),
        grid=(SEQ // tm, N_GROUPS),
        in_specs=[row(D_MODEL), row(LANES), row(D_MODEL), up_spec, up_spec, down_spec,
                  pl.BlockSpec((1, D_MODEL), lambda i, g: (0, 0))],
        out_specs=row(D_MODEL),
        out_shape=jax.ShapeDtypeStruct((SEQ, D_MODEL), F32),
        scratch_shapes=[pltpu.VMEM((tm, D_MODEL), F32)],
        compiler_params=_params("parallel", "arbitrary"),
        name="moe_experts",
    )(xn, gates, h, wg, wu, wd, gfin)


def _l1_mixer_kernel(x_ref, g_ref, w_in_ref, vg_ref, vb_ref, ws_ref, bs_ref, o_ref):
    tm = x_ref.shape[0]
    xn = _rmsnorm(x_ref[...], g_ref[...]).astype(BF16)
    p = _dot(xn, w_in_ref[...])
    p = 0.5 * p * (1.0 + lax.erf(p * (2.0 ** -0.5)))
    u = p[:, :D_MODEL]
    v = _layernorm(p[:, D_MODEL:], vg_ref[...], vb_ref[...]).astype(BF16)
    t_idx = lax.broadcasted_iota(jnp.int32, (GM_CHUNK, GM_CHUNK), 0)
    s_idx = lax.broadcasted_iota(jnp.int32, (GM_CHUNK, GM_CHUNK), 1)
    for g in range(GM_GROUPS):
        cols = slice(g * LANES, (g + 1) * LANES)
        wmix = jnp.where(t_idx >= s_idx, ws_ref[g], 0.0).astype(BF16)
        for c in range(tm // GM_CHUNK):
            rows = slice(c * GM_CHUNK, (c + 1) * GM_CHUNK)
            mixed = _dot(wmix, v[rows, cols]) + bs_ref[g]
            o_ref[rows, cols] = (u[rows, cols] * mixed).astype(BF16)


def _l1_mixer(x, g, w_in, vg, vb, ws, bs_b):
    tm = ROW_TILE
    return pl.pallas_call(
        _l1_mixer_kernel,
        grid=(SEQ // tm,),
        in_specs=[pl.BlockSpec((tm, D_MODEL), lambda i: (i, 0)), _full((1, D_MODEL)),
                  _full((D_MODEL, 2 * D_MODEL)), _full((1, D_MODEL)), _full((1, D_MODEL)),
                  _full((GM_GROUPS, GM_CHUNK, GM_CHUNK)), _full((GM_GROUPS, GM_CHUNK, LANES))],
        out_specs=pl.BlockSpec((tm, D_MODEL), lambda i: (i, 0)),
        out_shape=jax.ShapeDtypeStruct((SEQ, D_MODEL), BF16),
        compiler_params=_params("parallel"),
        name="l1_mixer",
    )(x, g, w_in, vg, vb, ws, bs_b)


def _router_weights(we, wg, be, bg):
    pad = LANES - N_EXPERTS - N_GROUPS
    w = jnp.concatenate([we, wg, jnp.zeros((D_MODEL, pad), F32)], axis=1)
    b = jnp.concatenate([be, bg, jnp.zeros((pad,), F32)])[None, :]
    w_hi = w.astype(BF16)
    w_lo = (w - w_hi.astype(F32)).astype(BF16)
    return w_hi, w_lo, b


def kernel(x, mem, mem_norm_g, norm_mix, norm_xa, norm_ffn, final_norm_g, ab_w_in, ab_conv_w, ab_conv_b, ab_cnorm_g, ab_cnorm_b, ab_w_out, c_w_in, c_vnorm_g, c_vnorm_b, c_ws, c_bs, c_w_out, xa_wq, xa_wk, xa_wv, xa_wo, rt_wg, rt_bg, rt_we, rt_be, ex_w_gate, ex_w_up, ex_w_down):
    assert x.shape == (1, SEQ, D_MODEL) and mem.shape == (1, MEM_LEN, D_MODEL)
    bf = lambda a: a.astype(BF16)
    r1 = lambda a: a.reshape(1, -1)
    h = x[0]

    k_all, v_all = _mem_kv(mem[0], r1(mem_norm_g), bf(xa_wk), bf(xa_wv))

    def tail(h, parts, weights, i, final_norm):
        wr_hi, wr_lo, br = _router_weights(rt_we[i], rt_wg[i], rt_be[i], rt_bg[i])
        h2, xn, gates = _out_xattn(h, parts, weights, r1(norm_xa[i]), bf(xa_wq[i]), k_all, v_all,
                                   bf(xa_wo[i]), r1(norm_ffn[i]), wr_hi, wr_lo, br, i)
        return _moe(xn, gates, h2, bf(ex_w_gate[i]), bf(ex_w_up[i]), bf(ex_w_down[i]),
                    r1(final_norm_g), final_norm)

    w_in = ab_w_in[0]
    q_lo, k_lo, v_lo = 2 * CONV_CH, 2 * CONV_CH + SB_DIM, 2 * CONV_CH + 2 * SB_DIM
    hglu, q, kt, v = _l0_inproj(h, r1(norm_mix[0]), bf(w_in[:, :q_lo]), bf(w_in[:, q_lo:k_lo]),
                                bf(w_in[:, k_lo:v_lo].T), bf(w_in[:, v_lo:]))
    conv_out = _conv(hglu, ab_conv_w[0], r1(ab_conv_b[0]), r1(ab_cnorm_g[0]), r1(ab_cnorm_b[0]))
    sb_out = _sb_attention(q, kt, v)
    w_out = bf(ab_w_out[0])
    h = tail(h, [conv_out, sb_out], [w_out[:CONV_CH], w_out[CONV_CH:]], 0, False)

    bs_b = jnp.broadcast_to(c_bs[0][:, :, None], (GM_GROUPS, GM_CHUNK, LANES))
    gated = _l1_mixer(h, r1(norm_mix[1]), bf(c_w_in[0]), r1(c_vnorm_g[0]), r1(c_vnorm_b[0]),
                      c_ws[0], bs_b)
    h = tail(h, [gated], [bf(c_w_out[0])], 1, True)
    return h[None]
```

```python
import functools

import jax
import jax.numpy as jnp
from jax import lax
from jax.experimental import pallas as pl
from jax.experimental.pallas import tpu as pltpu

D_MODEL = 1024
SEQ = 16384
MEM_LEN = 256
EPS = 1e-6
CONV_CH = 512
CONV_WIDTH = 31
SB_HEADS = 8
SB_HEAD_DIM = 64
SB_DIM = 512
GM_GROUPS = 8
GM_CHUNK = 128
XA_HEADS = 4
XA_HEAD_DIM = 256
N_GROUPS = 4
EXPERTS_PER_GROUP = 4
N_EXPERTS = 16
D_EXPERT = 256

LANES = 128
SUBLANES = 8
ROW_TILE = 512
SB_TQ = 256
SB_TK = 256
SB_LOG_UNDERFLOW = -104.0
CONV_HALO = 32
CONV_ROWS = 64
MOE_UNIT = 16
N_ROW_TILES = SEQ // ROW_TILE
LS_TILE = ROW_TILE + N_GROUPS * MOE_UNIT
LS_UNITS = LS_TILE // MOE_UNIT
SORT_UNITS = ROW_TILE // MOE_UNIT
N_SORT_TILES = N_ROW_TILES * LS_UNITS // SORT_UNITS + N_GROUPS
VMEM_LIMIT = 56 * 1024 * 1024

BF16 = jnp.bfloat16
F32 = jnp.float32


def _params(*semantics):
    return pltpu.CompilerParams(dimension_semantics=semantics, vmem_limit_bytes=VMEM_LIMIT)


def _dot(a, b):
    return jnp.dot(a, b, preferred_element_type=F32)


def _rmsnorm(x, g):
    return x * lax.rsqrt(jnp.mean(x * x, axis=-1, keepdims=True) + EPS) * g


def _layernorm(x, g, b):
    mu = jnp.mean(x, axis=-1, keepdims=True)
    xc = x - mu
    var = jnp.mean(xc * xc, axis=-1, keepdims=True)
    return xc * lax.rsqrt(var + EPS) * g + b


def _full(shape):
    return pl.BlockSpec(shape, lambda *_: (0,) * len(shape))


def _mem_kv_kernel(mem_ref, g_ref, wk_ref, wv_ref, k_ref, v_ref):
    memn = _rmsnorm(mem_ref[...], g_ref[...]).astype(BF16)
    k_ref[0] = _dot(memn, wk_ref[0]).astype(BF16)
    v_ref[0] = _dot(memn, wv_ref[0]).astype(BF16)


def _mem_kv(mem, g, wk, wv):
    depth = wk.shape[0]
    wspec = pl.BlockSpec((1, D_MODEL, D_MODEL), lambda i: (i, 0, 0))
    ospec = pl.BlockSpec((1, MEM_LEN, D_MODEL), lambda i: (i, 0, 0))
    return pl.pallas_call(
        _mem_kv_kernel,
        grid=(depth,),
        in_specs=[_full((MEM_LEN, D_MODEL)), _full((1, D_MODEL)), wspec, wspec],
        out_specs=[ospec, ospec],
        out_shape=[jax.ShapeDtypeStruct((depth, MEM_LEN, D_MODEL), BF16)] * 2,
        compiler_params=_params("arbitrary"),
        name="mem_kv",
    )(mem, g, wk, wv)


def _l0_inproj_kernel(x_ref, g_ref, w_ag_ref, w_q_ref, w_kt_ref, w_v_ref,
                      h_ref, q_ref, kt_ref, v_ref):
    xn = _rmsnorm(x_ref[...], g_ref[...]).astype(BF16)
    ag = _dot(xn, w_ag_ref[...])
    h_ref[...] = ag[:, :CONV_CH] * jax.nn.sigmoid(ag[:, CONV_CH:])
    q_ref[...] = (_dot(xn, w_q_ref[...]) * (SB_HEAD_DIM ** -0.5)).astype(BF16)
    kt_ref[...] = lax.dot_general(w_kt_ref[...], xn, (((1,), (1,)), ((), ())),
                                  preferred_element_type=F32).astype(BF16)
    v_ref[...] = _dot(xn, w_v_ref[...]).astype(BF16)


def _l0_inproj(x, g, w_ag, w_q, w_kt, w_v):
    tm = ROW_TILE
    row = lambda n: pl.BlockSpec((tm, n), lambda i: (i, 0))
    return pl.pallas_call(
        _l0_inproj_kernel,
        grid=(SEQ // tm,),
        in_specs=[row(D_MODEL), _full((1, D_MODEL)), _full((D_MODEL, 2 * CONV_CH)),
                  _full((D_MODEL, SB_DIM)), _full((SB_DIM, D_MODEL)), _full((D_MODEL, SB_DIM))],
        out_specs=[row(CONV_CH), row(SB_DIM), pl.BlockSpec((SB_DIM, tm), lambda i: (0, i)),
                   row(SB_DIM)],
        out_shape=[jax.ShapeDtypeStruct((SEQ, CONV_CH), F32),
                   jax.ShapeDtypeStruct((SEQ, SB_DIM), BF16),
                   jax.ShapeDtypeStruct((SB_DIM, SEQ), BF16),
                   jax.ShapeDtypeStruct((SEQ, SB_DIM), BF16)],
        compiler_params=_params("parallel"),
        name="l0_inproj",
    )(x, g, w_ag, w_q, w_kt, w_v)


def _conv_kernel(prev_ref, cur_ref, cw_ref, cb_ref, lg_ref, lb_ref, o_ref, ext_ref, y_ref):
    tm = cur_ref.shape[0]
    ext_ref[0:CONV_HALO, :] = jnp.where(pl.program_id(0) > 0, prev_ref[...], 0.0)
    ext_ref[CONV_HALO:, :] = cur_ref[...]
    first_tap = CONV_HALO - (CONV_WIDTH - 1)

    def chunk(c, _):
        r0 = pl.multiple_of(c * CONV_ROWS, CONV_ROWS)
        for j in range(CONV_CH // LANES):
            lanes = slice(j * LANES, (j + 1) * LANES)
            win = ext_ref[pl.ds(r0, CONV_ROWS + CONV_HALO), lanes]
            acc = jnp.zeros((CONV_ROWS, LANES), F32) + cb_ref[:, lanes]
            for sub in range(SUBLANES):
                offsets = [o for o in range(first_tap, first_tap + CONV_WIDTH)
                           if o % SUBLANES == sub]
                shifted = pltpu.roll(win, win.shape[0] - sub, axis=0) if sub else win
                for o in offsets:
                    tap = cw_ref[o - first_tap:o - first_tap + 1, lanes]
                    acc = acc + shifted[o - sub:o - sub + CONV_ROWS, :] * tap
            y_ref[pl.ds(r0, CONV_ROWS), lanes] = acc
        return 0

    lax.fori_loop(0, tm // CONV_ROWS, chunk, 0)
    y = _layernorm(y_ref[...], lg_ref[...], lb_ref[...])
    o_ref[...] = (y * jax.nn.sigmoid(y)).astype(BF16)


def _conv(h, cw, cb, lg, lb):
    tm = ROW_TILE
    halo_blocks = tm // CONV_HALO
    return pl.pallas_call(
        _conv_kernel,
        grid=(SEQ // tm,),
        in_specs=[pl.BlockSpec((CONV_HALO, CONV_CH),
                               lambda i: (jnp.maximum(i * halo_blocks - 1, 0), 0)),
                  pl.BlockSpec((tm, CONV_CH), lambda i: (i, 0)),
                  _full((CONV_WIDTH, CONV_CH)), _full((1, CONV_CH)),
                  _full((1, CONV_CH)), _full((1, CONV_CH))],
        out_specs=pl.BlockSpec((tm, CONV_CH), lambda i: (i, 0)),
        out_shape=jax.ShapeDtypeStruct((SEQ, CONV_CH), BF16),
        scratch_shapes=[pltpu.VMEM((tm + CONV_HALO, CONV_CH), F32),
                        pltpu.VMEM((tm, CONV_CH), F32)],
        compiler_params=_params("parallel"),
        name="conv_ln_silu",
    )(h, h, cw, cb, lg, lb)


def _sb_kernel(q_ref, kt_ref, v_ref, o_ref, acc_ref, carry_ref):
    qi = pl.program_id(1)
    tq, tk = SB_TQ, SB_TK
    lane = lax.broadcasted_iota(jnp.int32, (tq, LANES), 1)
    q2 = q_ref[...]
    zero = jnp.zeros_like(q2)
    qh = (jnp.where(lane < SB_HEAD_DIM, q2, zero), jnp.where(lane >= SB_HEAD_DIM, q2, zero))
    jj = lax.broadcasted_iota(jnp.int32, (tk, tk), 0)
    ss = lax.broadcasted_iota(jnp.int32, (tk, tk), 1)
    minus_later = jnp.where(jj > ss, -1.0, 0.0).astype(BF16)

    def block(h, kb, valid):
        k0 = pl.multiple_of(kb * tk, tk)
        z = _dot(qh[h], kt_ref[:, pl.ds(k0, tk)])
        sp = jnp.maximum(z, 0.0) + jnp.log(1.0 + jnp.exp(-jnp.abs(z)))
        if valid is not None:
            sp = jnp.where(valid, sp, 0.0)
        hi = sp.astype(BF16)
        lo = (sp - hi.astype(F32)).astype(BF16)
        after = _dot(hi, minus_later) + _dot(lo, minus_later)
        w = jnp.exp((z - sp) + after)
        if valid is not None:
            w = jnp.where(valid, w, 0.0)
        return _dot(w.astype(BF16), v_ref[pl.ds(k0, tk), :]), jnp.sum(sp, axis=1, keepdims=True)

    row = lax.broadcasted_iota(jnp.int32, (tq, tk), 0)
    col = lax.broadcasted_iota(jnp.int32, (tq, tk), 1)
    has_prev = qi > 0
    for h in range(2):
        pv_diag, sp_diag = block(h, qi, col < row)
        pv_prev, sp_prev = block(h, jnp.maximum(qi - 1, 0), None)
        acc_ref[h] = pv_diag + jnp.where(has_prev, jnp.exp(-sp_diag), 0.0) * pv_prev
        carry_ref[h] = jnp.broadcast_to(-(sp_diag + sp_prev), (tq, LANES))

    def alive():
        return jnp.max(carry_ref[...]) > SB_LOG_UNDERFLOW

    def cond(state):
        kb, go = state
        return (kb >= 0) & go

    def body(state):
        kb, _ = state
        for h in range(2):
            pv, sp_sum = block(h, kb, None)
            carry = carry_ref[h]
            acc_ref[h] += jnp.exp(carry) * pv
            carry_ref[h] = carry - sp_sum
        return kb - 1, alive()

    lax.while_loop(cond, body, (qi - 2, alive()))
    o_ref[...] = jnp.where(lane < SB_HEAD_DIM, acc_ref[0], acc_ref[1]).astype(BF16)


def _sb_attention(q, kt, v):
    tq = SB_TQ
    return pl.pallas_call(
        _sb_kernel,
        grid=(SB_DIM // LANES, SEQ // tq),
        in_specs=[pl.BlockSpec((tq, LANES), lambda p, i: (i, p)),
                  pl.BlockSpec((LANES, SEQ), lambda p, i: (p, 0)),
                  pl.BlockSpec((SEQ, LANES), lambda p, i: (0, p))],
        out_specs=pl.BlockSpec((tq, LANES), lambda p, i: (i, p)),
        out_shape=jax.ShapeDtypeStruct((SEQ, SB_DIM), BF16),
        scratch_shapes=[pltpu.VMEM((2, tq, LANES), F32), pltpu.VMEM((2, tq, LANES), F32)],
        compiler_params=_params("parallel", "parallel"),
        name="sb_attention",
    )(q, kt, v)


def _router_gates(logits):
    col = lax.broadcasted_iota(jnp.int32, logits.shape, 1)
    colf = col.astype(F32)
    ninf = -jnp.inf
    first = lambda hit: jnp.min(jnp.where(hit, colf, float(LANES)), axis=1, keepdims=True)

    is_group = (col >= N_EXPERTS) & (col < N_EXPERTS + N_GROUPS)
    lg = jnp.where(is_group, logits, ninf)
    gmax = jnp.max(lg, axis=1, keepdims=True)
    pg_top = 1.0 / jnp.sum(jnp.exp(lg - gmax), axis=1, keepdims=True)
    g_idx = first(lg == gmax) - float(N_EXPERTS)

    in_group = (col < N_EXPERTS) & ((col // EXPERTS_PER_GROUP).astype(F32) == g_idx)
    le = jnp.where(in_group, logits, ninf)
    m1 = jnp.max(le, axis=1, keepdims=True)
    i1 = first(le == m1)
    le2 = jnp.where(colf == i1, ninf, le)
    m2 = jnp.max(le2, axis=1, keepdims=True)
    i2 = first(le2 == m2)
    ee = jnp.exp(le - m1)
    pe = ee / jnp.sum(ee, axis=1, keepdims=True)
    p1 = jnp.sum(jnp.where(colf == i1, pe, 0.0), axis=1, keepdims=True)
    p2 = jnp.sum(jnp.where(colf == i2, pe, 0.0), axis=1, keepdims=True)
    den = p1 + p2
    gates = jnp.where(colf == i1, p1 / den * pg_top,
                      jnp.where(colf == i2, p2 / den * pg_top, 0.0))
    return gates, g_idx


def _local_sort(g_idx):
    tm = g_idx.shape[0]
    lane = lax.broadcasted_iota(jnp.int32, (tm, LANES), 1)
    member = jnp.where(lane.astype(F32) == g_idx, 1.0, 0.0)
    r_i = lax.broadcasted_iota(jnp.int32, (tm, tm), 0)
    c_i = lax.broadcasted_iota(jnp.int32, (tm, tm), 1)
    before = jnp.where(c_i < r_i, 1.0, 0.0).astype(BF16)
    rank = jnp.sum(member * _dot(before, member.astype(BF16)), axis=1, keepdims=True)
    count = jnp.sum(member, axis=0, keepdims=True)
    padded = jnp.floor((count + (MOE_UNIT - 1)) * (1.0 / MOE_UNIT)) * MOE_UNIT
    lane1 = lax.broadcasted_iota(jnp.int32, (1, LANES), 1)
    offset = jnp.zeros((1, LANES), F32)
    start = jnp.zeros((1, 1), F32)
    for g in range(N_GROUPS):
        offset = offset + jnp.where((lane1 == g) | (lane1 == N_GROUPS + g), start, 0.0)
        start = start + jnp.sum(jnp.where(lane1 == g, padded, 0.0), axis=1, keepdims=True)
    dest = jnp.sum(member * offset, axis=1, keepdims=True) + rank
    rows = jnp.where(lane1 == N_GROUPS - 1, LS_TILE - offset, padded)
    units = jnp.where(lane1 < N_GROUPS, rows, offset) * (1.0 / MOE_UNIT)
    return dest, units.astype(jnp.int32)


def _split3(x):
    hi = x.astype(BF16)
    r1 = x - hi.astype(F32)
    mid = r1.astype(BF16)
    return hi, mid, (r1 - mid.astype(F32)).astype(BF16)


def _out_xattn_kernel(n_parts, *refs):
    h_ref = refs[0]
    part_refs = refs[1:1 + n_parts]
    w_refs = refs[1 + n_parts:1 + 2 * n_parts]
    (gx_ref, wq_ref, k_ref, v_ref, wo_ref, gf_ref, wr_ref, br_ref,
     h2_ref, xn_ls_ref, gates_ls_ref, dest_ref, units_ref) = refs[1 + 2 * n_parts:]

    h1 = h_ref[...]
    for p_ref, w_ref in zip(part_refs, w_refs):
        h1 = h1 + _dot(p_ref[...], w_ref[...])

    xn = _rmsnorm(h1, gx_ref[...]).astype(BF16)
    q = (_dot(xn, wq_ref[...]) * (XA_HEAD_DIM ** -0.5)).astype(BF16)
    heads = []
    for hd in range(XA_HEADS):
        sl = slice(hd * XA_HEAD_DIM, (hd + 1) * XA_HEAD_DIM)
        s = lax.dot_general(q[:, sl], k_ref[0, :, sl], (((1,), (1,)), ((), ())),
                            preferred_element_type=F32)
        e = jnp.exp(s - jnp.max(s, axis=1, keepdims=True))
        p = e / jnp.sum(e, axis=1, keepdims=True)
        heads.append(_dot(p.astype(BF16), v_ref[0, :, sl]).astype(BF16))
    h2 = h1 + _dot(jnp.concatenate(heads, axis=1), wo_ref[...])
    h2_ref[...] = h2

    xf = _rmsnorm(h2, gf_ref[...])
    x_hi = xf.astype(BF16)
    x_lo = (xf - x_hi.astype(F32)).astype(BF16)
    both = _dot(x_hi, wr_ref[...])
    logits = both[:, :LANES] + both[:, LANES:] + _dot(x_lo, wr_ref[:, :LANES]) + br_ref[...]
    gates, g_idx = _router_gates(logits)

    dest, units = _local_sort(g_idx)
    tm = dest.shape[0]
    dest_ref[...] = jnp.broadcast_to(dest, (tm, LANES))
    units_ref[0] = jnp.broadcast_to(units, (SUBLANES, LANES))
    dest_row = jnp.transpose(jnp.broadcast_to(dest, (tm, LANES)))[0:1, :]
    slot = lax.broadcasted_iota(jnp.int32, (LS_TILE, tm), 0).astype(F32)
    place = jnp.where(slot == dest_row, 1.0, 0.0).astype(BF16)
    xn_ls_ref[...] = _dot(place, x_hi).astype(BF16)
    gates_ls_ref[...] = sum(_dot(place, term) for term in _split3(gates))


def _out_xattn(h, parts, weights, gx, wq, k_all, v_all, wo, gf, wr, br, layer):
    tm = ROW_TILE
    row = lambda n: pl.BlockSpec((tm, n), lambda i: (i, 0))
    ls_row = lambda n: pl.BlockSpec((LS_TILE, n), lambda i: (i, 0))
    kv_spec = pl.BlockSpec((1, MEM_LEN, D_MODEL), lambda i: (layer, 0, 0))
    in_specs = ([row(D_MODEL)] + [row(p.shape[1]) for p in parts]
                + [_full(w.shape) for w in weights]
                + [_full((1, D_MODEL)), _full((D_MODEL, D_MODEL)), kv_spec, kv_spec,
                   _full((D_MODEL, D_MODEL)), _full((1, D_MODEL)),
                   _full((D_MODEL, 2 * LANES)), _full((1, LANES))])
    return pl.pallas_call(
        functools.partial(_out_xattn_kernel, len(parts)),
        grid=(N_ROW_TILES,),
        in_specs=in_specs,
        out_specs=[row(D_MODEL), ls_row(D_MODEL), ls_row(LANES), row(LANES),
                   pl.BlockSpec((1, SUBLANES, LANES), lambda i: (i, 0, 0))],
        out_shape=[jax.ShapeDtypeStruct((SEQ, D_MODEL), F32),
                   jax.ShapeDtypeStruct((N_ROW_TILES * LS_TILE, D_MODEL), BF16),
                   jax.ShapeDtypeStruct((N_ROW_TILES * LS_TILE, LANES), F32),
                   jax.ShapeDtypeStruct((SEQ, LANES), F32),
                   jax.ShapeDtypeStruct((N_ROW_TILES, SUBLANES, LANES), jnp.int32)],
        compiler_params=_params("parallel"),
        name="out_xattn_router",
    )(h, *parts, *weights, gx, wq, k_all, v_all, wo, gf, wr, br)


def _route_index_kernel(units_ref, unit_of_ref, tile_group_ref):
    def clear_unit(k, _):
        unit_of_ref[k] = -1
        return 0

    def clear_tile(j, _):
        tile_group_ref[j] = -1
        return 0

    lax.fori_loop(0, N_SORT_TILES * SORT_UNITS, clear_unit, 0)
    lax.fori_loop(0, N_SORT_TILES, clear_tile, 0)

    pos = 0
    for g in range(N_GROUPS):
        first_tile = pos // SORT_UNITS

        def per_row_tile(i, pos):
            n = units_ref[i, g]
            base = i * LS_UNITS + units_ref[i, N_GROUPS + g]

            def per_unit(k, _):
                unit_of_ref[pos + k] = base + k
                return 0

            lax.fori_loop(0, n, per_unit, 0)
            return pos + n

        pos = lax.fori_loop(0, N_ROW_TILES, per_row_tile, pos)
        end_tile = (pos + SORT_UNITS - 1) // SORT_UNITS

        def mark_tile(j, _):
            tile_group_ref[j] = g
            return 0

        lax.fori_loop(first_tile, end_tile, mark_tile, 0)
        pos = end_tile * SORT_UNITS


def _route_index(units):
    smem = lambda: pl.BlockSpec(memory_space=pltpu.SMEM)
    return pl.pallas_call(
        _route_index_kernel,
        in_specs=[smem()],
        out_specs=[smem(), smem()],
        out_shape=[jax.ShapeDtypeStruct((N_SORT_TILES * SORT_UNITS,), jnp.int32),
                   jax.ShapeDtypeStruct((N_SORT_TILES,), jnp.int32)],
        name="moe_route_index",
    )(units)


def _moe_kernel(unit_of_ref, tile_group_ref, xn_hbm, gates_hbm, wg_ref, wu_ref, wd_ref,
                y_hbm, xbuf, gbuf, ybuf, gather_sem, scatter_sem):
    j = pl.program_id(0)
    n_tiles = pl.num_programs(0)
    slot = j % 2

    def for_units(tile, fn):
        def unit(k, _):
            u = unit_of_ref[tile * SORT_UNITS + k]
            fn(u, pl.ds(pl.multiple_of(k * MOE_UNIT, MOE_UNIT), MOE_UNIT),
               pl.ds(pl.multiple_of(jnp.maximum(u, 0) * MOE_UNIT, MOE_UNIT), MOE_UNIT))
            return 0

        lax.fori_loop(0, SORT_UNITS, unit, 0)

    def gather_copies(s, rows, ls_rows):
        return (pltpu.make_async_copy(xn_hbm.at[ls_rows, :], xbuf.at[s, rows, :], gather_sem.at[s]),
                pltpu.make_async_copy(gates_hbm.at[ls_rows, :], gbuf.at[s, rows, :],
                                      gather_sem.at[s]))

    def scatter_copy(s, rows, ls_rows):
        return pltpu.make_async_copy(ybuf.at[s, rows, :], y_hbm.at[ls_rows, :], scatter_sem.at[s])

    def start_gather(tile, s):
        def fn(u, rows, ls_rows):
            @pl.when(u >= 0)
            def _():
                for c in gather_copies(s, rows, ls_rows):
                    c.start()

            @pl.when(u < 0)
            def _():
                xbuf[s, rows, :] = jnp.zeros((MOE_UNIT, D_MODEL), BF16)
                gbuf[s, rows, :] = jnp.zeros((MOE_UNIT, LANES), F32)

        for_units(tile, fn)

    def wait_gather(tile, s):
        def fn(u, rows, ls_rows):
            @pl.when(u >= 0)
            def _():
                for c in gather_copies(s, rows, ls_rows):
                    c.wait()

        for_units(tile, fn)

    def start_scatter(tile, s):
        def fn(u, rows, ls_rows):
            @pl.when(u >= 0)
            def _():
                scatter_copy(s, rows, ls_rows).start()

        for_units(tile, fn)

    def wait_scatter(tile, s):
        def fn(u, rows, ls_rows):
            @pl.when(u >= 0)
            def _():
                scatter_copy(s, rows, ls_rows).wait()

        for_units(tile, fn)

    @pl.when(j == 0)
    def _():
        start_gather(0, 0)

    @pl.when(j + 1 < n_tiles)
    def _():
        start_gather(j + 1, 1 - slot)

    wait_gather(j, slot)

    @pl.when(j >= 2)
    def _():
        wait_scatter(j - 2, slot)

    grp = tile_group_ref[j]

    @pl.when(grp >= 0)
    def _():
        xn = xbuf[slot]
        gates = gbuf[slot]
        col = lax.broadcasted_iota(jnp.int32, gates.shape, 1)
        for e in range(EXPERTS_PER_GROUP):
            gate = jnp.sum(jnp.where(col == grp * EXPERTS_PER_GROUP + e, gates, 0.0),
                           axis=1, keepdims=True)
            hg = _dot(xn, wg_ref[e])
            hu = _dot(xn, wu_ref[e])
            act = (hg * jax.nn.sigmoid(hg)) * hu * gate
            out = _dot(act.astype(BF16), wd_ref[e])
            if e == 0:
                ybuf[slot] = out
            else:
                ybuf[slot] += out

    start_scatter(j, slot)

    @pl.when(j == n_tiles - 1)
    def _():
        wait_scatter(j - 1, 1 - slot)
        wait_scatter(j, slot)


def _moe(unit_of, tile_group, xn_ls, gates_ls, wg, wu, wd):
    group_of = lambda j, unit_of, tile_group: jnp.where(tile_group[j] < 0, N_GROUPS - 1,
                                                        tile_group[j])
    up_spec = pl.BlockSpec((EXPERTS_PER_GROUP, D_MODEL, D_EXPERT),
                           lambda j, u, t: (group_of(j, u, t), 0, 0))
    down_spec = pl.BlockSpec((EXPERTS_PER_GROUP, D_EXPERT, D_MODEL),
                             lambda j, u, t: (group_of(j, u, t), 0, 0))
    any_spec = pl.BlockSpec(memory_space=pl.ANY)
    return pl.pallas_call(
        _moe_kernel,
        grid_spec=pltpu.PrefetchScalarGridSpec(
            num_scalar_prefetch=2,
            grid=(N_SORT_TILES,),
            in_specs=[any_spec, any_spec, up_spec, up_spec, down_spec],
            out_specs=any_spec,
            scratch_shapes=[pltpu.VMEM((2, ROW_TILE, D_MODEL), BF16),
                            pltpu.VMEM((2, ROW_TILE, LANES), F32),
                            pltpu.VMEM((2, ROW_TILE, D_MODEL), F32),
                            pltpu.SemaphoreType.DMA((2,)),
                            pltpu.SemaphoreType.DMA((2,))]),
        out_shape=jax.ShapeDtypeStruct((N_ROW_TILES * LS_TILE, D_MODEL), F32),
        compiler_params=_params("arbitrary"),
        name="moe_experts",
    )(unit_of, tile_group, xn_ls, gates_ls, wg, wu, wd)


def _unsort_kernel(final_norm, h_ref, y_ls_ref, dest_ref, gfin_ref, o_ref):
    tm = h_ref.shape[0]
    slot = lax.broadcasted_iota(jnp.int32, (tm, LS_TILE), 1).astype(F32)
    pick = jnp.where(slot == dest_ref[:, 0:1], 1.0, 0.0).astype(BF16)
    y = y_ls_ref[...]
    y_hi = y.astype(BF16)
    y_lo = (y - y_hi.astype(F32)).astype(BF16)
    out = h_ref[...] + (_dot(pick, y_hi) + _dot(pick, y_lo))
    if final_norm:
        out = _rmsnorm(out, gfin_ref[...])
    o_ref[...] = out


def _unsort(h, y_ls, dest, gfin, final_norm):
    tm = ROW_TILE
    row = lambda n: pl.BlockSpec((tm, n), lambda i: (i, 0))
    return pl.pallas_call(
        functools.partial(_unsort_kernel, final_norm),
        grid=(N_ROW_TILES,),
        in_specs=[row(D_MODEL), pl.BlockSpec((LS_TILE, D_MODEL), lambda i: (i, 0)), row(LANES),
                  _full((1, D_MODEL))],
        out_specs=row(D_MODEL),
        out_shape=jax.ShapeDtypeStruct((SEQ, D_MODEL), F32),
        compiler_params=_params("parallel"),
        name="moe_unsort_residual",
    )(h, y_ls, dest, gfin)


def _l1_mixer_kernel(x_ref, g_ref, w_in_ref, vg_ref, vb_ref, ws_ref, bs_ref, o_ref):
    tm = x_ref.shape[0]
    xn = _rmsnorm(x_ref[...], g_ref[...]).astype(BF16)
    p = _dot(xn, w_in_ref[...])
    p = 0.5 * p * (1.0 + lax.erf(p * (2.0 ** -0.5)))
    u = p[:, :D_MODEL]
    v = _layernorm(p[:, D_MODEL:], vg_ref[...], vb_ref[...]).astype(BF16)
    t_idx = lax.broadcasted_iota(jnp.int32, (GM_CHUNK, GM_CHUNK), 0)
    s_idx = lax.broadcasted_iota(jnp.int32, (GM_CHUNK, GM_CHUNK), 1)
    for g in range(GM_GROUPS):
        cols = slice(g * LANES, (g + 1) * LANES)
        wmix = jnp.where(t_idx >= s_idx, ws_ref[g], 0.0).astype(BF16)
        for c in range(tm // GM_CHUNK):
            rows = slice(c * GM_CHUNK, (c + 1) * GM_CHUNK)
            mixed = _dot(wmix, v[rows, cols]) + bs_ref[g]
            o_ref[rows, cols] = (u[rows, cols] * mixed).astype(BF16)


def _l1_mixer(x, g, w_in, vg, vb, ws, bs_b):
    tm = ROW_TILE
    return pl.pallas_call(
        _l1_mixer_kernel,
        grid=(SEQ // tm,),
        in_specs=[pl.BlockSpec((tm, D_MODEL), lambda i: (i, 0)), _full((1, D_MODEL)),
                  _full((D_MODEL, 2 * D_MODEL)), _full((1, D_MODEL)), _full((1, D_MODEL)),
                  _full((GM_GROUPS, GM_CHUNK, GM_CHUNK)), _full((GM_GROUPS, GM_CHUNK, LANES))],
        out_specs=pl.BlockSpec((tm, D_MODEL), lambda i: (i, 0)),
        out_shape=jax.ShapeDtypeStruct((SEQ, D_MODEL), BF16),
        compiler_params=_params("parallel"),
        name="l1_mixer",
    )(x, g, w_in, vg, vb, ws, bs_b)


def _router_weights(we, wg, be, bg):
    pad = LANES - N_EXPERTS - N_GROUPS
    w = jnp.concatenate([we, wg, jnp.zeros((D_MODEL, pad), F32)], axis=1)
    b = jnp.concatenate([be, bg, jnp.zeros((pad,), F32)])[None, :]
    w_hi = w.astype(BF16)
    w_lo = (w - w_hi.astype(F32)).astype(BF16)
    return jnp.concatenate([w_hi, w_lo], axis=1), b


def kernel(x, mem, mem_norm_g, norm_mix, norm_xa, norm_ffn, final_norm_g, ab_w_in, ab_conv_w, ab_conv_b, ab_cnorm_g, ab_cnorm_b, ab_w_out, c_w_in, c_vnorm_g, c_vnorm_b, c_ws, c_bs, c_w_out, xa_wq, xa_wk, xa_wv, xa_wo, rt_wg, rt_bg, rt_we, rt_be, ex_w_gate, ex_w_up, ex_w_down):
    assert x.shape == (1, SEQ, D_MODEL) and mem.shape == (1, MEM_LEN, D_MODEL)
    bf = lambda a: a.astype(BF16)
    r1 = lambda a: a.reshape(1, -1)
    h = x[0]

    k_all, v_all = _mem_kv(mem[0], r1(mem_norm_g), bf(xa_wk), bf(xa_wv))

    def tail(h, parts, weights, i, final_norm):
        wr, br = _router_weights(rt_we[i], rt_wg[i], rt_be[i], rt_bg[i])
        h2, xn_ls, gates_ls, dest, units = _out_xattn(
            h, parts, weights, r1(norm_xa[i]), bf(xa_wq[i]), k_all, v_all, bf(xa_wo[i]),
            r1(norm_ffn[i]), wr, br, i)
        unit_of, tile_group = _route_index(units[:, 0, :])
        y_ls = _moe(unit_of, tile_group, xn_ls, gates_ls,
                    bf(ex_w_gate[i]), bf(ex_w_up[i]), bf(ex_w_down[i]))
        return _unsort(h2, y_ls, dest, r1(final_norm_g), final_norm)

    w_in = ab_w_in[0]
    q_lo, k_lo, v_lo = 2 * CONV_CH, 2 * CONV_CH + SB_DIM, 2 * CONV_CH + 2 * SB_DIM
    hglu, q, kt, v = _l0_inproj(h, r1(norm_mix[0]), bf(w_in[:, :q_lo]), bf(w_in[:, q_lo:k_lo]),
                                bf(w_in[:, k_lo:v_lo].T), bf(w_in[:, v_lo:]))
    conv_out = _conv(hglu, ab_conv_w[0], r1(ab_conv_b[0]), r1(ab_cnorm_g[0]), r1(ab_cnorm_b[0]))
    sb_out = _sb_attention(q, kt, v)
    w_out = bf(ab_w_out[0])
    h = tail(h, [conv_out, sb_out], [w_out[:CONV_CH], w_out[CONV_CH:]], 0, False)

    bs_b = jnp.broadcast_to(c_bs[0][:, :, None], (GM_GROUPS, GM_CHUNK, LANES))
    gated = _l1_mixer(h, r1(norm_mix[1]), bf(c_w_in[0]), r1(c_vnorm_g[0]), r1(c_vnorm_b[0]),
                      c_ws[0], bs_b)
    h = tail(h, [gated], [bf(c_w_out[0])], 1, True)
    return h[None]
```

```python
import functools

import jax
import jax.numpy as jnp
from jax import lax
from jax.experimental import pallas as pl
from jax.experimental.pallas import tpu as pltpu

D_MODEL = 1024
SEQ = 16384
MEM_LEN = 256
EPS = 1e-6
CONV_CH = 512
CONV_WIDTH = 31
SB_HEADS = 8
SB_HEAD_DIM = 64
SB_DIM = 512
GM_GROUPS = 8
GM_CHUNK = 128
XA_HEADS = 4
XA_HEAD_DIM = 256
N_GROUPS = 4
EXPERTS_PER_GROUP = 4
N_EXPERTS = 16
D_EXPERT = 256

LANES = 128
SUBLANES = 8
ROW_TILE = 512
SB_TQ = 256
SB_TK = 256
SB_LOG_UNDERFLOW = -104.0
CONV_HALO = 32
CONV_ROWS = 64
MOE_UNIT = 16
N_ROW_TILES = SEQ // ROW_TILE
LS_TILE = ROW_TILE + N_GROUPS * MOE_UNIT
LS_UNITS = LS_TILE // MOE_UNIT
SORT_UNITS = ROW_TILE // MOE_UNIT
N_SORT_TILES = N_ROW_TILES * LS_UNITS // SORT_UNITS + N_GROUPS
GATE_TERM_STRIDE = 32
LS_WIDTH = D_MODEL + LANES
VMEM_LIMIT = 56 * 1024 * 1024

BF16 = jnp.bfloat16
F32 = jnp.float32


def _params(*semantics):
    return pltpu.CompilerParams(dimension_semantics=semantics, vmem_limit_bytes=VMEM_LIMIT)


def _dot(a, b):
    return jnp.dot(a, b, preferred_element_type=F32)


def _rmsnorm(x, g):
    return x * lax.rsqrt(jnp.mean(x * x, axis=-1, keepdims=True) + EPS) * g


def _layernorm(x, g, b):
    mu = jnp.mean(x, axis=-1, keepdims=True)
    xc = x - mu
    var = jnp.mean(xc * xc, axis=-1, keepdims=True)
    return xc * lax.rsqrt(var + EPS) * g + b


def _full(shape):
    return pl.BlockSpec(shape, lambda *_: (0,) * len(shape))


def _mem_kv_kernel(mem_ref, g_ref, wk_ref, wv_ref, k_ref, v_ref):
    memn = _rmsnorm(mem_ref[...], g_ref[...]).astype(BF16)
    k_ref[0] = _dot(memn, wk_ref[0]).astype(BF16)
    v_ref[0] = _dot(memn, wv_ref[0]).astype(BF16)


def _mem_kv(mem, g, wk, wv):
    depth = wk.shape[0]
    wspec = pl.BlockSpec((1, D_MODEL, D_MODEL), lambda i: (i, 0, 0))
    ospec = pl.BlockSpec((1, MEM_LEN, D_MODEL), lambda i: (i, 0, 0))
    return pl.pallas_call(
        _mem_kv_kernel,
        grid=(depth,),
        in_specs=[_full((MEM_LEN, D_MODEL)), _full((1, D_MODEL)), wspec, wspec],
        out_specs=[ospec, ospec],
        out_shape=[jax.ShapeDtypeStruct((depth, MEM_LEN, D_MODEL), BF16)] * 2,
        compiler_params=_params("arbitrary"),
        name="mem_kv",
    )(mem, g, wk, wv)


def _l0_inproj_kernel(x_ref, g_ref, w_ag_ref, w_q_ref, w_kt_ref, w_v_ref,
                      h_ref, q_ref, kt_ref, v_ref):
    xn = _rmsnorm(x_ref[...], g_ref[...]).astype(BF16)
    ag = _dot(xn, w_ag_ref[...])
    h_ref[...] = ag[:, :CONV_CH] * jax.nn.sigmoid(ag[:, CONV_CH:])
    q_ref[...] = (_dot(xn, w_q_ref[...]) * (SB_HEAD_DIM ** -0.5)).astype(BF16)
    kt_ref[...] = lax.dot_general(w_kt_ref[...], xn, (((1,), (1,)), ((), ())),
                                  preferred_element_type=F32).astype(BF16)
    v_ref[...] = _dot(xn, w_v_ref[...]).astype(BF16)


def _l0_inproj(x, g, w_ag, w_q, w_kt, w_v):
    tm = ROW_TILE
    row = lambda n: pl.BlockSpec((tm, n), lambda i: (i, 0))
    return pl.pallas_call(
        _l0_inproj_kernel,
        grid=(SEQ // tm,),
        in_specs=[row(D_MODEL), _full((1, D_MODEL)), _full((D_MODEL, 2 * CONV_CH)),
                  _full((D_MODEL, SB_DIM)), _full((SB_DIM, D_MODEL)), _full((D_MODEL, SB_DIM))],
        out_specs=[row(CONV_CH), row(SB_DIM), pl.BlockSpec((SB_DIM, tm), lambda i: (0, i)),
                   row(SB_DIM)],
        out_shape=[jax.ShapeDtypeStruct((SEQ, CONV_CH), F32),
                   jax.ShapeDtypeStruct((SEQ, SB_DIM), BF16),
                   jax.ShapeDtypeStruct((SB_DIM, SEQ), BF16),
                   jax.ShapeDtypeStruct((SEQ, SB_DIM), BF16)],
        compiler_params=_params("parallel"),
        name="l0_inproj",
    )(x, g, w_ag, w_q, w_kt, w_v)


def _conv_kernel(prev_ref, cur_ref, cw_ref, cb_ref, lg_ref, lb_ref, o_ref, ext_ref, y_ref):
    tm = cur_ref.shape[0]
    ext_ref[0:CONV_HALO, :] = jnp.where(pl.program_id(0) > 0, prev_ref[...], 0.0)
    ext_ref[CONV_HALO:, :] = cur_ref[...]
    first_tap = CONV_HALO - (CONV_WIDTH - 1)

    def chunk(c, _):
        r0 = pl.multiple_of(c * CONV_ROWS, CONV_ROWS)
        for j in range(CONV_CH // LANES):
            lanes = slice(j * LANES, (j + 1) * LANES)
            win = ext_ref[pl.ds(r0, CONV_ROWS + CONV_HALO), lanes]
            acc = jnp.zeros((CONV_ROWS, LANES), F32) + cb_ref[:, lanes]
            for sub in range(SUBLANES):
                offsets = [o for o in range(first_tap, first_tap + CONV_WIDTH)
                           if o % SUBLANES == sub]
                shifted = pltpu.roll(win, win.shape[0] - sub, axis=0) if sub else win
                for o in offsets:
                    tap = cw_ref[o - first_tap:o - first_tap + 1, lanes]
                    acc = acc + shifted[o - sub:o - sub + CONV_ROWS, :] * tap
            y_ref[pl.ds(r0, CONV_ROWS), lanes] = acc
        return 0

    lax.fori_loop(0, tm // CONV_ROWS, chunk, 0)
    y = _layernorm(y_ref[...], lg_ref[...], lb_ref[...])
    o_ref[...] = (y * jax.nn.sigmoid(y)).astype(BF16)


def _conv(h, cw, cb, lg, lb):
    tm = ROW_TILE
    halo_blocks = tm // CONV_HALO
    return pl.pallas_call(
        _conv_kernel,
        grid=(SEQ // tm,),
        in_specs=[pl.BlockSpec((CONV_HALO, CONV_CH),
                               lambda i: (jnp.maximum(i * halo_blocks - 1, 0), 0)),
                  pl.BlockSpec((tm, CONV_CH), lambda i: (i, 0)),
                  _full((CONV_WIDTH, CONV_CH)), _full((1, CONV_CH)),
                  _full((1, CONV_CH)), _full((1, CONV_CH))],
        out_specs=pl.BlockSpec((tm, CONV_CH), lambda i: (i, 0)),
        out_shape=jax.ShapeDtypeStruct((SEQ, CONV_CH), BF16),
        scratch_shapes=[pltpu.VMEM((tm + CONV_HALO, CONV_CH), F32),
                        pltpu.VMEM((tm, CONV_CH), F32)],
        compiler_params=_params("parallel"),
        name="conv_ln_silu",
    )(h, h, cw, cb, lg, lb)


def _sb_kernel(q_ref, kt_ref, v_ref, o_ref, acc_ref, carry_ref):
    qi = pl.program_id(1)
    tq, tk = SB_TQ, SB_TK
    lane = lax.broadcasted_iota(jnp.int32, (tq, LANES), 1)
    q2 = q_ref[...]
    zero = jnp.zeros_like(q2)
    qh = (jnp.where(lane < SB_HEAD_DIM, q2, zero), jnp.where(lane >= SB_HEAD_DIM, q2, zero))
    jj = lax.broadcasted_iota(jnp.int32, (tk, tk), 0)
    ss = lax.broadcasted_iota(jnp.int32, (tk, tk), 1)
    minus_later = jnp.where(jj > ss, -1.0, 0.0).astype(BF16)
    minus_later2 = jnp.concatenate([minus_later, minus_later], axis=0)

    def scores(h, kb):
        return _dot(qh[h], kt_ref[:, pl.ds(pl.multiple_of(kb * tk, tk), tk)])

    def stay(z, valid):
        sp = jnp.maximum(z, 0.0) + jnp.log(1.0 + jnp.exp(-jnp.abs(z)))
        if valid is not None:
            sp = jnp.where(valid, sp, 0.0)
        hi = sp.astype(BF16)
        return sp, hi, (sp - hi.astype(F32)).astype(BF16)

    def later_sum(hi, lo):
        return _dot(jnp.concatenate([hi, lo], axis=1), minus_later2)

    def weighted_values(z, sp, after, valid, kb):
        w = jnp.exp((z - sp) + after)
        if valid is not None:
            w = jnp.where(valid, w, 0.0)
        return _dot(w.astype(BF16), v_ref[pl.ds(pl.multiple_of(kb * tk, tk), tk), :])

    def block(h, kb, valid):
        z = scores(h, kb)
        sp, hi, lo = stay(z, valid)
        pv = weighted_values(z, sp, later_sum(hi, lo), valid, kb)
        return pv, jnp.sum(sp, axis=1, keepdims=True)

    row = lax.broadcasted_iota(jnp.int32, (tq, tk), 0)
    col = lax.broadcasted_iota(jnp.int32, (tq, tk), 1)
    has_prev = qi > 0
    chains = [(h, kb, valid) for h in range(2)
              for kb, valid in ((qi, col < row), (jnp.maximum(qi - 1, 0), None))]
    zs = [scores(h, kb) for h, kb, _ in chains]
    stays = [stay(z, valid) for z, (_, _, valid) in zip(zs, chains)]
    afters = [later_sum(hi, lo) for _, hi, lo in stays]
    pvs = [weighted_values(z, sp, after, valid, kb)
           for z, (sp, _, _), after, (_, kb, valid) in zip(zs, stays, afters, chains)]
    sums = [jnp.sum(sp, axis=1, keepdims=True) for sp, _, _ in stays]
    for h in range(2):
        (pv_diag, pv_prev), (sp_diag, sp_prev) = pvs[2 * h:2 * h + 2], sums[2 * h:2 * h + 2]
        acc_ref[h] = pv_diag + jnp.where(has_prev, jnp.exp(-sp_diag), 0.0) * pv_prev
        carry_ref[h] = jnp.broadcast_to(-(sp_diag + sp_prev), (tq, LANES))

    def alive():
        return jnp.max(carry_ref[...]) > SB_LOG_UNDERFLOW

    def cond(state):
        kb, go = state
        return (kb >= 0) & go

    def body(state):
        kb, _ = state
        for h in range(2):
            pv, sp_sum = block(h, kb, None)
            carry = carry_ref[h]
            acc_ref[h] += jnp.exp(carry) * pv
            carry_ref[h] = carry - sp_sum
        return kb - 1, alive()

    lax.while_loop(cond, body, (qi - 2, alive()))
    o_ref[...] = jnp.where(lane < SB_HEAD_DIM, acc_ref[0], acc_ref[1]).astype(BF16)


def _sb_attention(q, kt, v):
    tq = SB_TQ
    return pl.pallas_call(
        _sb_kernel,
        grid=(SB_DIM // LANES, SEQ // tq),
        in_specs=[pl.BlockSpec((tq, LANES), lambda p, i: (i, p)),
                  pl.BlockSpec((LANES, SEQ), lambda p, i: (p, 0)),
                  pl.BlockSpec((SEQ, LANES), lambda p, i: (0, p))],
        out_specs=pl.BlockSpec((tq, LANES), lambda p, i: (i, p)),
        out_shape=jax.ShapeDtypeStruct((SEQ, SB_DIM), BF16),
        scratch_shapes=[pltpu.VMEM((2, tq, LANES), F32), pltpu.VMEM((2, tq, LANES), F32)],
        compiler_params=_params("parallel", "parallel"),
        name="sb_attention",
    )(q, kt, v)


def _router_gates(logits):
    col = lax.broadcasted_iota(jnp.int32, logits.shape, 1)
    colf = col.astype(F32)
    ninf = -jnp.inf
    first = lambda hit: jnp.min(jnp.where(hit, colf, float(LANES)), axis=1, keepdims=True)

    is_group = (col >= N_EXPERTS) & (col < N_EXPERTS + N_GROUPS)
    lg = jnp.where(is_group, logits, ninf)
    gmax = jnp.max(lg, axis=1, keepdims=True)
    pg_top = 1.0 / jnp.sum(jnp.exp(lg - gmax), axis=1, keepdims=True)
    g_idx = first(lg == gmax) - float(N_EXPERTS)

    in_group = (col < N_EXPERTS) & ((col // EXPERTS_PER_GROUP).astype(F32) == g_idx)
    le = jnp.where(in_group, logits, ninf)
    m1 = jnp.max(le, axis=1, keepdims=True)
    i1 = first(le == m1)
    le2 = jnp.where(colf == i1, ninf, le)
    m2 = jnp.max(le2, axis=1, keepdims=True)
    i2 = first(le2 == m2)
    ee = jnp.exp(le - m1)
    pe = ee / jnp.sum(ee, axis=1, keepdims=True)
    p1 = jnp.sum(jnp.where(colf == i1, pe, 0.0), axis=1, keepdims=True)
    p2 = jnp.sum(jnp.where(colf == i2, pe, 0.0), axis=1, keepdims=True)
    den = p1 + p2
    gates = jnp.where(colf == i1, p1 / den * pg_top,
                      jnp.where(colf == i2, p2 / den * pg_top, 0.0))
    return gates, g_idx


def _local_sort(g_idx):
    tm = g_idx.shape[0]
    lane = lax.broadcasted_iota(jnp.int32, (tm, LANES), 1)
    member = jnp.where(lane.astype(F32) == g_idx, 1.0, 0.0)
    r_i = lax.broadcasted_iota(jnp.int32, (tm, tm), 0)
    c_i = lax.broadcasted_iota(jnp.int32, (tm, tm), 1)
    before = jnp.where(c_i < r_i, 1.0, 0.0).astype(BF16)
    rank = jnp.sum(member * _dot(before, member.astype(BF16)), axis=1, keepdims=True)
    count = jnp.sum(member, axis=0, keepdims=True)
    padded = jnp.floor((count + (MOE_UNIT - 1)) * (1.0 / MOE_UNIT)) * MOE_UNIT
    lane1 = lax.broadcasted_iota(jnp.int32, (1, LANES), 1)
    offset = jnp.zeros((1, LANES), F32)
    start = jnp.zeros((1, 1), F32)
    for g in range(N_GROUPS):
        offset = offset + jnp.where((lane1 == g) | (lane1 == N_GROUPS + g), start, 0.0)
        start = start + jnp.sum(jnp.where(lane1 == g, padded, 0.0), axis=1, keepdims=True)
    dest = jnp.sum(member * offset, axis=1, keepdims=True) + rank
    rows = jnp.where(lane1 == N_GROUPS - 1, LS_TILE - offset, padded)
    units = jnp.where(lane1 < N_GROUPS, rows, offset) * (1.0 / MOE_UNIT)
    return dest, units.astype(jnp.int32)


def _pack_gates(gates):
    hi = gates.astype(BF16).astype(F32)
    rest = gates - hi
    mid = rest.astype(BF16).astype(F32)
    lo = rest - mid
    packed = hi + pltpu.roll(mid, GATE_TERM_STRIDE, axis=1) + pltpu.roll(lo, 2 * GATE_TERM_STRIDE, axis=1)
    return packed.astype(BF16)


def _unpack_gate(packed, expert):
    lane = lax.broadcasted_iota(jnp.int32, packed.shape, 1)
    terms = jnp.where(lane % GATE_TERM_STRIDE == expert, packed.astype(F32), 0.0)
    return jnp.sum(terms, axis=1, keepdims=True)


def _out_xattn_kernel(n_parts, *refs):
    h_ref = refs[0]
    part_refs = refs[1:1 + n_parts]
    w_refs = refs[1 + n_parts:1 + 2 * n_parts]
    (gx_ref, wq_ref, k_ref, v_ref, wo_ref, gf_ref, wr_ref, br_ref,
     h2_ref, xn_ls_ref, dest_ref, units_ref) = refs[1 + 2 * n_parts:]

    h1 = h_ref[...]
    for p_ref, w_ref in zip(part_refs, w_refs):
        h1 = h1 + _dot(p_ref[...], w_ref[...])

    xn = _rmsnorm(h1, gx_ref[...]).astype(BF16)
    q = (_dot(xn, wq_ref[...]) * (XA_HEAD_DIM ** -0.5)).astype(BF16)
    heads = []
    for hd in range(XA_HEADS):
        sl = slice(hd * XA_HEAD_DIM, (hd + 1) * XA_HEAD_DIM)
        s = lax.dot_general(q[:, sl], k_ref[0, :, sl], (((1,), (1,)), ((), ())),
                            preferred_element_type=F32)
        e = jnp.exp(s - jnp.max(s, axis=1, keepdims=True))
        p = e / jnp.sum(e, axis=1, keepdims=True)
        heads.append(_dot(p.astype(BF16), v_ref[0, :, sl]).astype(BF16))
    h2 = h1 + _dot(jnp.concatenate(heads, axis=1), wo_ref[...])
    h2_ref[...] = h2

    xf = _rmsnorm(h2, gf_ref[...])
    x_hi = xf.astype(BF16)
    x_lo = (xf - x_hi.astype(F32)).astype(BF16)
    both = _dot(x_hi, wr_ref[...])
    logits = both[:, :LANES] + both[:, LANES:] + _dot(x_lo, wr_ref[:, :LANES]) + br_ref[...]
    gates, g_idx = _router_gates(logits)

    dest, units = _local_sort(g_idx)
    tm = dest.shape[0]
    dest_ref[...] = jnp.broadcast_to(dest, (tm, LANES))
    units_ref[0] = jnp.broadcast_to(units, (SUBLANES, LANES))
    dest_row = jnp.transpose(jnp.broadcast_to(dest, (tm, LANES)))[0:1, :]
    slot = lax.broadcasted_iota(jnp.int32, (LS_TILE, tm), 0).astype(F32)
    place = jnp.where(slot == dest_row, 1.0, 0.0).astype(BF16)
    routed = jnp.concatenate([x_hi, _pack_gates(gates)], axis=1)
    xn_ls_ref[...] = _dot(place, routed).astype(BF16)


def _out_xattn(h, parts, weights, gx, wq, k_all, v_all, wo, gf, wr, br, layer):
    tm = ROW_TILE
    row = lambda n: pl.BlockSpec((tm, n), lambda i: (i, 0))
    ls_row = lambda n: pl.BlockSpec((LS_TILE, n), lambda i: (i, 0))
    kv_spec = pl.BlockSpec((1, MEM_LEN, D_MODEL), lambda i: (layer, 0, 0))
    in_specs = ([row(D_MODEL)] + [row(p.shape[1]) for p in parts]
                + [_full(w.shape) for w in weights]
                + [_full((1, D_MODEL)), _full((D_MODEL, D_MODEL)), kv_spec, kv_spec,
                   _full((D_MODEL, D_MODEL)), _full((1, D_MODEL)),
                   _full((D_MODEL, 2 * LANES)), _full((1, LANES))])
    return pl.pallas_call(
        functools.partial(_out_xattn_kernel, len(parts)),
        grid=(N_ROW_TILES,),
        in_specs=in_specs,
        out_specs=[row(D_MODEL), ls_row(LS_WIDTH), row(LANES),
                   pl.BlockSpec((1, SUBLANES, LANES), lambda i: (i, 0, 0))],
        out_shape=[jax.ShapeDtypeStruct((SEQ, D_MODEL), F32),
                   jax.ShapeDtypeStruct((N_ROW_TILES * LS_TILE, LS_WIDTH), BF16),
                   jax.ShapeDtypeStruct((SEQ, LANES), F32),
                   jax.ShapeDtypeStruct((N_ROW_TILES, SUBLANES, LANES), jnp.int32)],
        compiler_params=_params("parallel"),
        name="out_xattn_router",
    )(h, *parts, *weights, gx, wq, k_all, v_all, wo, gf, wr, br)


def _route_index_kernel(units_ref, unit_of_ref, tile_group_ref):
    def no_unit(k, _):
        unit_of_ref[k] = -1
        return 0

    def no_group(j, _):
        tile_group_ref[j] = -1
        return 0

    pos = 0
    for g in range(N_GROUPS):
        first_tile = pos // SORT_UNITS

        def per_row_tile(i, pos):
            n = units_ref[i, g]
            base = i * LS_UNITS + units_ref[i, N_GROUPS + g]

            def per_unit(k, _):
                unit_of_ref[pos + k] = base + k
                return 0

            lax.fori_loop(0, n, per_unit, 0)
            return pos + n

        pos = lax.fori_loop(0, N_ROW_TILES, per_row_tile, pos)
        end_tile = (pos + SORT_UNITS - 1) // SORT_UNITS

        def mark_tile(j, _):
            tile_group_ref[j] = g
            return 0

        lax.fori_loop(first_tile, end_tile, mark_tile, 0)
        lax.fori_loop(pos, end_tile * SORT_UNITS, no_unit, 0)
        pos = end_tile * SORT_UNITS

    lax.fori_loop(pos, N_SORT_TILES * SORT_UNITS, no_unit, 0)
    lax.fori_loop(pos // SORT_UNITS, N_SORT_TILES, no_group, 0)


def _route_index(units):
    smem = lambda: pl.BlockSpec(memory_space=pltpu.SMEM)
    return pl.pallas_call(
        _route_index_kernel,
        in_specs=[smem()],
        out_specs=[smem(), smem()],
        out_shape=[jax.ShapeDtypeStruct((N_SORT_TILES * SORT_UNITS,), jnp.int32),
                   jax.ShapeDtypeStruct((N_SORT_TILES,), jnp.int32)],
        name="moe_route_index",
    )(units)


def _moe_kernel(unit_of_ref, tile_group_ref, xn_hbm, wg_ref, wu_ref, wd_ref, y_hbm,
                xbuf, ybuf, wg_bf, wu_bf, wd_bf, gather_sem, scatter_sem):
    j = pl.program_id(0)
    n_tiles = pl.num_programs(0)
    slot = j % 2

    def unit_rows(k):
        return pl.ds(pl.multiple_of(k * MOE_UNIT, MOE_UNIT), MOE_UNIT)

    def gather_copy(s, k, u):
        return pltpu.make_async_copy(xn_hbm.at[unit_rows(u), :], xbuf.at[s, unit_rows(k), :],
                                     gather_sem.at[s])

    def scatter_copy(s, k, u):
        return pltpu.make_async_copy(ybuf.at[s, unit_rows(k), :], y_hbm.at[unit_rows(u), :],
                                     scatter_sem.at[s])

    def is_full(tile):
        return unit_of_ref[tile * SORT_UNITS + SORT_UNITS - 1] >= 0

    def for_units(tile, full_fn, partial_fn):
        def run(fn, unroll):
            def unit(k, _):
                fn(k, unit_of_ref[tile * SORT_UNITS + k])
                return 0

            lax.fori_loop(0, SORT_UNITS, unit, 0, unroll=unroll)

        @pl.when(is_full(tile))
        def _():
            run(full_fn, 8)

        @pl.when(jnp.logical_not(is_full(tile)))
        def _():
            def guarded(k, u):
                @pl.when(u >= 0)
                def _():
                    full_fn(k, u)

                if partial_fn is not None:
                    @pl.when(u < 0)
                    def _():
                        partial_fn(k)

            run(guarded, 1)

    def start_gather(tile, s):
        def zero_fill(k):
            xbuf[s, unit_rows(k), :] = jnp.zeros((MOE_UNIT, LS_WIDTH), BF16)

        for_units(tile, lambda k, u: gather_copy(s, k, u).start(), zero_fill)

    def start_scatter(tile, s):
        for_units(tile, lambda k, u: scatter_copy(s, k, u).start(), None)

    def wait_units(tile, whole_copy, unit_copy):
        @pl.when(is_full(tile))
        def _():
            whole_copy.wait()

        @pl.when(jnp.logical_not(is_full(tile)))
        def _():
            def unit(k, _):
                u = unit_of_ref[tile * SORT_UNITS + k]

                @pl.when(u >= 0)
                def _():
                    unit_copy(k, u).wait()

                return 0

            lax.fori_loop(0, SORT_UNITS, unit, 0)

    def wait_gather(tile, s):
        whole = pltpu.make_async_copy(xn_hbm.at[pl.ds(0, ROW_TILE), :], xbuf.at[s], gather_sem.at[s])
        wait_units(tile, whole, lambda k, u: gather_copy(s, k, u))

    def wait_scatter(tile, s):
        whole = pltpu.make_async_copy(ybuf.at[s], y_hbm.at[pl.ds(0, ROW_TILE), :], scatter_sem.at[s])
        wait_units(tile, whole, lambda k, u: scatter_copy(s, k, u))

    @pl.when(j == 0)
    def _():
        start_gather(0, 0)

    @pl.when(j + 1 < n_tiles)
    def _():
        start_gather(j + 1, 1 - slot)

    grp = tile_group_ref[j]
    new_group = (j == 0) | (grp != tile_group_ref[jnp.maximum(j - 1, 0)])

    @pl.when((grp >= 0) & new_group)
    def _():
        wg_bf[...] = wg_ref[...].astype(BF16)
        wu_bf[...] = wu_ref[...].astype(BF16)
        wd_bf[...] = wd_ref[...].astype(BF16)

    wait_gather(j, slot)

    @pl.when(j >= 2)
    def _():
        wait_scatter(j - 2, slot)

    @pl.when(grp >= 0)
    def _():
        xn = xbuf[slot, :, :D_MODEL]
        packed_gates = xbuf[slot, :, D_MODEL:]
        for e in range(EXPERTS_PER_GROUP):
            gate = _unpack_gate(packed_gates, grp * EXPERTS_PER_GROUP + e)
            hg = _dot(xn, wg_bf[e])
            hu = _dot(xn, wu_bf[e])
            act = (hg * jax.nn.sigmoid(hg)) * hu * gate
            out = _dot(act.astype(BF16), wd_bf[e])
            if e == 0:
                ybuf[slot] = out
            else:
                ybuf[slot] += out

    start_scatter(j, slot)

    @pl.when(j == n_tiles - 1)
    def _():
        wait_scatter(j - 1, 1 - slot)
        wait_scatter(j, slot)


def _moe(unit_of, tile_group, xn_ls, wg, wu, wd, layer):
    group_of = lambda j, unit_of, tile_group: jnp.where(tile_group[j] < 0, N_GROUPS - 1,
                                                        tile_group[j])
    up_spec = pl.BlockSpec((None, EXPERTS_PER_GROUP, D_MODEL, D_EXPERT),
                           lambda j, u, t: (layer, group_of(j, u, t), 0, 0))
    down_spec = pl.BlockSpec((None, EXPERTS_PER_GROUP, D_EXPERT, D_MODEL),
                             lambda j, u, t: (layer, group_of(j, u, t), 0, 0))
    any_spec = pl.BlockSpec(memory_space=pl.ANY)
    return pl.pallas_call(
        _moe_kernel,
        grid_spec=pltpu.PrefetchScalarGridSpec(
            num_scalar_prefetch=2,
            grid=(N_SORT_TILES,),
            in_specs=[any_spec, up_spec, up_spec, down_spec],
            out_specs=any_spec,
            scratch_shapes=[pltpu.VMEM((2, ROW_TILE, LS_WIDTH), BF16),
                            pltpu.VMEM((2, ROW_TILE, D_MODEL), F32),
                            pltpu.VMEM((EXPERTS_PER_GROUP, D_MODEL, D_EXPERT), BF16),
                            pltpu.VMEM((EXPERTS_PER_GROUP, D_MODEL, D_EXPERT), BF16),
                            pltpu.VMEM((EXPERTS_PER_GROUP, D_EXPERT, D_MODEL), BF16),
                            pltpu.SemaphoreType.DMA((2,)),
                            pltpu.SemaphoreType.DMA((2,))]),
        out_shape=jax.ShapeDtypeStruct((N_ROW_TILES * LS_TILE, D_MODEL), F32),
        compiler_params=_params("arbitrary"),
        name="moe_experts",
    )(unit_of, tile_group, xn_ls, wg, wu, wd)


def _unsort_kernel(final_norm, h_ref, y_ls_ref, dest_ref, gfin_ref, o_ref):
    tm = h_ref.shape[0]
    slot = lax.broadcasted_iota(jnp.int32, (tm, LS_TILE), 1).astype(F32)
    pick = jnp.where(slot == dest_ref[:, 0:1], 1.0, 0.0).astype(BF16)
    y = y_ls_ref[...]
    y_hi = y.astype(BF16)
    y_lo = (y - y_hi.astype(F32)).astype(BF16)
    out = h_ref[...] + (_dot(pick, y_hi) + _dot(pick, y_lo))
    if final_norm:
        out = _rmsnorm(out, gfin_ref[...])
    o_ref[...] = out


def _unsort(h, y_ls, dest, gfin, final_norm):
    tm = ROW_TILE
    row = lambda n: pl.BlockSpec((tm, n), lambda i: (i, 0))
    return pl.pallas_call(
        functools.partial(_unsort_kernel, final_norm),
        grid=(N_ROW_TILES,),
        in_specs=[row(D_MODEL), pl.BlockSpec((LS_TILE, D_MODEL), lambda i: (i, 0)), row(LANES),
                  _full((1, D_MODEL))],
        out_specs=row(D_MODEL),
        out_shape=jax.ShapeDtypeStruct((SEQ, D_MODEL), F32),
        compiler_params=_params("parallel"),
        name="moe_unsort_residual",
    )(h, y_ls, dest, gfin)


def _l1_mixer_kernel(x_ref, g_ref, w_in_ref, vg_ref, vb_ref, ws_ref, bs_ref, o_ref):
    tm = x_ref.shape[0]
    xn = _rmsnorm(x_ref[...], g_ref[...]).astype(BF16)
    p = _dot(xn, w_in_ref[...])
    p = 0.5 * p * (1.0 + lax.erf(p * (2.0 ** -0.5)))
    u = p[:, :D_MODEL]
    v = _layernorm(p[:, D_MODEL:], vg_ref[...], vb_ref[...]).astype(BF16)
    t_idx = lax.broadcasted_iota(jnp.int32, (GM_CHUNK, GM_CHUNK), 0)
    s_idx = lax.broadcasted_iota(jnp.int32, (GM_CHUNK, GM_CHUNK), 1)
    for g in range(GM_GROUPS):
        cols = slice(g * LANES, (g + 1) * LANES)
        wmix = jnp.where(t_idx >= s_idx, ws_ref[g], 0.0).astype(BF16)
        for c in range(tm // GM_CHUNK):
            rows = slice(c * GM_CHUNK, (c + 1) * GM_CHUNK)
            mixed = _dot(wmix, v[rows, cols]) + bs_ref[g]
            o_ref[rows, cols] = (u[rows, cols] * mixed).astype(BF16)


def _l1_mixer(x, g, w_in, vg, vb, ws, bs_b):
    tm = ROW_TILE
    return pl.pallas_call(
        _l1_mixer_kernel,
        grid=(SEQ // tm,),
        in_specs=[pl.BlockSpec((tm, D_MODEL), lambda i: (i, 0)), _full((1, D_MODEL)),
                  _full((D_MODEL, 2 * D_MODEL)), _full((1, D_MODEL)), _full((1, D_MODEL)),
                  _full((GM_GROUPS, GM_CHUNK, GM_CHUNK)), _full((GM_GROUPS, GM_CHUNK, LANES))],
        out_specs=pl.BlockSpec((tm, D_MODEL), lambda i: (i, 0)),
        out_shape=jax.ShapeDtypeStruct((SEQ, D_MODEL), BF16),
        compiler_params=_params("parallel"),
        name="l1_mixer",
    )(x, g, w_in, vg, vb, ws, bs_b)


def _router_weights(we, wg, be, bg):
    pad = LANES - N_EXPERTS - N_GROUPS
    w = jnp.concatenate([we, wg, jnp.zeros((D_MODEL, pad), F32)], axis=1)
    b = jnp.concatenate([be, bg, jnp.zeros((pad,), F32)])[None, :]
    w_hi = w.astype(BF16)
    w_lo = (w - w_hi.astype(F32)).astype(BF16)
    return jnp.concatenate([w_hi, w_lo], axis=1), b


def kernel(x, mem, mem_norm_g, norm_mix, norm_xa, norm_ffn, final_norm_g, ab_w_in, ab_conv_w, ab_conv_b, ab_cnorm_g, ab_cnorm_b, ab_w_out, c_w_in, c_vnorm_g, c_vnorm_b, c_ws, c_bs, c_w_out, xa_wq, xa_wk, xa_wv, xa_wo, rt_wg, rt_bg, rt_we, rt_be, ex_w_gate, ex_w_up, ex_w_down):
    assert x.shape == (1, SEQ, D_MODEL) and mem.shape == (1, MEM_LEN, D_MODEL)
    bf = lambda a: a.astype(BF16)
    r1 = lambda a: a.reshape(1, -1)
    h = x[0]

    k_all, v_all = _mem_kv(mem[0], r1(mem_norm_g), bf(xa_wk), bf(xa_wv))

    def tail(h, parts, weights, i, final_norm):
        wr, br = _router_weights(rt_we[i], rt_wg[i], rt_be[i], rt_bg[i])
        h2, xn_ls, dest, units = _out_xattn(
            h, parts, weights, r1(norm_xa[i]), bf(xa_wq[i]), k_all, v_all, bf(xa_wo[i]),
            r1(norm_ffn[i]), wr, br, i)
        unit_of, tile_group = _route_index(units[:, 0, :])
        y_ls = _moe(unit_of, tile_group, xn_ls, ex_w_gate, ex_w_up, ex_w_down, i)
        return _unsort(h2, y_ls, dest, r1(final_norm_g), final_norm)

    w_in = ab_w_in[0]
    q_lo, k_lo, v_lo = 2 * CONV_CH, 2 * CONV_CH + SB_DIM, 2 * CONV_CH + 2 * SB_DIM
    hglu, q, kt, v = _l0_inproj(h, r1(norm_mix[0]), bf(w_in[:, :q_lo]), bf(w_in[:, q_lo:k_lo]),
                                bf(w_in[:, k_lo:v_lo].T), bf(w_in[:, v_lo:]))
    conv_out = _conv(hglu, ab_conv_w[0], r1(ab_conv_b[0]), r1(ab_cnorm_g[0]), r1(ab_cnorm_b[0]))
    sb_out = _sb_attention(q, kt, v)
    w_out = bf(ab_w_out[0])
    h = tail(h, [conv_out, sb_out], [w_out[:CONV_CH], w_out[CONV_CH:]], 0, False)

    bs_b = jnp.broadcast_to(c_bs[0][:, :, None], (GM_GROUPS, GM_CHUNK, LANES))
    gated = _l1_mixer(h, r1(norm_mix[1]), bf(c_w_in[0]), r1(c_vnorm_g[0]), r1(c_vnorm_b[0]),
                      c_ws[0], bs_b)
    h = tail(h, [gated], [bf(c_w_out[0])], 1, True)
    return h[None]
```

```python
import functools

import jax
import jax.numpy as jnp
from jax import lax
from jax.experimental import pallas as pl
from jax.experimental.pallas import tpu as pltpu

D_MODEL = 1024
SEQ = 16384
MEM_LEN = 256
EPS = 1e-6
CONV_CH = 512
CONV_WIDTH = 31
SB_HEADS = 8
SB_HEAD_DIM = 64
SB_DIM = 512
GM_GROUPS = 8
GM_CHUNK = 128
XA_HEADS = 4
XA_HEAD_DIM = 256
N_GROUPS = 4
EXPERTS_PER_GROUP = 4
N_EXPERTS = 16
D_EXPERT = 256

LANES = 128
SUBLANES = 8
ROW_TILE = 512
XA_ROWS = 512
SB_TQ = 256
SB_TK = 256
SB_LOG_UNDERFLOW = -104.0
CONV_HALO = 32
CONV_ROWS = 64
MOE_UNIT = 16
N_ROW_TILES = SEQ // ROW_TILE
LS_TILE = ROW_TILE + N_GROUPS * MOE_UNIT
LS_UNITS = LS_TILE // MOE_UNIT
SORT_UNITS = ROW_TILE // MOE_UNIT
N_SORT_TILES = N_ROW_TILES * LS_UNITS // SORT_UNITS + N_GROUPS
GATE_TERM_STRIDE = 32
LS_WIDTH = D_MODEL + LANES
VMEM_LIMIT = 56 * 1024 * 1024

BF16 = jnp.bfloat16
F32 = jnp.float32


def _params(*semantics):
    return pltpu.CompilerParams(dimension_semantics=semantics, vmem_limit_bytes=VMEM_LIMIT)


def _dot(a, b):
    return jnp.dot(a, b, preferred_element_type=F32)


def _rmsnorm(x, g):
    return x * lax.rsqrt(jnp.mean(x * x, axis=-1, keepdims=True) + EPS) * g


def _layernorm(x, g, b):
    mu = jnp.mean(x, axis=-1, keepdims=True)
    xc = x - mu
    var = jnp.mean(xc * xc, axis=-1, keepdims=True)
    return xc * lax.rsqrt(var + EPS) * g + b


def _full(shape):
    return pl.BlockSpec(shape, lambda *_: (0,) * len(shape))


def _mem_kv_kernel(mem_ref, g_ref, wk_ref, wv_ref, wq_ref, wo_ref, qk_ref, vo_ref):
    memn = _rmsnorm(mem_ref[...], g_ref[...]).astype(BF16)
    k = _dot(memn, wk_ref[0]).astype(BF16)
    v = _dot(memn, wv_ref[0]).astype(BF16)
    for hd in range(XA_HEADS):
        sl = slice(hd * XA_HEAD_DIM, (hd + 1) * XA_HEAD_DIM)
        qk = lax.dot_general(wq_ref[0, :, sl], k[:, sl], (((1,), (1,)), ((), ())),
                             preferred_element_type=F32)
        qk_ref[0, :, hd * MEM_LEN:(hd + 1) * MEM_LEN] = (qk * (XA_HEAD_DIM ** -0.5)).astype(BF16)
        vo_ref[0, hd * MEM_LEN:(hd + 1) * MEM_LEN, :] = _dot(v[:, sl], wo_ref[0, sl, :]).astype(BF16)


def _mem_kv(mem, g, wk, wv, wq, wo):
    depth = wk.shape[0]
    wspec = pl.BlockSpec((1, D_MODEL, D_MODEL), lambda i: (i, 0, 0))
    qk_spec = pl.BlockSpec((1, D_MODEL, XA_HEADS * MEM_LEN), lambda i: (i, 0, 0))
    vo_spec = pl.BlockSpec((1, XA_HEADS * MEM_LEN, D_MODEL), lambda i: (i, 0, 0))
    return pl.pallas_call(
        _mem_kv_kernel,
        grid=(depth,),
        in_specs=[_full((MEM_LEN, D_MODEL)), _full((1, D_MODEL)), wspec, wspec, wspec, wspec],
        out_specs=[qk_spec, vo_spec],
        out_shape=[jax.ShapeDtypeStruct((depth, D_MODEL, XA_HEADS * MEM_LEN), BF16),
                   jax.ShapeDtypeStruct((depth, XA_HEADS * MEM_LEN, D_MODEL), BF16)],
        compiler_params=_params("arbitrary"),
        name="mem_kv",
    )(mem, g, wk, wv, wq, wo)


def _l0_inproj_kernel(x_ref, g_ref, w_ag_ref, w_q_ref, w_kt_ref, w_v_ref,
                      h_ref, q_ref, kt_ref, v_ref):
    xn = _rmsnorm(x_ref[...], g_ref[...]).astype(BF16)
    ag = _dot(xn, w_ag_ref[...])
    h_ref[...] = ag[:, :CONV_CH] * jax.nn.sigmoid(ag[:, CONV_CH:])
    q_ref[...] = (_dot(xn, w_q_ref[...]) * (SB_HEAD_DIM ** -0.5)).astype(BF16)
    kt_ref[...] = lax.dot_general(w_kt_ref[...], xn, (((1,), (1,)), ((), ())),
                                  preferred_element_type=F32).astype(BF16)
    v_ref[...] = _dot(xn, w_v_ref[...]).astype(BF16)


def _l0_inproj(x, g, w_ag, w_q, w_kt, w_v):
    tm = ROW_TILE
    row = lambda n: pl.BlockSpec((tm, n), lambda i: (i, 0))
    return pl.pallas_call(
        _l0_inproj_kernel,
        grid=(SEQ // tm,),
        in_specs=[row(D_MODEL), _full((1, D_MODEL)), _full((D_MODEL, 2 * CONV_CH)),
                  _full((D_MODEL, SB_DIM)), _full((SB_DIM, D_MODEL)), _full((D_MODEL, SB_DIM))],
        out_specs=[row(CONV_CH), row(SB_DIM), pl.BlockSpec((SB_DIM, tm), lambda i: (0, i)),
                   row(SB_DIM)],
        out_shape=[jax.ShapeDtypeStruct((SEQ, CONV_CH), F32),
                   jax.ShapeDtypeStruct((SEQ, SB_DIM), BF16),
                   jax.ShapeDtypeStruct((SB_DIM, SEQ), BF16),
                   jax.ShapeDtypeStruct((SEQ, SB_DIM), BF16)],
        compiler_params=_params("parallel"),
        name="l0_inproj",
    )(x, g, w_ag, w_q, w_kt, w_v)


def _conv_kernel(prev_ref, cur_ref, cw_ref, cb_ref, lg_ref, lb_ref, o_ref, ext_ref, y_ref):
    tm = cur_ref.shape[0]
    ext_ref[0:CONV_HALO, :] = jnp.where(pl.program_id(0) > 0, prev_ref[...], 0.0)
    ext_ref[CONV_HALO:, :] = cur_ref[...]
    first_tap = CONV_HALO - (CONV_WIDTH - 1)

    def chunk(c, _):
        r0 = pl.multiple_of(c * CONV_ROWS, CONV_ROWS)
        for j in range(CONV_CH // LANES):
            lanes = slice(j * LANES, (j + 1) * LANES)
            win = ext_ref[pl.ds(r0, CONV_ROWS + CONV_HALO), lanes]
            acc = jnp.zeros((CONV_ROWS, LANES), F32) + cb_ref[:, lanes]
            for sub in range(SUBLANES):
                offsets = [o for o in range(first_tap, first_tap + CONV_WIDTH)
                           if o % SUBLANES == sub]
                shifted = pltpu.roll(win, win.shape[0] - sub, axis=0) if sub else win
                for o in offsets:
                    tap = cw_ref[o - first_tap:o - first_tap + 1, lanes]
                    acc = acc + shifted[o - sub:o - sub + CONV_ROWS, :] * tap
            y_ref[pl.ds(r0, CONV_ROWS), lanes] = acc
        return 0

    lax.fori_loop(0, tm // CONV_ROWS, chunk, 0)
    y = _layernorm(y_ref[...], lg_ref[...], lb_ref[...])
    o_ref[...] = (y * jax.nn.sigmoid(y)).astype(BF16)


def _conv(h, cw, cb, lg, lb):
    tm = ROW_TILE
    halo_blocks = tm // CONV_HALO
    return pl.pallas_call(
        _conv_kernel,
        grid=(SEQ // tm,),
        in_specs=[pl.BlockSpec((CONV_HALO, CONV_CH),
                               lambda i: (jnp.maximum(i * halo_blocks - 1, 0), 0)),
                  pl.BlockSpec((tm, CONV_CH), lambda i: (i, 0)),
                  _full((CONV_WIDTH, CONV_CH)), _full((1, CONV_CH)),
                  _full((1, CONV_CH)), _full((1, CONV_CH))],
        out_specs=pl.BlockSpec((tm, CONV_CH), lambda i: (i, 0)),
        out_shape=jax.ShapeDtypeStruct((SEQ, CONV_CH), BF16),
        scratch_shapes=[pltpu.VMEM((tm + CONV_HALO, CONV_CH), F32),
                        pltpu.VMEM((tm, CONV_CH), F32)],
        compiler_params=_params("parallel"),
        name="conv_ln_silu",
    )(h, h, cw, cb, lg, lb)


def _sb_kernel(q_ref, kt_ref, v_ref, o_ref, acc_ref, carry_ref):
    qi = pl.program_id(1)
    tq, tk = SB_TQ, SB_TK
    lane = lax.broadcasted_iota(jnp.int32, (tq, LANES), 1)
    q2 = q_ref[...]
    zero = jnp.zeros_like(q2)
    qh = (jnp.where(lane < SB_HEAD_DIM, q2, zero), jnp.where(lane >= SB_HEAD_DIM, q2, zero))
    jj = lax.broadcasted_iota(jnp.int32, (tk, tk), 0)
    ss = lax.broadcasted_iota(jnp.int32, (tk, tk), 1)
    minus_later = jnp.where(jj > ss, -1.0, 0.0).astype(BF16)
    minus_later2 = jnp.concatenate([minus_later, minus_later], axis=0)

    def scores(h, kb):
        return _dot(qh[h], kt_ref[:, pl.ds(pl.multiple_of(kb * tk, tk), tk)])

    def stay(z, valid):
        sp = jnp.maximum(z, 0.0) + jnp.log(1.0 + jnp.exp(-jnp.abs(z)))
        if valid is not None:
            sp = jnp.where(valid, sp, 0.0)
        hi = sp.astype(BF16)
        return sp, hi, (sp - hi.astype(F32)).astype(BF16)

    def later_sum(hi, lo):
        return _dot(jnp.concatenate([hi, lo], axis=1), minus_later2)

    def weighted_values(z, sp, after, valid, kb):
        w = jnp.exp((z - sp) + after)
        if valid is not None:
            w = jnp.where(valid, w, 0.0)
        return _dot(w.astype(BF16), v_ref[pl.ds(pl.multiple_of(kb * tk, tk), tk), :])

    def block(h, kb, valid):
        z = scores(h, kb)
        sp, hi, lo = stay(z, valid)
        pv = weighted_values(z, sp, later_sum(hi, lo), valid, kb)
        return pv, jnp.sum(sp, axis=1, keepdims=True)

    row = lax.broadcasted_iota(jnp.int32, (tq, tk), 0)
    col = lax.broadcasted_iota(jnp.int32, (tq, tk), 1)
    has_prev = qi > 0
    chains = [(h, kb, valid) for h in range(2)
              for kb, valid in ((qi, col < row), (jnp.maximum(qi - 1, 0), None))]
    zs = [scores(h, kb) for h, kb, _ in chains]
    stays = [stay(z, valid) for z, (_, _, valid) in zip(zs, chains)]
    afters = [later_sum(hi, lo) for _, hi, lo in stays]
    pvs = [weighted_values(z, sp, after, valid, kb)
           for z, (sp, _, _), after, (_, kb, valid) in zip(zs, stays, afters, chains)]
    sums = [jnp.sum(sp, axis=1, keepdims=True) for sp, _, _ in stays]
    for h in range(2):
        (pv_diag, pv_prev), (sp_diag, sp_prev) = pvs[2 * h:2 * h + 2], sums[2 * h:2 * h + 2]
        acc_ref[h] = pv_diag + jnp.where(has_prev, jnp.exp(-sp_diag), 0.0) * pv_prev
        carry_ref[h] = jnp.broadcast_to(-(sp_diag + sp_prev), (tq, LANES))

    def alive():
        return jnp.max(carry_ref[...]) > SB_LOG_UNDERFLOW

    def cond(state):
        kb, go = state
        return (kb >= 0) & go

    def body(state):
        kb, _ = state
        for h in range(2):
            pv, sp_sum = block(h, kb, None)
            carry = carry_ref[h]
            acc_ref[h] += jnp.exp(carry) * pv
            carry_ref[h] = carry - sp_sum
        return kb - 1, alive()

    lax.while_loop(cond, body, (qi - 2, alive()))
    o_ref[...] = jnp.where(lane < SB_HEAD_DIM, acc_ref[0], acc_ref[1]).astype(BF16)


def _sb_attention(q, kt, v):
    tq = SB_TQ
    return pl.pallas_call(
        _sb_kernel,
        grid=(SB_DIM // LANES, SEQ // tq),
        in_specs=[pl.BlockSpec((tq, LANES), lambda p, i: (i, p)),
                  pl.BlockSpec((LANES, SEQ), lambda p, i: (p, 0)),
                  pl.BlockSpec((SEQ, LANES), lambda p, i: (0, p))],
        out_specs=pl.BlockSpec((tq, LANES), lambda p, i: (i, p)),
        out_shape=jax.ShapeDtypeStruct((SEQ, SB_DIM), BF16),
        scratch_shapes=[pltpu.VMEM((2, tq, LANES), F32), pltpu.VMEM((2, tq, LANES), F32)],
        compiler_params=_params("parallel", "parallel"),
        name="sb_attention",
    )(q, kt, v)


def _router_gates(logits):
    col = lax.broadcasted_iota(jnp.int32, logits.shape, 1)
    colf = col.astype(F32)
    ninf = -jnp.inf
    first = lambda hit: jnp.min(jnp.where(hit, colf, float(LANES)), axis=1, keepdims=True)

    is_group = (col >= N_EXPERTS) & (col < N_EXPERTS + N_GROUPS)
    lg = jnp.where(is_group, logits, ninf)
    gmax = jnp.max(lg, axis=1, keepdims=True)
    pg_top = 1.0 / jnp.sum(jnp.exp(lg - gmax), axis=1, keepdims=True)
    g_idx = first(lg == gmax) - float(N_EXPERTS)

    in_group = (col < N_EXPERTS) & ((col // EXPERTS_PER_GROUP).astype(F32) == g_idx)
    le = jnp.where(in_group, logits, ninf)
    m1 = jnp.max(le, axis=1, keepdims=True)
    i1 = first(le == m1)
    le2 = jnp.where(colf == i1, ninf, le)
    m2 = jnp.max(le2, axis=1, keepdims=True)
    i2 = first(le2 == m2)
    ee = jnp.exp(le - m1)
    pe = ee / jnp.sum(ee, axis=1, keepdims=True)
    p1 = jnp.sum(jnp.where(colf == i1, pe, 0.0), axis=1, keepdims=True)
    p2 = jnp.sum(jnp.where(colf == i2, pe, 0.0), axis=1, keepdims=True)
    den = p1 + p2
    gates = jnp.where(colf == i1, p1 / den * pg_top,
                      jnp.where(colf == i2, p2 / den * pg_top, 0.0))
    return gates, g_idx


def _local_sort(g_idx):
    tm = g_idx.shape[0]
    lane = lax.broadcasted_iota(jnp.int32, (tm, LANES), 1)
    member = jnp.where(lane.astype(F32) == g_idx, 1.0, 0.0)
    r_i = lax.broadcasted_iota(jnp.int32, (tm, tm), 0)
    c_i = lax.broadcasted_iota(jnp.int32, (tm, tm), 1)
    before = jnp.where(c_i < r_i, 1.0, 0.0).astype(BF16)
    rank = jnp.sum(member * _dot(before, member.astype(BF16)), axis=1, keepdims=True)
    count = jnp.sum(member, axis=0, keepdims=True)
    padded = jnp.floor((count + (MOE_UNIT - 1)) * (1.0 / MOE_UNIT)) * MOE_UNIT
    lane1 = lax.broadcasted_iota(jnp.int32, (1, LANES), 1)
    offset = jnp.zeros((1, LANES), F32)
    start = jnp.zeros((1, 1), F32)
    for g in range(N_GROUPS):
        offset = offset + jnp.where((lane1 == g) | (lane1 == N_GROUPS + g), start, 0.0)
        start = start + jnp.sum(jnp.where(lane1 == g, padded, 0.0), axis=1, keepdims=True)
    dest = jnp.sum(member * offset, axis=1, keepdims=True) + rank
    rows = jnp.where(lane1 == N_GROUPS - 1, LS_TILE - offset, padded)
    units = jnp.where(lane1 < N_GROUPS, rows, offset) * (1.0 / MOE_UNIT)
    return dest, units.astype(jnp.int32)


def _pack_gates(gates):
    hi = gates.astype(BF16).astype(F32)
    rest = gates - hi
    mid = rest.astype(BF16).astype(F32)
    lo = rest - mid
    packed = hi + pltpu.roll(mid, GATE_TERM_STRIDE, axis=1) + pltpu.roll(lo, 2 * GATE_TERM_STRIDE, axis=1)
    return packed.astype(BF16)


def _unpack_gate(packed, expert):
    lane = lax.broadcasted_iota(jnp.int32, packed.shape, 1)
    terms = jnp.where(lane % GATE_TERM_STRIDE == expert, packed.astype(F32), 0.0)
    return jnp.sum(terms, axis=1, keepdims=True)


def _out_xattn_kernel(n_parts, *refs):
    h_ref = refs[0]
    part_refs = refs[1:1 + n_parts]
    w_refs = refs[1 + n_parts:1 + 2 * n_parts]
    (gx_ref, qk_ref, vo_ref, gf_ref, wr_ref, br_ref,
     h2_ref, xn_ls_ref, dest_ref, units_ref) = refs[1 + 2 * n_parts:]

    tm = h_ref.shape[0]
    blocks = [slice(r, r + XA_ROWS) for r in range(0, tm, XA_ROWS)]
    both_blocks = lambda fn, *lists: [fn(*args) for args in zip(*lists)]

    def out_proj(rows):
        h1 = h_ref[rows, :]
        for p_ref, w_ref in zip(part_refs, w_refs):
            h1 = h1 + _dot(p_ref[rows, :], w_ref[...])
        return h1

    def scores(h1):
        xn = _rmsnorm(h1, gx_ref[...]).astype(BF16)
        return _dot(xn, qk_ref[0])

    def attend(s_all):
        probs = []
        for hd in range(XA_HEADS):
            s = s_all[:, hd * MEM_LEN:(hd + 1) * MEM_LEN]
            e = jnp.exp(s - jnp.max(s, axis=1, keepdims=True))
            probs.append((e / jnp.sum(e, axis=1, keepdims=True)).astype(BF16))
        return _dot(jnp.concatenate(probs, axis=1), vo_ref[0])

    def route(h2):
        xf = _rmsnorm(h2, gf_ref[...])
        x_hi = xf.astype(BF16)
        x_lo = (xf - x_hi.astype(F32)).astype(BF16)
        both = _dot(x_hi, wr_ref[...])
        logits = both[:, :LANES] + both[:, LANES:] + _dot(x_lo, wr_ref[:, :LANES]) + br_ref[...]
        return (x_hi,) + _router_gates(logits)

    h1s = both_blocks(out_proj, blocks)
    attended = both_blocks(attend, both_blocks(scores, h1s))
    h2s = both_blocks(lambda h1, o: h1 + o, h1s, attended)
    for rows, h2 in zip(blocks, h2s):
        h2_ref[rows, :] = h2
    routed_blocks = both_blocks(route, h2s)
    x_hi, gates, g_idx = (jnp.concatenate(parts, axis=0) for parts in zip(*routed_blocks))

    dest, units = _local_sort(g_idx)
    tm = dest.shape[0]
    dest_ref[...] = jnp.broadcast_to(dest, (tm, LANES))
    units_ref[0] = jnp.broadcast_to(units, (SUBLANES, LANES))
    dest_row = jnp.transpose(jnp.broadcast_to(dest, (tm, LANES)))[0:1, :]
    slot = lax.broadcasted_iota(jnp.int32, (LS_TILE, tm), 0).astype(F32)
    place = jnp.where(slot == dest_row, 1.0, 0.0).astype(BF16)
    routed = jnp.concatenate([x_hi, _pack_gates(gates)], axis=1)
    xn_ls_ref[...] = _dot(place, routed).astype(BF16)


def _out_xattn(h, parts, weights, gx, qk_all, vo_all, gf, wr, br, layer):
    tm = ROW_TILE
    row = lambda n: pl.BlockSpec((tm, n), lambda i: (i, 0))
    ls_row = lambda n: pl.BlockSpec((LS_TILE, n), lambda i: (i, 0))
    mem_spec = pl.BlockSpec((1, D_MODEL, D_MODEL), lambda i: (layer, 0, 0))
    in_specs = ([row(D_MODEL)] + [row(p.shape[1]) for p in parts]
                + [_full(w.shape) for w in weights]
                + [_full((1, D_MODEL)), mem_spec, mem_spec, _full((1, D_MODEL)),
                   _full((D_MODEL, 2 * LANES)), _full((1, LANES))])
    return pl.pallas_call(
        functools.partial(_out_xattn_kernel, len(parts)),
        grid=(N_ROW_TILES,),
        in_specs=in_specs,
        out_specs=[row(D_MODEL), ls_row(LS_WIDTH), row(LANES),
                   pl.BlockSpec((1, SUBLANES, LANES), lambda i: (i, 0, 0))],
        out_shape=[jax.ShapeDtypeStruct((SEQ, D_MODEL), F32),
                   jax.ShapeDtypeStruct((N_ROW_TILES * LS_TILE, LS_WIDTH), BF16),
                   jax.ShapeDtypeStruct((SEQ, LANES), F32),
                   jax.ShapeDtypeStruct((N_ROW_TILES, SUBLANES, LANES), jnp.int32)],
        compiler_params=_params("parallel"),
        name="out_xattn_router",
    )(h, *parts, *weights, gx, qk_all, vo_all, gf, wr, br)


def _route_index_kernel(units_ref, unit_of_ref, tile_group_ref):
    def no_unit(k, _):
        unit_of_ref[k] = -1
        return 0

    def no_group(j, _):
        tile_group_ref[j] = -1
        return 0

    pos = 0
    for g in range(N_GROUPS):
        first_tile = pos // SORT_UNITS

        def per_row_tile(i, pos):
            n = units_ref[i, g]
            base = i * LS_UNITS + units_ref[i, N_GROUPS + g]

            def per_unit(k, _):
                unit_of_ref[pos + k] = base + k
                return 0

            lax.fori_loop(0, n, per_unit, 0)
            return pos + n

        pos = lax.fori_loop(0, N_ROW_TILES, per_row_tile, pos)
        end_tile = (pos + SORT_UNITS - 1) // SORT_UNITS

        def mark_tile(j, _):
            tile_group_ref[j] = g
            return 0

        lax.fori_loop(first_tile, end_tile, mark_tile, 0)
        lax.fori_loop(pos, end_tile * SORT_UNITS, no_unit, 0)
        pos = end_tile * SORT_UNITS

    lax.fori_loop(pos, N_SORT_TILES * SORT_UNITS, no_unit, 0)
    lax.fori_loop(pos // SORT_UNITS, N_SORT_TILES, no_group, 0)


def _route_index(units):
    smem = lambda: pl.BlockSpec(memory_space=pltpu.SMEM)
    return pl.pallas_call(
        _route_index_kernel,
        in_specs=[smem()],
        out_specs=[smem(), smem()],
        out_shape=[jax.ShapeDtypeStruct((N_SORT_TILES * SORT_UNITS,), jnp.int32),
                   jax.ShapeDtypeStruct((N_SORT_TILES,), jnp.int32)],
        name="moe_route_index",
    )(units)


def _moe_kernel(unit_of_ref, tile_group_ref, xn_hbm, wg_ref, wu_ref, wd_ref, y_hbm,
                xbuf, ybuf, acc_ref, wg_bf, wu_bf, wd_bf, gather_sem, scatter_sem):
    j = pl.program_id(0)
    n_tiles = pl.num_programs(0)
    slot = j % 2

    def unit_rows(k):
        return pl.ds(pl.multiple_of(k * MOE_UNIT, MOE_UNIT), MOE_UNIT)

    def gather_copy(s, k, u):
        return pltpu.make_async_copy(xn_hbm.at[unit_rows(u), :], xbuf.at[s, unit_rows(k), :],
                                     gather_sem.at[s])

    def scatter_copy(s, k, u):
        return pltpu.make_async_copy(ybuf.at[s, unit_rows(k), :], y_hbm.at[unit_rows(u), :],
                                     scatter_sem.at[s])

    def is_full(tile):
        return unit_of_ref[tile * SORT_UNITS + SORT_UNITS - 1] >= 0

    def for_units(tile, full_fn, partial_fn):
        def run(fn, unroll):
            def unit(k, _):
                fn(k, unit_of_ref[tile * SORT_UNITS + k])
                return 0

            lax.fori_loop(0, SORT_UNITS, unit, 0, unroll=unroll)

        @pl.when(is_full(tile))
        def _():
            run(full_fn, 8)

        @pl.when(jnp.logical_not(is_full(tile)))
        def _():
            def guarded(k, u):
                @pl.when(u >= 0)
                def _():
                    full_fn(k, u)

                if partial_fn is not None:
                    @pl.when(u < 0)
                    def _():
                        partial_fn(k)

            run(guarded, 1)

    def start_gather(tile, s):
        def zero_fill(k):
            xbuf[s, unit_rows(k), :] = jnp.zeros((MOE_UNIT, LS_WIDTH), BF16)

        for_units(tile, lambda k, u: gather_copy(s, k, u).start(), zero_fill)

    def start_scatter(tile, s):
        for_units(tile, lambda k, u: scatter_copy(s, k, u).start(), None)

    def wait_units(tile, whole_copy, unit_copy):
        @pl.when(is_full(tile))
        def _():
            whole_copy.wait()

        @pl.when(jnp.logical_not(is_full(tile)))
        def _():
            def unit(k, _):
                u = unit_of_ref[tile * SORT_UNITS + k]

                @pl.when(u >= 0)
                def _():
                    unit_copy(k, u).wait()

                return 0

            lax.fori_loop(0, SORT_UNITS, unit, 0)

    def wait_gather(tile, s):
        whole = pltpu.make_async_copy(xn_hbm.at[pl.ds(0, ROW_TILE), :], xbuf.at[s], gather_sem.at[s])
        wait_units(tile, whole, lambda k, u: gather_copy(s, k, u))

    def wait_scatter(tile, s):
        whole = pltpu.make_async_copy(ybuf.at[s], y_hbm.at[pl.ds(0, ROW_TILE), :], scatter_sem.at[s])
        wait_units(tile, whole, lambda k, u: scatter_copy(s, k, u))

    @pl.when(j == 0)
    def _():
        start_gather(0, 0)

    @pl.when(j + 1 < n_tiles)
    def _():
        start_gather(j + 1, 1 - slot)

    grp = tile_group_ref[j]
    new_group = (j == 0) | (grp != tile_group_ref[jnp.maximum(j - 1, 0)])

    @pl.when((grp >= 0) & new_group)
    def _():
        wg_bf[...] = wg_ref[...].astype(BF16)
        wu_bf[...] = wu_ref[...].astype(BF16)
        wd_bf[...] = wd_ref[...].astype(BF16)

    wait_gather(j, slot)

    @pl.when(j >= 2)
    def _():
        wait_scatter(j - 2, slot)

    @pl.when(grp >= 0)
    def _():
        xn = xbuf[slot, :, :D_MODEL]
        packed_gates = xbuf[slot, :, D_MODEL:]
        for e in range(EXPERTS_PER_GROUP):
            gate = _unpack_gate(packed_gates, grp * EXPERTS_PER_GROUP + e)
            hg = _dot(xn, wg_bf[e])
            hu = _dot(xn, wu_bf[e])
            act = (hg * jax.nn.sigmoid(hg)) * hu * gate
            out = _dot(act.astype(BF16), wd_bf[e])
            if e == 0:
                acc_ref[...] = out
            else:
                acc_ref[...] += out
        y = acc_ref[...]
        y_hi = y.astype(BF16)
        ybuf[slot, :, :D_MODEL] = y_hi
        ybuf[slot, :, D_MODEL:] = (y - y_hi.astype(F32)).astype(BF16)

    start_scatter(j, slot)

    @pl.when(j == n_tiles - 1)
    def _():
        wait_scatter(j - 1, 1 - slot)
        wait_scatter(j, slot)


def _moe(unit_of, tile_group, xn_ls, wg, wu, wd, layer):
    group_of = lambda j, unit_of, tile_group: jnp.where(tile_group[j] < 0, N_GROUPS - 1,
                                                        tile_group[j])
    up_spec = pl.BlockSpec((None, EXPERTS_PER_GROUP, D_MODEL, D_EXPERT),
                           lambda j, u, t: (layer, group_of(j, u, t), 0, 0))
    down_spec = pl.BlockSpec((None, EXPERTS_PER_GROUP, D_EXPERT, D_MODEL),
                             lambda j, u, t: (layer, group_of(j, u, t), 0, 0))
    any_spec = pl.BlockSpec(memory_space=pl.ANY)
    return pl.pallas_call(
        _moe_kernel,
        grid_spec=pltpu.PrefetchScalarGridSpec(
            num_scalar_prefetch=2,
            grid=(N_SORT_TILES,),
            in_specs=[any_spec, up_spec, up_spec, down_spec],
            out_specs=any_spec,
            scratch_shapes=[pltpu.VMEM((2, ROW_TILE, LS_WIDTH), BF16),
                            pltpu.VMEM((2, ROW_TILE, 2 * D_MODEL), BF16),
                            pltpu.VMEM((ROW_TILE, D_MODEL), F32),
                            pltpu.VMEM((EXPERTS_PER_GROUP, D_MODEL, D_EXPERT), BF16),
                            pltpu.VMEM((EXPERTS_PER_GROUP, D_MODEL, D_EXPERT), BF16),
                            pltpu.VMEM((EXPERTS_PER_GROUP, D_EXPERT, D_MODEL), BF16),
                            pltpu.SemaphoreType.DMA((2,)),
                            pltpu.SemaphoreType.DMA((2,))]),
        out_shape=jax.ShapeDtypeStruct((N_ROW_TILES * LS_TILE, 2 * D_MODEL), BF16),
        compiler_params=_params("arbitrary"),
        name="moe_experts",
    )(unit_of, tile_group, xn_ls, wg, wu, wd)


def _add_unsorted(h_ref, y_ls_ref, dest_ref):
    tm = h_ref.shape[0]
    slot = lax.broadcasted_iota(jnp.int32, (tm, LS_TILE), 1).astype(F32)
    pick = jnp.where(slot == dest_ref[:, 0:1], 1.0, 0.0).astype(BF16)
    y = _dot(pick, y_ls_ref[...])
    return h_ref[...] + (y[:, :D_MODEL] + y[:, D_MODEL:])


def _unsort_final_kernel(h_ref, y_ls_ref, dest_ref, gfin_ref, o_ref):
    o_ref[...] = _rmsnorm(_add_unsorted(h_ref, y_ls_ref, dest_ref), gfin_ref[...])


def _unsort_specs():
    row = lambda n: pl.BlockSpec((ROW_TILE, n), lambda i: (i, 0))
    return [row(D_MODEL), pl.BlockSpec((LS_TILE, 2 * D_MODEL), lambda i: (i, 0)), row(LANES)]


def _unsort_final(h, y_ls, dest, gfin):
    return pl.pallas_call(
        _unsort_final_kernel,
        grid=(N_ROW_TILES,),
        in_specs=_unsort_specs() + [_full((1, D_MODEL))],
        out_specs=pl.BlockSpec((ROW_TILE, D_MODEL), lambda i: (i, 0)),
        out_shape=jax.ShapeDtypeStruct((SEQ, D_MODEL), F32),
        compiler_params=_params("parallel"),
        name="moe_unsort_final_norm",
    )(h, y_ls, dest, gfin)


def _l1_mixer_kernel(h_ref, y_ls_ref, dest_ref, g_ref, w_in_ref, vg_ref, vb_ref, ws_ref, bs_ref,
                     x_ref, o_ref):
    tm = h_ref.shape[0]
    x = _add_unsorted(h_ref, y_ls_ref, dest_ref)
    x_ref[...] = x
    xn = _rmsnorm(x, g_ref[...]).astype(BF16)
    p = _dot(xn, w_in_ref[...])
    p = 0.5 * p * (1.0 + lax.erf(p * (2.0 ** -0.5)))
    u = p[:, :D_MODEL]
    v = _layernorm(p[:, D_MODEL:], vg_ref[...], vb_ref[...]).astype(BF16)
    t_idx = lax.broadcasted_iota(jnp.int32, (GM_CHUNK, GM_CHUNK), 0)
    s_idx = lax.broadcasted_iota(jnp.int32, (GM_CHUNK, GM_CHUNK), 1)
    for g in range(GM_GROUPS):
        cols = slice(g * LANES, (g + 1) * LANES)
        wmix = jnp.where(t_idx >= s_idx, ws_ref[g], 0.0).astype(BF16)
        for c in range(tm // GM_CHUNK):
            rows = slice(c * GM_CHUNK, (c + 1) * GM_CHUNK)
            mixed = _dot(wmix, v[rows, cols]) + bs_ref[g]
            o_ref[rows, cols] = (u[rows, cols] * mixed).astype(BF16)


def _l1_mixer(h, y_ls, dest, g, w_in, vg, vb, ws, bs_b):
    row = pl.BlockSpec((ROW_TILE, D_MODEL), lambda i: (i, 0))
    return pl.pallas_call(
        _l1_mixer_kernel,
        grid=(N_ROW_TILES,),
        in_specs=_unsort_specs() + [
            _full((1, D_MODEL)), _full((D_MODEL, 2 * D_MODEL)), _full((1, D_MODEL)),
            _full((1, D_MODEL)), _full((GM_GROUPS, GM_CHUNK, GM_CHUNK)),
            _full((GM_GROUPS, GM_CHUNK, LANES))],
        out_specs=[row, row],
        out_shape=[jax.ShapeDtypeStruct((SEQ, D_MODEL), F32),
                   jax.ShapeDtypeStruct((SEQ, D_MODEL), BF16)],
        compiler_params=_params("parallel"),
        name="l1_mixer",
    )(h, y_ls, dest, g, w_in, vg, vb, ws, bs_b)


def _router_weights(we, wg, be, bg):
    pad = LANES - N_EXPERTS - N_GROUPS
    w = jnp.concatenate([we, wg, jnp.zeros((D_MODEL, pad), F32)], axis=1)
    b = jnp.concatenate([be, bg, jnp.zeros((pad,), F32)])[None, :]
    w_hi = w.astype(BF16)
    w_lo = (w - w_hi.astype(F32)).astype(BF16)
    return jnp.concatenate([w_hi, w_lo], axis=1), b


def kernel(x, mem, mem_norm_g, norm_mix, norm_xa, norm_ffn, final_norm_g, ab_w_in, ab_conv_w, ab_conv_b, ab_cnorm_g, ab_cnorm_b, ab_w_out, c_w_in, c_vnorm_g, c_vnorm_b, c_ws, c_bs, c_w_out, xa_wq, xa_wk, xa_wv, xa_wo, rt_wg, rt_bg, rt_we, rt_be, ex_w_gate, ex_w_up, ex_w_down):
    assert x.shape == (1, SEQ, D_MODEL) and mem.shape == (1, MEM_LEN, D_MODEL)
    bf = lambda a: a.astype(BF16)
    r1 = lambda a: a.reshape(1, -1)
    h = x[0]

    qk_all, vo_all = _mem_kv(mem[0], r1(mem_norm_g), bf(xa_wk), bf(xa_wv), bf(xa_wq), bf(xa_wo))

    def tail(h, parts, weights, i):
        wr, br = _router_weights(rt_we[i], rt_wg[i], rt_be[i], rt_bg[i])
        h2, xn_ls, dest, units = _out_xattn(
            h, parts, weights, r1(norm_xa[i]), qk_all, vo_all, r1(norm_ffn[i]), wr, br, i)
        unit_of, tile_group = _route_index(units[:, 0, :])
        y_ls = _moe(unit_of, tile_group, xn_ls, ex_w_gate, ex_w_up, ex_w_down, i)
        return h2, y_ls, dest

    w_in = ab_w_in[0]
    q_lo, k_lo, v_lo = 2 * CONV_CH, 2 * CONV_CH + SB_DIM, 2 * CONV_CH + 2 * SB_DIM
    hglu, q, kt, v = _l0_inproj(h, r1(norm_mix[0]), bf(w_in[:, :q_lo]), bf(w_in[:, q_lo:k_lo]),
                                bf(w_in[:, k_lo:v_lo].T), bf(w_in[:, v_lo:]))
    conv_out = _conv(hglu, ab_conv_w[0], r1(ab_conv_b[0]), r1(ab_cnorm_g[0]), r1(ab_cnorm_b[0]))
    sb_out = _sb_attention(q, kt, v)
    w_out = bf(ab_w_out[0])
    h2, y_ls, dest = tail(h, [conv_out, sb_out], [w_out[:CONV_CH], w_out[CONV_CH:]], 0)

    bs_b = jnp.broadcast_to(c_bs[0][:, :, None], (GM_GROUPS, GM_CHUNK, LANES))
    h, gated = _l1_mixer(h2, y_ls, dest, r1(norm_mix[1]), bf(c_w_in[0]), r1(c_vnorm_g[0]),
                         r1(c_vnorm_b[0]), c_ws[0], bs_b)
    h2, y_ls, dest = tail(h, [gated], [bf(c_w_out[0])], 1)
    return _unsort_final(h2, y_ls, dest, r1(final_norm_g))[None]
```

```python
import functools

import jax
import jax.numpy as jnp
from jax import lax
from jax.experimental import pallas as pl
from jax.experimental.pallas import tpu as pltpu

D_MODEL = 1024
SEQ = 16384
MEM_LEN = 256
EPS = 1e-6
CONV_CH = 512
CONV_WIDTH = 31
SB_HEADS = 8
SB_HEAD_DIM = 64
SB_DIM = 512
GM_GROUPS = 8
GM_CHUNK = 128
XA_HEADS = 4
XA_HEAD_DIM = 256
N_GROUPS = 4
EXPERTS_PER_GROUP = 4
N_EXPERTS = 16
D_EXPERT = 256

LANES = 128
SUBLANES = 8
ROW_TILE = 512
SB_TQ = 256
SB_TK = 256
SB_LOG_UNDERFLOW = -104.0
SB_HIDDEN_SCORE = -1e30
CONV_HALO = 32
CONV_ROWS = 64
MOE_UNIT = 16
N_ROW_TILES = SEQ // ROW_TILE
LS_TILE = ROW_TILE + N_GROUPS * MOE_UNIT
LS_UNITS = LS_TILE // MOE_UNIT
SORT_UNITS = ROW_TILE // MOE_UNIT
N_SORT_TILES = N_ROW_TILES * LS_UNITS // SORT_UNITS + N_GROUPS
GATE_TERM_STRIDE = 32
LS_WIDTH = D_MODEL + LANES
VMEM_LIMIT = 56 * 1024 * 1024

BF16 = jnp.bfloat16
F32 = jnp.float32


def _params(*semantics):
    return pltpu.CompilerParams(dimension_semantics=semantics, vmem_limit_bytes=VMEM_LIMIT)


def _dot(a, b):
    return jnp.dot(a, b, preferred_element_type=F32)


def _rmsnorm(x, g):
    return x * lax.rsqrt(jnp.mean(x * x, axis=-1, keepdims=True) + EPS) * g


def _layernorm(x, g, b):
    mu = jnp.mean(x, axis=-1, keepdims=True)
    xc = x - mu
    var = jnp.mean(xc * xc, axis=-1, keepdims=True)
    return xc * lax.rsqrt(var + EPS) * g + b


def _full(shape):
    return pl.BlockSpec(shape, lambda *_: (0,) * len(shape))


def _mem_kv_kernel(mem_ref, g_ref, wk_ref, wv_ref, wq_ref, wo_ref, qk_ref, vo_ref):
    memn = _rmsnorm(mem_ref[...], g_ref[...]).astype(BF16)
    k = _dot(memn, wk_ref[0]).astype(BF16)
    v = _dot(memn, wv_ref[0]).astype(BF16)
    for hd in range(XA_HEADS):
        sl = slice(hd * XA_HEAD_DIM, (hd + 1) * XA_HEAD_DIM)
        qk = lax.dot_general(wq_ref[0, :, sl], k[:, sl], (((1,), (1,)), ((), ())),
                             preferred_element_type=F32)
        qk_ref[0, :, hd * MEM_LEN:(hd + 1) * MEM_LEN] = (qk * (XA_HEAD_DIM ** -0.5)).astype(BF16)
        vo_ref[0, hd * MEM_LEN:(hd + 1) * MEM_LEN, :] = _dot(v[:, sl], wo_ref[0, sl, :]).astype(BF16)


def _mem_kv(mem, g, wk, wv, wq, wo):
    depth = wk.shape[0]
    wspec = pl.BlockSpec((1, D_MODEL, D_MODEL), lambda i: (i, 0, 0))
    qk_spec = pl.BlockSpec((1, D_MODEL, XA_HEADS * MEM_LEN), lambda i: (i, 0, 0))
    vo_spec = pl.BlockSpec((1, XA_HEADS * MEM_LEN, D_MODEL), lambda i: (i, 0, 0))
    return pl.pallas_call(
        _mem_kv_kernel,
        grid=(depth,),
        in_specs=[_full((MEM_LEN, D_MODEL)), _full((1, D_MODEL)), wspec, wspec, wspec, wspec],
        out_specs=[qk_spec, vo_spec],
        out_shape=[jax.ShapeDtypeStruct((depth, D_MODEL, XA_HEADS * MEM_LEN), BF16),
                   jax.ShapeDtypeStruct((depth, XA_HEADS * MEM_LEN, D_MODEL), BF16)],
        compiler_params=_params("arbitrary"),
        name="mem_kv",
    )(mem, g, wk, wv, wq, wo)


def _l0_inproj_kernel(x_ref, g_ref, w_ag_ref, w_q_ref, w_kt_ref, w_v_ref,
                      cw_ref, cb_ref, lg_ref, lb_ref,
                      conv_ref, q_ref, kt_ref, v_ref, ext_ref, y_ref):
    tm = x_ref.shape[0]

    @pl.when(pl.program_id(0) == 0)
    def _():
        ext_ref[0:CONV_HALO, :] = jnp.zeros((CONV_HALO, CONV_CH), F32)

    xn = _rmsnorm(x_ref[...], g_ref[...]).astype(BF16)
    ag = _dot(xn, w_ag_ref[...])
    ext_ref[CONV_HALO:, :] = ag[:, :CONV_CH] * jax.nn.sigmoid(ag[:, CONV_CH:])
    q_ref[...] = (_dot(xn, w_q_ref[...]) * (SB_HEAD_DIM ** -0.5)).astype(BF16)
    kt_ref[...] = lax.dot_general(w_kt_ref[...], xn, (((1,), (1,)), ((), ())),
                                  preferred_element_type=F32).astype(BF16)
    v_ref[...] = _dot(xn, w_v_ref[...]).astype(BF16)

    first_tap = CONV_HALO - (CONV_WIDTH - 1)
    for r0 in range(0, tm, CONV_ROWS):
        for j in range(CONV_CH // LANES):
            lanes = slice(j * LANES, (j + 1) * LANES)
            win = ext_ref[r0:r0 + CONV_ROWS + CONV_HALO, lanes]
            acc = jnp.zeros((CONV_ROWS, LANES), F32) + cb_ref[:, lanes]
            for sub in range(SUBLANES):
                offsets = [o for o in range(first_tap, first_tap + CONV_WIDTH)
                           if o % SUBLANES == sub]
                shifted = pltpu.roll(win, win.shape[0] - sub, axis=0) if sub else win
                for o in offsets:
                    tap = cw_ref[o - first_tap:o - first_tap + 1, lanes]
                    acc = acc + shifted[o - sub:o - sub + CONV_ROWS, :] * tap
            y_ref[r0:r0 + CONV_ROWS, lanes] = acc
    y = _layernorm(y_ref[...], lg_ref[...], lb_ref[...])
    conv_ref[...] = (y * jax.nn.sigmoid(y)).astype(BF16)
    ext_ref[0:CONV_HALO, :] = ext_ref[tm:tm + CONV_HALO, :]


def _l0_inproj(x, g, w_ag, w_q, w_kt, w_v, cw, cb, lg, lb):
    tm = ROW_TILE
    row = lambda n: pl.BlockSpec((tm, n), lambda i: (i, 0))
    return pl.pallas_call(
        _l0_inproj_kernel,
        grid=(N_ROW_TILES,),
        in_specs=[row(D_MODEL), _full((1, D_MODEL)), _full((D_MODEL, 2 * CONV_CH)),
                  _full((D_MODEL, SB_DIM)), _full((SB_DIM, D_MODEL)), _full((D_MODEL, SB_DIM)),
                  _full((CONV_WIDTH, CONV_CH)), _full((1, CONV_CH)),
                  _full((1, CONV_CH)), _full((1, CONV_CH))],
        out_specs=[row(CONV_CH), row(SB_DIM), pl.BlockSpec((SB_DIM, tm), lambda i: (0, i)),
                   row(SB_DIM)],
        out_shape=[jax.ShapeDtypeStruct((SEQ, CONV_CH), BF16),
                   jax.ShapeDtypeStruct((SEQ, SB_DIM), BF16),
                   jax.ShapeDtypeStruct((SB_DIM, SEQ), BF16),
                   jax.ShapeDtypeStruct((SEQ, SB_DIM), BF16)],
        scratch_shapes=[pltpu.VMEM((tm + CONV_HALO, CONV_CH), F32),
                        pltpu.VMEM((tm, CONV_CH), F32)],
        compiler_params=_params("arbitrary"),
        name="l0_inproj_conv",
    )(x, g, w_ag, w_q, w_kt, w_v, cw, cb, lg, lb)


def _sb_kernel(q_ref, kt_ref, v_ref, o_ref, acc_ref, carry_ref):
    qi = pl.program_id(1)
    tq, tk = SB_TQ, SB_TK
    lane = lax.broadcasted_iota(jnp.int32, (tq, LANES), 1)
    q2 = q_ref[...]
    zero = jnp.zeros_like(q2)
    qh = (jnp.where(lane < SB_HEAD_DIM, q2, zero), jnp.where(lane >= SB_HEAD_DIM, q2, zero))
    jj = lax.broadcasted_iota(jnp.int32, (tk, tk), 0)
    ss = lax.broadcasted_iota(jnp.int32, (tk, tk), 1)
    minus_later = jnp.where(jj > ss, -1.0, 0.0).astype(BF16)
    minus_later2 = jnp.concatenate([minus_later, minus_later], axis=0)

    def scores(h, kb, valid):
        z = _dot(qh[h], kt_ref[:, pl.ds(pl.multiple_of(kb * tk, tk), tk)])
        return z if valid is None else jnp.where(valid, z, SB_HIDDEN_SCORE)

    def stay(z):
        sp = jnp.maximum(z, 0.0) + jnp.log(1.0 + jnp.exp(-jnp.abs(z)))
        hi = sp.astype(BF16)
        return sp, hi, (sp - hi.astype(F32)).astype(BF16)

    def later_sum(hi, lo):
        return _dot(jnp.concatenate([hi, lo], axis=1), minus_later2)

    def weighted_values(z, sp, after, kb):
        w = jnp.exp((z - sp) + after)
        return _dot(w.astype(BF16), v_ref[pl.ds(pl.multiple_of(kb * tk, tk), tk), :])

    def block(h, kb, valid):
        z = scores(h, kb, valid)
        sp, hi, lo = stay(z)
        pv = weighted_values(z, sp, later_sum(hi, lo), kb)
        return pv, jnp.sum(sp, axis=1, keepdims=True)

    row = lax.broadcasted_iota(jnp.int32, (tq, tk), 0)
    col = lax.broadcasted_iota(jnp.int32, (tq, tk), 1)
    has_prev = qi > 0
    chains = [(h, kb, valid) for h in range(2)
              for kb, valid in ((qi, col < row), (jnp.maximum(qi - 1, 0), None))]
    zs = [scores(h, kb, valid) for h, kb, valid in chains]
    stays = [stay(z) for z in zs]
    afters = [later_sum(hi, lo) for _, hi, lo in stays]
    pvs = [weighted_values(z, sp, after, kb)
           for z, (sp, _, _), after, (_, kb, _) in zip(zs, stays, afters, chains)]
    sums = [jnp.sum(sp, axis=1, keepdims=True) for sp, _, _ in stays]
    for h in range(2):
        (pv_diag, pv_prev), (sp_diag, sp_prev) = pvs[2 * h:2 * h + 2], sums[2 * h:2 * h + 2]
        acc_ref[h] = pv_diag + jnp.where(has_prev, jnp.exp(-sp_diag), 0.0) * pv_prev
        carry_ref[h] = jnp.broadcast_to(-(sp_diag + sp_prev), (tq, LANES))

    def alive():
        return jnp.max(carry_ref[...]) > SB_LOG_UNDERFLOW

    def cond(state):
        kb, go = state
        return (kb >= 0) & go

    def body(state):
        kb, _ = state
        for h in range(2):
            pv, sp_sum = block(h, kb, None)
            carry = carry_ref[h]
            acc_ref[h] += jnp.exp(carry) * pv
            carry_ref[h] = carry - sp_sum
        return kb - 1, alive()

    lax.while_loop(cond, body, (qi - 2, alive()))
    o_ref[...] = jnp.where(lane < SB_HEAD_DIM, acc_ref[0], acc_ref[1]).astype(BF16)


def _sb_attention(q, kt, v):
    tq = SB_TQ
    return pl.pallas_call(
        _sb_kernel,
        grid=(SB_DIM // LANES, SEQ // tq),
        in_specs=[pl.BlockSpec((tq, LANES), lambda p, i: (i, p)),
                  pl.BlockSpec((LANES, SEQ), lambda p, i: (p, 0)),
                  pl.BlockSpec((SEQ, LANES), lambda p, i: (0, p))],
        out_specs=pl.BlockSpec((tq, LANES), lambda p, i: (i, p)),
        out_shape=jax.ShapeDtypeStruct((SEQ, SB_DIM), BF16),
        scratch_shapes=[pltpu.VMEM((2, tq, LANES), F32), pltpu.VMEM((2, tq, LANES), F32)],
        compiler_params=_params("parallel", "parallel"),
        name="sb_attention",
    )(q, kt, v)


def _router_gates(logits):
    col = lax.broadcasted_iota(jnp.int32, logits.shape, 1)
    colf = col.astype(F32)
    ninf = -jnp.inf
    first = lambda hit: jnp.min(jnp.where(hit, colf, float(LANES)), axis=1, keepdims=True)

    is_group = (col >= N_EXPERTS) & (col < N_EXPERTS + N_GROUPS)
    lg = jnp.where(is_group, logits, ninf)
    gmax = jnp.max(lg, axis=1, keepdims=True)
    pg_top = 1.0 / jnp.sum(jnp.exp(lg - gmax), axis=1, keepdims=True)
    g_idx = first(lg == gmax) - float(N_EXPERTS)

    in_group = (col < N_EXPERTS) & ((col // EXPERTS_PER_GROUP).astype(F32) == g_idx)
    le = jnp.where(in_group, logits, ninf)
    m1 = jnp.max(le, axis=1, keepdims=True)
    i1 = first(le == m1)
    le2 = jnp.where(colf == i1, ninf, le)
    m2 = jnp.max(le2, axis=1, keepdims=True)
    i2 = first(le2 == m2)
    ee = jnp.exp(le - m1)
    pe = ee / jnp.sum(ee, axis=1, keepdims=True)
    p1 = jnp.sum(jnp.where(colf == i1, pe, 0.0), axis=1, keepdims=True)
    p2 = jnp.sum(jnp.where(colf == i2, pe, 0.0), axis=1, keepdims=True)
    den = p1 + p2
    gates = jnp.where(colf == i1, p1 / den * pg_top,
                      jnp.where(colf == i2, p2 / den * pg_top, 0.0))
    return gates, g_idx


def _local_sort(g_idx):
    tm = g_idx.shape[0]
    lane = lax.broadcasted_iota(jnp.int32, (tm, LANES), 1)
    member = jnp.where(lane.astype(F32) == g_idx, 1.0, 0.0)
    r_i = lax.broadcasted_iota(jnp.int32, (tm, tm), 0)
    c_i = lax.broadcasted_iota(jnp.int32, (tm, tm), 1)
    before = jnp.where(c_i < r_i, 1.0, 0.0).astype(BF16)
    rank = jnp.sum(member * _dot(before, member.astype(BF16)), axis=1, keepdims=True)
    count = jnp.sum(member, axis=0, keepdims=True)
    padded = jnp.floor((count + (MOE_UNIT - 1)) * (1.0 / MOE_UNIT)) * MOE_UNIT
    lane1 = lax.broadcasted_iota(jnp.int32, (1, LANES), 1)
    offset = jnp.zeros((1, LANES), F32)
    start = jnp.zeros((1, 1), F32)
    for g in range(N_GROUPS):
        offset = offset + jnp.where((lane1 == g) | (lane1 == N_GROUPS + g), start, 0.0)
        start = start + jnp.sum(jnp.where(lane1 == g, padded, 0.0), axis=1, keepdims=True)
    dest = jnp.sum(member * offset, axis=1, keepdims=True) + rank
    rows = jnp.where(lane1 == N_GROUPS - 1, LS_TILE - offset, padded)
    units = jnp.where(lane1 < N_GROUPS, rows, offset) * (1.0 / MOE_UNIT)
    return dest, units.astype(jnp.int32)


def _pack_gates(gates):
    hi = gates.astype(BF16).astype(F32)
    rest = gates - hi
    mid = rest.astype(BF16).astype(F32)
    lo = rest - mid
    packed = hi + pltpu.roll(mid, GATE_TERM_STRIDE, axis=1) + pltpu.roll(lo, 2 * GATE_TERM_STRIDE, axis=1)
    return packed.astype(BF16)


def _unpack_gate(packed, expert):
    lane = lax.broadcasted_iota(jnp.int32, packed.shape, 1)
    terms = jnp.where(lane % GATE_TERM_STRIDE == expert, packed.astype(F32), 0.0)
    return jnp.sum(terms, axis=1, keepdims=True)


def _out_xattn_kernel(n_parts, *refs):
    h_ref = refs[0]
    part_refs = refs[1:1 + n_parts]
    w_refs = refs[1 + n_parts:1 + 2 * n_parts]
    (gx_ref, qk_ref, vo_ref, gf_ref, wr_ref, br_ref,
     h2_ref, xn_ls_ref, dest_ref, units_ref) = refs[1 + 2 * n_parts:]

    h1 = h_ref[...]
    for p_ref, w_ref in zip(part_refs, w_refs):
        h1 = h1 + _dot(p_ref[...], w_ref[...])

    xn = _rmsnorm(h1, gx_ref[...]).astype(BF16)
    s_all = _dot(xn, qk_ref[0])
    probs = []
    for hd in range(XA_HEADS):
        s = s_all[:, hd * MEM_LEN:(hd + 1) * MEM_LEN]
        e = jnp.exp(s - jnp.max(s, axis=1, keepdims=True))
        probs.append((e / jnp.sum(e, axis=1, keepdims=True)).astype(BF16))
    h2 = h1 + _dot(jnp.concatenate(probs, axis=1), vo_ref[0])
    h2_ref[...] = h2

    xf = _rmsnorm(h2, gf_ref[...])
    x_hi = xf.astype(BF16)
    x_lo = (xf - x_hi.astype(F32)).astype(BF16)
    both = _dot(x_hi, wr_ref[...])
    logits = both[:, :LANES] + both[:, LANES:] + _dot(x_lo, wr_ref[:, :LANES]) + br_ref[...]

    gates, g_idx = _router_gates(logits)
    dest, units = _local_sort(g_idx)
    tm = dest.shape[0]
    dest_ref[...] = jnp.broadcast_to(dest, (tm, LANES))
    units_ref[0] = jnp.broadcast_to(units, (SUBLANES, LANES))
    dest_row = jnp.transpose(jnp.broadcast_to(dest, (tm, LANES)))[0:1, :]
    slot = lax.broadcasted_iota(jnp.int32, (LS_TILE, tm), 0).astype(F32)
    place = jnp.where(slot == dest_row, 1.0, 0.0).astype(BF16)
    routed = jnp.concatenate([x_hi, _pack_gates(gates)], axis=1)
    xn_ls_ref[...] = _dot(place, routed).astype(BF16)


def _out_xattn(h, parts, weights, gx, qk_all, vo_all, gf, wr, br, layer):
    tm = ROW_TILE
    row = lambda n, rows=tm: pl.BlockSpec((rows, n), lambda i: (i, 0))
    qk_spec = pl.BlockSpec((1, D_MODEL, XA_HEADS * MEM_LEN), lambda i: (layer, 0, 0))
    vo_spec = pl.BlockSpec((1, XA_HEADS * MEM_LEN, D_MODEL), lambda i: (layer, 0, 0))
    in_specs = ([row(D_MODEL)] + [row(p.shape[1]) for p in parts]
                + [_full(w.shape) for w in weights]
                + [_full((1, D_MODEL)), qk_spec, vo_spec, _full((1, D_MODEL)),
                   _full((D_MODEL, 2 * LANES)), _full((1, LANES))])
    return pl.pallas_call(
        functools.partial(_out_xattn_kernel, len(parts)),
        grid=(N_ROW_TILES,),
        in_specs=in_specs,
        out_specs=[row(D_MODEL), row(LS_WIDTH, LS_TILE), row(LANES),
                   pl.BlockSpec((1, SUBLANES, LANES), lambda i: (i, 0, 0))],
        out_shape=[jax.ShapeDtypeStruct((SEQ, D_MODEL), F32),
                   jax.ShapeDtypeStruct((N_ROW_TILES * LS_TILE, LS_WIDTH), BF16),
                   jax.ShapeDtypeStruct((SEQ, LANES), F32),
                   jax.ShapeDtypeStruct((N_ROW_TILES, SUBLANES, LANES), jnp.int32)],
        compiler_params=_params("parallel"),
        name="out_xattn_router",
    )(h, *parts, *weights, gx, qk_all, vo_all, gf, wr, br)


def _route_index_kernel(units_ref, unit_of_ref, tile_group_ref):
    def no_unit(k, _):
        unit_of_ref[k] = -1
        return 0

    def no_group(j, _):
        tile_group_ref[j] = -1
        return 0

    pos = 0
    for g in range(N_GROUPS):
        first_tile = pos // SORT_UNITS

        def per_row_tile(i, pos):
            n = units_ref[i, g]
            base = i * LS_UNITS + units_ref[i, N_GROUPS + g]

            def per_unit(k, _):
                unit_of_ref[pos + k] = base + k
                return 0

            lax.fori_loop(0, n, per_unit, 0)
            return pos + n

        pos = lax.fori_loop(0, N_ROW_TILES, per_row_tile, pos)
        end_tile = (pos + SORT_UNITS - 1) // SORT_UNITS

        def mark_tile(j, _):
            tile_group_ref[j] = g
            return 0

        lax.fori_loop(first_tile, end_tile, mark_tile, 0)
        lax.fori_loop(pos, end_tile * SORT_UNITS, no_unit, 0)
        pos = end_tile * SORT_UNITS

    lax.fori_loop(pos, N_SORT_TILES * SORT_UNITS, no_unit, 0)
    lax.fori_loop(pos // SORT_UNITS, N_SORT_TILES, no_group, 0)


def _route_index(units):
    smem = lambda: pl.BlockSpec(memory_space=pltpu.SMEM)
    return pl.pallas_call(
        _route_index_kernel,
        in_specs=[smem()],
        out_specs=[smem(), smem()],
        out_shape=[jax.ShapeDtypeStruct((N_SORT_TILES * SORT_UNITS,), jnp.int32),
                   jax.ShapeDtypeStruct((N_SORT_TILES,), jnp.int32)],
        name="moe_route_index",
    )(units)


def _moe_kernel(unit_of_ref, tile_group_ref, xn_hbm, wg_ref, wu_ref, wd_ref, y_hbm,
                xbuf, ybuf, acc_ref, wg_bf, wu_bf, wd_bf, gather_sem, scatter_sem):
    j = pl.program_id(0)
    n_tiles = pl.num_programs(0)
    slot = j % 2

    def unit_rows(k):
        return pl.ds(pl.multiple_of(k * MOE_UNIT, MOE_UNIT), MOE_UNIT)

    def gather_copy(s, k, u):
        return pltpu.make_async_copy(xn_hbm.at[unit_rows(u), :], xbuf.at[s, unit_rows(k), :],
                                     gather_sem.at[s])

    def scatter_copy(s, k, u):
        return pltpu.make_async_copy(ybuf.at[s, unit_rows(k), :], y_hbm.at[unit_rows(u), :],
                                     scatter_sem.at[s])

    def is_full(tile):
        return unit_of_ref[tile * SORT_UNITS + SORT_UNITS - 1] >= 0

    def for_units(tile, full_fn, partial_fn):
        def run(fn, unroll):
            def unit(k, _):
                fn(k, unit_of_ref[tile * SORT_UNITS + k])
                return 0

            lax.fori_loop(0, SORT_UNITS, unit, 0, unroll=unroll)

        @pl.when(is_full(tile))
        def _():
            run(full_fn, 8)

        @pl.when(jnp.logical_not(is_full(tile)))
        def _():
            def guarded(k, u):
                @pl.when(u >= 0)
                def _():
                    full_fn(k, u)

                if partial_fn is not None:
                    @pl.when(u < 0)
                    def _():
                        partial_fn(k)

            run(guarded, 1)

    def start_gather(tile, s):
        def zero_fill(k):
            xbuf[s, unit_rows(k), :] = jnp.zeros((MOE_UNIT, LS_WIDTH), BF16)

        for_units(tile, lambda k, u: gather_copy(s, k, u).start(), zero_fill)

    def start_scatter(tile, s):
        for_units(tile, lambda k, u: scatter_copy(s, k, u).start(), None)

    def wait_units(tile, whole_copy, unit_copy):
        @pl.when(is_full(tile))
        def _():
            whole_copy.wait()

        @pl.when(jnp.logical_not(is_full(tile)))
        def _():
            def unit(k, _):
                u = unit_of_ref[tile * SORT_UNITS + k]

                @pl.when(u >= 0)
                def _():
                    unit_copy(k, u).wait()

                return 0

            lax.fori_loop(0, SORT_UNITS, unit, 0)

    def wait_gather(tile, s):
        whole = pltpu.make_async_copy(xn_hbm.at[pl.ds(0, ROW_TILE), :], xbuf.at[s], gather_sem.at[s])
        wait_units(tile, whole, lambda k, u: gather_copy(s, k, u))

    def wait_scatter(tile, s):
        whole = pltpu.make_async_copy(ybuf.at[s], y_hbm.at[pl.ds(0, ROW_TILE), :], scatter_sem.at[s])
        wait_units(tile, whole, lambda k, u: scatter_copy(s, k, u))

    @pl.when(j == 0)
    def _():
        start_gather(0, 0)

    @pl.when(j + 1 < n_tiles)
    def _():
        start_gather(j + 1, 1 - slot)

    grp = tile_group_ref[j]
    new_group = (j == 0) | (grp != tile_group_ref[jnp.maximum(j - 1, 0)])

    @pl.when((grp >= 0) & new_group)
    def _():
        wg_bf[...] = wg_ref[...].astype(BF16)
        wu_bf[...] = wu_ref[...].astype(BF16)
        wd_bf[...] = wd_ref[...].astype(BF16)

    wait_gather(j, slot)

    @pl.when(j >= 2)
    def _():
        wait_scatter(j - 2, slot)

    @pl.when(grp >= 0)
    def _():
        xn = xbuf[slot, :, :D_MODEL]
        packed_gates = xbuf[slot, :, D_MODEL:]
        for e in range(EXPERTS_PER_GROUP):
            gate = _unpack_gate(packed_gates, grp * EXPERTS_PER_GROUP + e)
            hg = _dot(xn, wg_bf[e])
            hu = _dot(xn, wu_bf[e])
            act = (hg * jax.nn.sigmoid(hg)) * hu * gate
            out = _dot(act.astype(BF16), wd_bf[e])
            if e == 0:
                acc_ref[...] = out
            else:
                acc_ref[...] += out
        y = acc_ref[...]
        y_hi = y.astype(BF16)
        ybuf[slot, :, :D_MODEL] = y_hi
        ybuf[slot, :, D_MODEL:] = (y - y_hi.astype(F32)).astype(BF16)

    start_scatter(j, slot)

    @pl.when(j == n_tiles - 1)
    def _():
        wait_scatter(j - 1, 1 - slot)
        wait_scatter(j, slot)


def _moe(unit_of, tile_group, xn_ls, wg, wu, wd, layer):
    group_of = lambda j, unit_of, tile_group: jnp.where(tile_group[j] < 0, N_GROUPS - 1,
                                                        tile_group[j])
    up_spec = pl.BlockSpec((None, EXPERTS_PER_GROUP, D_MODEL, D_EXPERT),
                           lambda j, u, t: (layer, group_of(j, u, t), 0, 0))
    down_spec = pl.BlockSpec((None, EXPERTS_PER_GROUP, D_EXPERT, D_MODEL),
                             lambda j, u, t: (layer, group_of(j, u, t), 0, 0))
    any_spec = pl.BlockSpec(memory_space=pl.ANY)
    return pl.pallas_call(
        _moe_kernel,
        grid_spec=pltpu.PrefetchScalarGridSpec(
            num_scalar_prefetch=2,
            grid=(N_SORT_TILES,),
            in_specs=[any_spec, up_spec, up_spec, down_spec],
            out_specs=any_spec,
            scratch_shapes=[pltpu.VMEM((2, ROW_TILE, LS_WIDTH), BF16),
                            pltpu.VMEM((2, ROW_TILE, 2 * D_MODEL), BF16),
                            pltpu.VMEM((ROW_TILE, D_MODEL), F32),
                            pltpu.VMEM((EXPERTS_PER_GROUP, D_MODEL, D_EXPERT), BF16),
                            pltpu.VMEM((EXPERTS_PER_GROUP, D_MODEL, D_EXPERT), BF16),
                            pltpu.VMEM((EXPERTS_PER_GROUP, D_EXPERT, D_MODEL), BF16),
                            pltpu.SemaphoreType.DMA((2,)),
                            pltpu.SemaphoreType.DMA((2,))]),
        out_shape=jax.ShapeDtypeStruct((N_ROW_TILES * LS_TILE, 2 * D_MODEL), BF16),
        compiler_params=_params("arbitrary"),
        name="moe_experts",
    )(unit_of, tile_group, xn_ls, wg, wu, wd)


def _add_unsorted(h_ref, y_ls_ref, dest_ref):
    tm = h_ref.shape[0]
    slot = lax.broadcasted_iota(jnp.int32, (tm, LS_TILE), 1).astype(F32)
    pick = jnp.where(slot == dest_ref[:, 0:1], 1.0, 0.0).astype(BF16)
    y = _dot(pick, y_ls_ref[...])
    return h_ref[...] + (y[:, :D_MODEL] + y[:, D_MODEL:])


def _unsort_final_kernel(h_ref, y_ls_ref, dest_ref, gfin_ref, o_ref):
    o_ref[...] = _rmsnorm(_add_unsorted(h_ref, y_ls_ref, dest_ref), gfin_ref[...])


def _unsort_specs():
    row = lambda n: pl.BlockSpec((ROW_TILE, n), lambda i: (i, 0))
    return [row(D_MODEL), pl.BlockSpec((LS_TILE, 2 * D_MODEL), lambda i: (i, 0)), row(LANES)]


def _unsort_final(h, y_ls, dest, gfin):
    return pl.pallas_call(
        _unsort_final_kernel,
        grid=(N_ROW_TILES,),
        in_specs=_unsort_specs() + [_full((1, D_MODEL))],
        out_specs=pl.BlockSpec((ROW_TILE, D_MODEL), lambda i: (i, 0)),
        out_shape=jax.ShapeDtypeStruct((SEQ, D_MODEL), F32),
        compiler_params=_params("parallel"),
        name="moe_unsort_final_norm",
    )(h, y_ls, dest, gfin)


def _l1_mixer_kernel(h_ref, y_ls_ref, dest_ref, g_ref, w_in_ref, vg_ref, vb_ref, ws_ref, bs_ref,
                     x_ref, o_ref):
    tm = h_ref.shape[0]
    x = _add_unsorted(h_ref, y_ls_ref, dest_ref)
    x_ref[...] = x
    xn = _rmsnorm(x, g_ref[...]).astype(BF16)
    p = _dot(xn, w_in_ref[...])
    p = 0.5 * p * (1.0 + lax.erf(p * (2.0 ** -0.5)))
    u = p[:, :D_MODEL]
    v = _layernorm(p[:, D_MODEL:], vg_ref[...], vb_ref[...]).astype(BF16)
    t_idx = lax.broadcasted_iota(jnp.int32, (GM_CHUNK, GM_CHUNK), 0)
    s_idx = lax.broadcasted_iota(jnp.int32, (GM_CHUNK, GM_CHUNK), 1)
    for g in range(GM_GROUPS):
        cols = slice(g * LANES, (g + 1) * LANES)
        wmix = jnp.where(t_idx >= s_idx, ws_ref[g], 0.0).astype(BF16)
        for c in range(tm // GM_CHUNK):
            rows = slice(c * GM_CHUNK, (c + 1) * GM_CHUNK)
            mixed = _dot(wmix, v[rows, cols]) + bs_ref[g]
            o_ref[rows, cols] = (u[rows, cols] * mixed).astype(BF16)


def _l1_mixer(h, y_ls, dest, g, w_in, vg, vb, ws, bs_b):
    row = pl.BlockSpec((ROW_TILE, D_MODEL), lambda i: (i, 0))
    return pl.pallas_call(
        _l1_mixer_kernel,
        grid=(N_ROW_TILES,),
        in_specs=_unsort_specs() + [
            _full((1, D_MODEL)), _full((D_MODEL, 2 * D_MODEL)), _full((1, D_MODEL)),
            _full((1, D_MODEL)), _full((GM_GROUPS, GM_CHUNK, GM_CHUNK)),
            _full((GM_GROUPS, GM_CHUNK, LANES))],
        out_specs=[row, row],
        out_shape=[jax.ShapeDtypeStruct((SEQ, D_MODEL), F32),
                   jax.ShapeDtypeStruct((SEQ, D_MODEL), BF16)],
        compiler_params=_params("parallel"),
        name="l1_mixer",
    )(h, y_ls, dest, g, w_in, vg, vb, ws, bs_b)


def _router_weights(we, wg, be, bg):
    pad = LANES - N_EXPERTS - N_GROUPS
    w = jnp.concatenate([we, wg, jnp.zeros((D_MODEL, pad), F32)], axis=1)
    b = jnp.concatenate([be, bg, jnp.zeros((pad,), F32)])[None, :]
    w_hi = w.astype(BF16)
    w_lo = (w - w_hi.astype(F32)).astype(BF16)
    return jnp.concatenate([w_hi, w_lo], axis=1), b


def kernel(x, mem, mem_norm_g, norm_mix, norm_xa, norm_ffn, final_norm_g, ab_w_in, ab_conv_w, ab_conv_b, ab_cnorm_g, ab_cnorm_b, ab_w_out, c_w_in, c_vnorm_g, c_vnorm_b, c_ws, c_bs, c_w_out, xa_wq, xa_wk, xa_wv, xa_wo, rt_wg, rt_bg, rt_we, rt_be, ex_w_gate, ex_w_up, ex_w_down):
    assert x.shape == (1, SEQ, D_MODEL) and mem.shape == (1, MEM_LEN, D_MODEL)
    bf = lambda a: a.astype(BF16)
    r1 = lambda a: a.reshape(1, -1)
    h = x[0]

    qk_all, vo_all = _mem_kv(mem[0], r1(mem_norm_g), bf(xa_wk), bf(xa_wv), bf(xa_wq), bf(xa_wo))

    def tail(h, parts, weights, i):
        wr, br = _router_weights(rt_we[i], rt_wg[i], rt_be[i], rt_bg[i])
        h2, xn_ls, dest, units = _out_xattn(
            h, parts, weights, r1(norm_xa[i]), qk_all, vo_all, r1(norm_ffn[i]), wr, br, i)
        unit_of, tile_group = _route_index(units[:, 0, :])
        y_ls = _moe(unit_of, tile_group, xn_ls, ex_w_gate, ex_w_up, ex_w_down, i)
        return h2, y_ls, dest

    w_in = ab_w_in[0]
    q_lo, k_lo, v_lo = 2 * CONV_CH, 2 * CONV_CH + SB_DIM, 2 * CONV_CH + 2 * SB_DIM
    conv_out, q, kt, v = _l0_inproj(
        h, r1(norm_mix[0]), bf(w_in[:, :q_lo]), bf(w_in[:, q_lo:k_lo]), bf(w_in[:, k_lo:v_lo].T),
        bf(w_in[:, v_lo:]), ab_conv_w[0], r1(ab_conv_b[0]), r1(ab_cnorm_g[0]), r1(ab_cnorm_b[0]))
    sb_out = _sb_attention(q, kt, v)
    w_out = bf(ab_w_out[0])
    h2, y_ls, dest = tail(h, [conv_out, sb_out], [w_out[:CONV_CH], w_out[CONV_CH:]], 0)

    bs_b = jnp.broadcast_to(c_bs[0][:, :, None], (GM_GROUPS, GM_CHUNK, LANES))
    h, gated = _l1_mixer(h2, y_ls, dest, r1(norm_mix[1]), bf(c_w_in[0]), r1(c_vnorm_g[0]),
                         r1(c_vnorm_b[0]), c_ws[0], bs_b)
    h2, y_ls, dest = tail(h, [gated], [bf(c_w_out[0])], 1)
    return _unsort_final(h2, y_ls, dest, r1(final_norm_g))[None]
```

```python
import functools

import jax
import jax.numpy as jnp
from jax import lax
from jax.experimental import pallas as pl
from jax.experimental.pallas import tpu as pltpu

D_MODEL = 1024
SEQ = 16384
MEM_LEN = 256
EPS = 1e-6
CONV_CH = 512
CONV_WIDTH = 31
SB_HEADS = 8
SB_HEAD_DIM = 64
SB_DIM = 512
GM_GROUPS = 8
GM_CHUNK = 128
XA_HEADS = 4
XA_HEAD_DIM = 256
N_GROUPS = 4
EXPERTS_PER_GROUP = 4
N_EXPERTS = 16
D_EXPERT = 256

LANES = 128
SUBLANES = 8
ROW_TILE = 512
SB_TQ = 256
SB_TK = 256
SB_LOG_UNDERFLOW = -104.0
SB_HIDDEN_SCORE = -1e30
CONV_HALO = 32
CONV_ROWS = 64
MOE_UNIT = 16
N_ROW_TILES = SEQ // ROW_TILE
LS_TILE = ROW_TILE + N_GROUPS * MOE_UNIT
LS_UNITS = LS_TILE // MOE_UNIT
SORT_UNITS = ROW_TILE // MOE_UNIT
N_SORT_TILES = N_ROW_TILES * LS_UNITS // SORT_UNITS + N_GROUPS
GATE_TERM_STRIDE = 32
LS_WIDTH = D_MODEL + LANES
VMEM_LIMIT = 56 * 1024 * 1024

BF16 = jnp.bfloat16
F32 = jnp.float32


def _params(*semantics):
    return pltpu.CompilerParams(dimension_semantics=semantics, vmem_limit_bytes=VMEM_LIMIT)


def _dot(a, b):
    return jnp.dot(a, b, preferred_element_type=F32)


def _rmsnorm(x, g):
    return x * lax.rsqrt(jnp.mean(x * x, axis=-1, keepdims=True) + EPS) * g


def _layernorm(x, g, b):
    mu = jnp.mean(x, axis=-1, keepdims=True)
    xc = x - mu
    var = jnp.mean(xc * xc, axis=-1, keepdims=True)
    return xc * lax.rsqrt(var + EPS) * g + b


def _full(shape):
    return pl.BlockSpec(shape, lambda *_: (0,) * len(shape))


def _mem_qk_kernel(mem_ref, g_ref, wk_ref, wq_ref, qk_ref):
    memn = _rmsnorm(mem_ref[...], g_ref[...]).astype(BF16)
    k = _dot(memn, wk_ref[0].astype(BF16)).astype(BF16)
    wq = wq_ref[0].astype(BF16)
    for hd in range(XA_HEADS):
        sl = slice(hd * XA_HEAD_DIM, (hd + 1) * XA_HEAD_DIM)
        qk = lax.dot_general(wq[:, sl], k[:, sl], (((1,), (1,)), ((), ())),
                             preferred_element_type=F32)
        qk_ref[0, :, hd * MEM_LEN:(hd + 1) * MEM_LEN] = (qk * (XA_HEAD_DIM ** -0.5)).astype(BF16)


def _mem_vo_kernel(mem_ref, g_ref, wv_ref, wo_ref, vo_ref):
    memn = _rmsnorm(mem_ref[...], g_ref[...]).astype(BF16)
    v = _dot(memn, wv_ref[0].astype(BF16)).astype(BF16)
    wo = wo_ref[0].astype(BF16)
    for hd in range(XA_HEADS):
        sl = slice(hd * XA_HEAD_DIM, (hd + 1) * XA_HEAD_DIM)
        vo_ref[0, hd * MEM_LEN:(hd + 1) * MEM_LEN, :] = _dot(v[:, sl], wo[sl, :]).astype(BF16)


def _mem_weights(kernel_fn, name, mem, g, w_mem, w_attn, out_rows, out_cols):
    depth = w_mem.shape[0]
    wspec = pl.BlockSpec((1, D_MODEL, D_MODEL), lambda i: (i, 0, 0))
    return pl.pallas_call(
        kernel_fn,
        grid=(depth,),
        in_specs=[_full((MEM_LEN, D_MODEL)), _full((1, D_MODEL)), wspec, wspec],
        out_specs=pl.BlockSpec((1, out_rows, out_cols), lambda i: (i, 0, 0)),
        out_shape=jax.ShapeDtypeStruct((depth, out_rows, out_cols), BF16),
        compiler_params=_params("arbitrary"),
        name=name,
    )(mem, g, w_mem, w_attn)


def _l0_inproj_kernel(x_ref, g_ref, w_in_ref, cw_ref, cb_ref, lg_ref, lb_ref,
                      conv_ref, q_ref, kt_ref, v_ref,
                      ext_ref, y_ref, w_ag_ref, w_q_ref, w_kt_ref, w_v_ref):
    tm = x_ref.shape[0]

    @pl.when(pl.program_id(0) == 0)
    def _():
        ext_ref[0:CONV_HALO, :] = jnp.zeros((CONV_HALO, CONV_CH), F32)
        q_lo, k_lo, v_lo = 2 * CONV_CH, 2 * CONV_CH + SB_DIM, 2 * CONV_CH + 2 * SB_DIM
        w_ag_ref[...] = w_in_ref[:, :q_lo].astype(BF16)
        w_q_ref[...] = w_in_ref[:, q_lo:k_lo].astype(BF16)
        w_kt_ref[...] = jnp.transpose(w_in_ref[:, k_lo:v_lo]).astype(BF16)
        w_v_ref[...] = w_in_ref[:, v_lo:].astype(BF16)

    xn = _rmsnorm(x_ref[...], g_ref[...]).astype(BF16)
    ag = _dot(xn, w_ag_ref[...])
    ext_ref[CONV_HALO:, :] = ag[:, :CONV_CH] * jax.nn.sigmoid(ag[:, CONV_CH:])
    q_ref[...] = (_dot(xn, w_q_ref[...]) * (SB_HEAD_DIM ** -0.5)).astype(BF16)
    kt_ref[...] = lax.dot_general(w_kt_ref[...], xn, (((1,), (1,)), ((), ())),
                                  preferred_element_type=F32).astype(BF16)
    v_ref[...] = _dot(xn, w_v_ref[...]).astype(BF16)

    first_tap = CONV_HALO - (CONV_WIDTH - 1)
    for r0 in range(0, tm, CONV_ROWS):
        for j in range(CONV_CH // LANES):
            lanes = slice(j * LANES, (j + 1) * LANES)
            win = ext_ref[r0:r0 + CONV_ROWS + CONV_HALO, lanes]
            acc = jnp.zeros((CONV_ROWS, LANES), F32) + cb_ref[:, lanes]
            for sub in range(SUBLANES):
                offsets = [o for o in range(first_tap, first_tap + CONV_WIDTH)
                           if o % SUBLANES == sub]
                shifted = pltpu.roll(win, win.shape[0] - sub, axis=0) if sub else win
                for o in offsets:
                    tap = cw_ref[o - first_tap:o - first_tap + 1, lanes]
                    acc = acc + shifted[o - sub:o - sub + CONV_ROWS, :] * tap
            y_ref[r0:r0 + CONV_ROWS, lanes] = acc
    y = _layernorm(y_ref[...], lg_ref[...], lb_ref[...])
    conv_ref[...] = (y * jax.nn.sigmoid(y)).astype(BF16)
    ext_ref[0:CONV_HALO, :] = ext_ref[tm:tm + CONV_HALO, :]


def _l0_inproj(x, g, w_in, cw, cb, lg, lb):
    tm = ROW_TILE
    row = lambda n: pl.BlockSpec((tm, n), lambda i: (i, 0))
    return pl.pallas_call(
        _l0_inproj_kernel,
        grid=(N_ROW_TILES,),
        in_specs=[row(D_MODEL), _full((1, D_MODEL)),
                  pl.BlockSpec((None, D_MODEL, 2 * CONV_CH + 3 * SB_DIM), lambda i: (0, 0, 0)),
                  _full((CONV_WIDTH, CONV_CH)), _full((1, CONV_CH)),
                  _full((1, CONV_CH)), _full((1, CONV_CH))],
        out_specs=[row(CONV_CH), row(SB_DIM), pl.BlockSpec((SB_DIM, tm), lambda i: (0, i)),
                   row(SB_DIM)],
        out_shape=[jax.ShapeDtypeStruct((SEQ, CONV_CH), BF16),
                   jax.ShapeDtypeStruct((SEQ, SB_DIM), BF16),
                   jax.ShapeDtypeStruct((SB_DIM, SEQ), BF16),
                   jax.ShapeDtypeStruct((SEQ, SB_DIM), BF16)],
        scratch_shapes=[pltpu.VMEM((tm + CONV_HALO, CONV_CH), F32),
                        pltpu.VMEM((tm, CONV_CH), F32),
                        pltpu.VMEM((D_MODEL, 2 * CONV_CH), BF16),
                        pltpu.VMEM((D_MODEL, SB_DIM), BF16),
                        pltpu.VMEM((SB_DIM, D_MODEL), BF16),
                        pltpu.VMEM((D_MODEL, SB_DIM), BF16)],
        compiler_params=_params("arbitrary"),
        name="l0_inproj_conv",
    )(x, g, w_in, cw, cb, lg, lb)


def _sb_kernel(q_ref, kt_ref, v_ref, o_ref, acc_ref, carry_ref):
    qi = pl.program_id(1)
    tq, tk = SB_TQ, SB_TK
    lane = lax.broadcasted_iota(jnp.int32, (tq, LANES), 1)
    q2 = q_ref[...]
    zero = jnp.zeros_like(q2)
    qh = (jnp.where(lane < SB_HEAD_DIM, q2, zero), jnp.where(lane >= SB_HEAD_DIM, q2, zero))
    jj = lax.broadcasted_iota(jnp.int32, (tk, tk), 0)
    ss = lax.broadcasted_iota(jnp.int32, (tk, tk), 1)
    minus_later = jnp.where(jj > ss, -1.0, 0.0).astype(BF16)
    minus_later2 = jnp.concatenate([minus_later, minus_later], axis=0)

    def scores(h, kb, valid):
        z = _dot(qh[h], kt_ref[:, pl.ds(pl.multiple_of(kb * tk, tk), tk)])
        return z if valid is None else jnp.where(valid, z, SB_HIDDEN_SCORE)

    def stay(z):
        sp = jnp.maximum(z, 0.0) + jnp.log(1.0 + jnp.exp(-jnp.abs(z)))
        hi = sp.astype(BF16)
        return sp, hi, (sp - hi.astype(F32)).astype(BF16)

    def later_sum(hi, lo):
        return _dot(jnp.concatenate([hi, lo], axis=1), minus_later2)

    def weighted_values(z, sp, after, kb):
        w = jnp.exp((z - sp) + after)
        return _dot(w.astype(BF16), v_ref[pl.ds(pl.multiple_of(kb * tk, tk), tk), :])

    def block(h, kb, valid):
        z = scores(h, kb, valid)
        sp, hi, lo = stay(z)
        pv = weighted_values(z, sp, later_sum(hi, lo), kb)
        return pv, jnp.sum(sp, axis=1, keepdims=True)

    row = lax.broadcasted_iota(jnp.int32, (tq, tk), 0)
    col = lax.broadcasted_iota(jnp.int32, (tq, tk), 1)
    has_prev = qi > 0
    chains = [(h, kb, valid) for h in range(2)
              for kb, valid in ((qi, col < row), (jnp.maximum(qi - 1, 0), None))]
    zs = [scores(h, kb, valid) for h, kb, valid in chains]
    stays = [stay(z) for z in zs]
    afters = [later_sum(hi, lo) for _, hi, lo in stays]
    pvs = [weighted_values(z, sp, after, kb)
           for z, (sp, _, _), after, (_, kb, _) in zip(zs, stays, afters, chains)]
    sums = [jnp.sum(sp, axis=1, keepdims=True) for sp, _, _ in stays]
    for h in range(2):
        (pv_diag, pv_prev), (sp_diag, sp_prev) = pvs[2 * h:2 * h + 2], sums[2 * h:2 * h + 2]
        acc_ref[h] = pv_diag + jnp.where(has_prev, jnp.exp(-sp_diag), 0.0) * pv_prev
        carry_ref[h] = jnp.broadcast_to(-(sp_diag + sp_prev), (tq, LANES))

    def alive():
        return jnp.max(carry_ref[...]) > SB_LOG_UNDERFLOW

    def cond(state):
        kb, go = state
        return (kb >= 0) & go

    def body(state):
        kb, _ = state
        for h in range(2):
            pv, sp_sum = block(h, kb, None)
            carry = carry_ref[h]
            acc_ref[h] += jnp.exp(carry) * pv
            carry_ref[h] = carry - sp_sum
        return kb - 1, alive()

    lax.while_loop(cond, body, (qi - 2, alive()))
    o_ref[...] = jnp.where(lane < SB_HEAD_DIM, acc_ref[0], acc_ref[1]).astype(BF16)


def _sb_attention(q, kt, v):
    tq = SB_TQ
    return pl.pallas_call(
        _sb_kernel,
        grid=(SB_DIM // LANES, SEQ // tq),
        in_specs=[pl.BlockSpec((tq, LANES), lambda p, i: (i, p)),
                  pl.BlockSpec((LANES, SEQ), lambda p, i: (p, 0)),
                  pl.BlockSpec((SEQ, LANES), lambda p, i: (0, p))],
        out_specs=pl.BlockSpec((tq, LANES), lambda p, i: (i, p)),
        out_shape=jax.ShapeDtypeStruct((SEQ, SB_DIM), BF16),
        scratch_shapes=[pltpu.VMEM((2, tq, LANES), F32), pltpu.VMEM((2, tq, LANES), F32)],
        compiler_params=_params("parallel", "parallel"),
        name="sb_attention",
    )(q, kt, v)


def _router_gates(logits):
    col = lax.broadcasted_iota(jnp.int32, logits.shape, 1)
    colf = col.astype(F32)
    ninf = -jnp.inf
    first = lambda hit: jnp.min(jnp.where(hit, colf, float(LANES)), axis=1, keepdims=True)

    is_group = (col >= N_EXPERTS) & (col < N_EXPERTS + N_GROUPS)
    lg = jnp.where(is_group, logits, ninf)
    gmax = jnp.max(lg, axis=1, keepdims=True)
    pg_top = 1.0 / jnp.sum(jnp.exp(lg - gmax), axis=1, keepdims=True)
    g_idx = first(lg == gmax) - float(N_EXPERTS)

    in_group = (col < N_EXPERTS) & ((col // EXPERTS_PER_GROUP).astype(F32) == g_idx)
    le = jnp.where(in_group, logits, ninf)
    m1 = jnp.max(le, axis=1, keepdims=True)
    i1 = first(le == m1)
    le2 = jnp.where(colf == i1, ninf, le)
    m2 = jnp.max(le2, axis=1, keepdims=True)
    i2 = first(le2 == m2)
    ee = jnp.exp(le - m1)
    pe = ee / jnp.sum(ee, axis=1, keepdims=True)
    p1 = jnp.sum(jnp.where(colf == i1, pe, 0.0), axis=1, keepdims=True)
    p2 = jnp.sum(jnp.where(colf == i2, pe, 0.0), axis=1, keepdims=True)
    den = p1 + p2
    gates = jnp.where(colf == i1, p1 / den * pg_top,
                      jnp.where(colf == i2, p2 / den * pg_top, 0.0))
    return gates, g_idx


def _local_sort(g_idx):
    tm = g_idx.shape[0]
    lane = lax.broadcasted_iota(jnp.int32, (tm, LANES), 1)
    member = jnp.where(lane.astype(F32) == g_idx, 1.0, 0.0)
    r_i = lax.broadcasted_iota(jnp.int32, (tm, tm), 0)
    c_i = lax.broadcasted_iota(jnp.int32, (tm, tm), 1)
    before = jnp.where(c_i < r_i, 1.0, 0.0).astype(BF16)
    rank = jnp.sum(member * _dot(before, member.astype(BF16)), axis=1, keepdims=True)
    count = jnp.sum(member, axis=0, keepdims=True)
    padded = jnp.floor((count + (MOE_UNIT - 1)) * (1.0 / MOE_UNIT)) * MOE_UNIT
    lane1 = lax.broadcasted_iota(jnp.int32, (1, LANES), 1)
    offset = jnp.zeros((1, LANES), F32)
    start = jnp.zeros((1, 1), F32)
    for g in range(N_GROUPS):
        offset = offset + jnp.where((lane1 == g) | (lane1 == N_GROUPS + g), start, 0.0)
        start = start + jnp.sum(jnp.where(lane1 == g, padded, 0.0), axis=1, keepdims=True)
    dest = jnp.sum(member * offset, axis=1, keepdims=True) + rank
    rows = jnp.where(lane1 == N_GROUPS - 1, LS_TILE - offset, padded)
    units = jnp.where(lane1 < N_GROUPS, rows, offset) * (1.0 / MOE_UNIT)
    return dest, units.astype(jnp.int32)


def _pack_gates(gates):
    hi = gates.astype(BF16).astype(F32)
    rest = gates - hi
    mid = rest.astype(BF16).astype(F32)
    lo = rest - mid
    packed = hi + pltpu.roll(mid, GATE_TERM_STRIDE, axis=1) + pltpu.roll(lo, 2 * GATE_TERM_STRIDE, axis=1)
    return packed.astype(BF16)


def _unpack_gate(packed, expert):
    lane = lax.broadcasted_iota(jnp.int32, packed.shape, 1)
    terms = jnp.where(lane % GATE_TERM_STRIDE == expert, packed.astype(F32), 0.0)
    return jnp.sum(terms, axis=1, keepdims=True)


def _out_xattn_kernel(n_parts, *refs):
    h_ref = refs[0]
    part_refs = refs[1:1 + n_parts]
    w_refs = refs[1 + n_parts:1 + 2 * n_parts]
    (gx_ref, qk_ref, vo_ref, gf_ref, wr_ref, br_ref,
     h2_ref, xn_ls_ref, dest_ref, units_ref) = refs[1 + 2 * n_parts:]

    h1 = h_ref[...]
    for p_ref, w_ref in zip(part_refs, w_refs):
        h1 = h1 + _dot(p_ref[...], w_ref[...])

    xn = _rmsnorm(h1, gx_ref[...]).astype(BF16)
    s_all = _dot(xn, qk_ref[0])
    probs = []
    for hd in range(XA_HEADS):
        s = s_all[:, hd * MEM_LEN:(hd + 1) * MEM_LEN]
        e = jnp.exp(s - jnp.max(s, axis=1, keepdims=True))
        probs.append((e / jnp.sum(e, axis=1, keepdims=True)).astype(BF16))
    h2 = h1 + _dot(jnp.concatenate(probs, axis=1), vo_ref[0])
    h2_ref[...] = h2

    xf = _rmsnorm(h2, gf_ref[...])
    x_hi = xf.astype(BF16)
    x_lo = (xf - x_hi.astype(F32)).astype(BF16)
    both = _dot(x_hi, wr_ref[...])
    logits = both[:, :LANES] + both[:, LANES:] + _dot(x_lo, wr_ref[:, :LANES]) + br_ref[...]

    gates, g_idx = _router_gates(logits)
    dest, units = _local_sort(g_idx)
    tm = dest.shape[0]
    dest_ref[...] = jnp.broadcast_to(dest, (tm, LANES))
    units_ref[0] = jnp.broadcast_to(units, (SUBLANES, LANES))
    dest_row = jnp.transpose(jnp.broadcast_to(dest, (tm, LANES)))[0:1, :]
    slot = lax.broadcasted_iota(jnp.int32, (LS_TILE, tm), 0).astype(F32)
    place = jnp.where(slot == dest_row, 1.0, 0.0).astype(BF16)
    routed = jnp.concatenate([x_hi, _pack_gates(gates)], axis=1)
    xn_ls_ref[...] = _dot(place, routed).astype(BF16)


def _out_xattn(h, parts, weights, gx, qk_all, vo_all, gf, wr, br, layer):
    tm = ROW_TILE
    row = lambda n, rows=tm: pl.BlockSpec((rows, n), lambda i: (i, 0))
    qk_spec = pl.BlockSpec((1, D_MODEL, XA_HEADS * MEM_LEN), lambda i: (layer, 0, 0))
    vo_spec = pl.BlockSpec((1, XA_HEADS * MEM_LEN, D_MODEL), lambda i: (layer, 0, 0))
    in_specs = ([row(D_MODEL)] + [row(p.shape[1]) for p in parts]
                + [_full(w.shape) for w in weights]
                + [_full((1, D_MODEL)), qk_spec, vo_spec, _full((1, D_MODEL)),
                   _full((D_MODEL, 2 * LANES)), _full((1, LANES))])
    return pl.pallas_call(
        functools.partial(_out_xattn_kernel, len(parts)),
        grid=(N_ROW_TILES,),
        in_specs=in_specs,
        out_specs=[row(D_MODEL), row(LS_WIDTH, LS_TILE), row(LANES),
                   pl.BlockSpec((1, SUBLANES, LANES), lambda i: (i, 0, 0))],
        out_shape=[jax.ShapeDtypeStruct((SEQ, D_MODEL), F32),
                   jax.ShapeDtypeStruct((N_ROW_TILES * LS_TILE, LS_WIDTH), BF16),
                   jax.ShapeDtypeStruct((SEQ, LANES), F32),
                   jax.ShapeDtypeStruct((N_ROW_TILES, SUBLANES, LANES), jnp.int32)],
        compiler_params=_params("parallel"),
        name="out_xattn_router",
    )(h, *parts, *weights, gx, qk_all, vo_all, gf, wr, br)


def _route_index_kernel(units_ref, unit_of_ref, tile_group_ref):
    def no_unit(k, _):
        unit_of_ref[k] = -1
        return 0

    def no_group(j, _):
        tile_group_ref[j] = -1
        return 0

    pos = 0
    for g in range(N_GROUPS):
        first_tile = pos // SORT_UNITS

        def per_row_tile(i, pos):
            n = units_ref[i, g]
            base = i * LS_UNITS + units_ref[i, N_GROUPS + g]

            def per_unit(k, _):
                unit_of_ref[pos + k] = base + k
                return 0

            lax.fori_loop(0, n, per_unit, 0)
            return pos + n

        pos = lax.fori_loop(0, N_ROW_TILES, per_row_tile, pos)
        end_tile = (pos + SORT_UNITS - 1) // SORT_UNITS

        def mark_tile(j, _):
            tile_group_ref[j] = g
            return 0

        lax.fori_loop(first_tile, end_tile, mark_tile, 0)
        lax.fori_loop(pos, end_tile * SORT_UNITS, no_unit, 0)
        pos = end_tile * SORT_UNITS

    lax.fori_loop(pos, N_SORT_TILES * SORT_UNITS, no_unit, 0)
    lax.fori_loop(pos // SORT_UNITS, N_SORT_TILES, no_group, 0)


def _route_index(units):
    smem = lambda: pl.BlockSpec(memory_space=pltpu.SMEM)
    return pl.pallas_call(
        _route_index_kernel,
        in_specs=[smem()],
        out_specs=[smem(), smem()],
        out_shape=[jax.ShapeDtypeStruct((N_SORT_TILES * SORT_UNITS,), jnp.int32),
                   jax.ShapeDtypeStruct((N_SORT_TILES,), jnp.int32)],
        name="moe_route_index",
    )(units)


def _moe_kernel(unit_of_ref, tile_group_ref, xn_hbm, wg_ref, wu_ref, wd_ref, y_hbm,
                xbuf, ybuf, acc_ref, wg_bf, wu_bf, wd_bf, gather_sem, scatter_sem):
    j = pl.program_id(0)
    n_tiles = pl.num_programs(0)
    slot = j % 2

    def unit_rows(k):
        return pl.ds(pl.multiple_of(k * MOE_UNIT, MOE_UNIT), MOE_UNIT)

    def gather_copy(s, k, u):
        return pltpu.make_async_copy(xn_hbm.at[unit_rows(u), :], xbuf.at[s, unit_rows(k), :],
                                     gather_sem.at[s])

    def scatter_copy(s, k, u):
        return pltpu.make_async_copy(ybuf.at[s, unit_rows(k), :], y_hbm.at[unit_rows(u), :],
                                     scatter_sem.at[s])

    def is_full(tile):
        return unit_of_ref[tile * SORT_UNITS + SORT_UNITS - 1] >= 0

    def for_units(tile, full_fn, partial_fn):
        def run(fn, unroll):
            def unit(k, _):
                fn(k, unit_of_ref[tile * SORT_UNITS + k])
                return 0

            lax.fori_loop(0, SORT_UNITS, unit, 0, unroll=unroll)

        @pl.when(is_full(tile))
        def _():
            run(full_fn, 8)

        @pl.when(jnp.logical_not(is_full(tile)))
        def _():
            def guarded(k, u):
                @pl.when(u >= 0)
                def _():
                    full_fn(k, u)

                if partial_fn is not None:
                    @pl.when(u < 0)
                    def _():
                        partial_fn(k)

            run(guarded, 1)

    def start_gather(tile, s):
        def zero_fill(k):
            xbuf[s, unit_rows(k), :] = jnp.zeros((MOE_UNIT, LS_WIDTH), BF16)

        for_units(tile, lambda k, u: gather_copy(s, k, u).start(), zero_fill)

    def start_scatter(tile, s):
        for_units(tile, lambda k, u: scatter_copy(s, k, u).start(), None)

    def wait_units(tile, whole_copy, unit_copy):
        @pl.when(is_full(tile))
        def _():
            whole_copy.wait()

        @pl.when(jnp.logical_not(is_full(tile)))
        def _():
            def unit(k, _):
                u = unit_of_ref[tile * SORT_UNITS + k]

                @pl.when(u >= 0)
                def _():
                    unit_copy(k, u).wait()

                return 0

            lax.fori_loop(0, SORT_UNITS, unit, 0)

    def wait_gather(tile, s):
        whole = pltpu.make_async_copy(xn_hbm.at[pl.ds(0, ROW_TILE), :], xbuf.at[s], gather_sem.at[s])
        wait_units(tile, whole, lambda k, u: gather_copy(s, k, u))

    def wait_scatter(tile, s):
        whole = pltpu.make_async_copy(ybuf.at[s], y_hbm.at[pl.ds(0, ROW_TILE), :], scatter_sem.at[s])
        wait_units(tile, whole, lambda k, u: scatter_copy(s, k, u))

    @pl.when(j == 0)
    def _():
        start_gather(0, 0)

    @pl.when(j + 1 < n_tiles)
    def _():
        start_gather(j + 1, 1 - slot)

    grp = tile_group_ref[j]
    new_group = (j == 0) | (grp != tile_group_ref[jnp.maximum(j - 1, 0)])

    @pl.when((grp >= 0) & new_group)
    def _():
        wg_bf[...] = wg_ref[...].astype(BF16)
        wu_bf[...] = wu_ref[...].astype(BF16)
        wd_bf[...] = wd_ref[...].astype(BF16)

    wait_gather(j, slot)

    @pl.when(j >= 2)
    def _():
        wait_scatter(j - 2, slot)

    @pl.when(grp >= 0)
    def _():
        xn = xbuf[slot, :, :D_MODEL]
        packed_gates = xbuf[slot, :, D_MODEL:]
        for e in range(EXPERTS_PER_GROUP):
            gate = _unpack_gate(packed_gates, grp * EXPERTS_PER_GROUP + e)
            hg = _dot(xn, wg_bf[e])
            hu = _dot(xn, wu_bf[e])
            act = (hg * jax.nn.sigmoid(hg)) * hu * gate
            out = _dot(act.astype(BF16), wd_bf[e])
            if e == 0:
                acc_ref[...] = out
            else:
                acc_ref[...] += out
        y = acc_ref[...]
        y_hi = y.astype(BF16)
        ybuf[slot, :, :D_MODEL] = y_hi
        ybuf[slot, :, D_MODEL:] = (y - y_hi.astype(F32)).astype(BF16)

    start_scatter(j, slot)

    @pl.when(j == n_tiles - 1)
    def _():
        wait_scatter(j - 1, 1 - slot)
        wait_scatter(j, slot)


def _moe(unit_of, tile_group, xn_ls, wg, wu, wd, layer):
    group_of = lambda j, unit_of, tile_group: jnp.where(tile_group[j] < 0, N_GROUPS - 1,
                                                        tile_group[j])
    up_spec = pl.BlockSpec((None, EXPERTS_PER_GROUP, D_MODEL, D_EXPERT),
                           lambda j, u, t: (layer, group_of(j, u, t), 0, 0))
    down_spec = pl.BlockSpec((None, EXPERTS_PER_GROUP, D_EXPERT, D_MODEL),
                             lambda j, u, t: (layer, group_of(j, u, t), 0, 0))
    any_spec = pl.BlockSpec(memory_space=pl.ANY)
    return pl.pallas_call(
        _moe_kernel,
        grid_spec=pltpu.PrefetchScalarGridSpec(
            num_scalar_prefetch=2,
            grid=(N_SORT_TILES,),
            in_specs=[any_spec, up_spec, up_spec, down_spec],
            out_specs=any_spec,
            scratch_shapes=[pltpu.VMEM((2, ROW_TILE, LS_WIDTH), BF16),
                            pltpu.VMEM((2, ROW_TILE, 2 * D_MODEL), BF16),
                            pltpu.VMEM((ROW_TILE, D_MODEL), F32),
                            pltpu.VMEM((EXPERTS_PER_GROUP, D_MODEL, D_EXPERT), BF16),
                            pltpu.VMEM((EXPERTS_PER_GROUP, D_MODEL, D_EXPERT), BF16),
                            pltpu.VMEM((EXPERTS_PER_GROUP, D_EXPERT, D_MODEL), BF16),
                            pltpu.SemaphoreType.DMA((2,)),
                            pltpu.SemaphoreType.DMA((2,))]),
        out_shape=jax.ShapeDtypeStruct((N_ROW_TILES * LS_TILE, 2 * D_MODEL), BF16),
        compiler_params=_params("arbitrary"),
        name="moe_experts",
    )(unit_of, tile_group, xn_ls, wg, wu, wd)


def _add_unsorted(h_ref, y_ls_ref, dest_ref):
    tm = h_ref.shape[0]
    slot = lax.broadcasted_iota(jnp.int32, (tm, LS_TILE), 1).astype(F32)
    pick = jnp.where(slot == dest_ref[:, 0:1], 1.0, 0.0).astype(BF16)
    y = _dot(pick, y_ls_ref[...])
    return h_ref[...] + (y[:, :D_MODEL] + y[:, D_MODEL:])


def _unsort_final_kernel(h_ref, y_ls_ref, dest_ref, gfin_ref, o_ref):
    o_ref[...] = _rmsnorm(_add_unsorted(h_ref, y_ls_ref, dest_ref), gfin_ref[...])


def _unsort_specs():
    row = lambda n: pl.BlockSpec((ROW_TILE, n), lambda i: (i, 0))
    return [row(D_MODEL), pl.BlockSpec((LS_TILE, 2 * D_MODEL), lambda i: (i, 0)), row(LANES)]


def _unsort_final(h, y_ls, dest, gfin):
    return pl.pallas_call(
        _unsort_final_kernel,
        grid=(N_ROW_TILES,),
        in_specs=_unsort_specs() + [_full((1, D_MODEL))],
        out_specs=pl.BlockSpec((ROW_TILE, D_MODEL), lambda i: (i, 0)),
        out_shape=jax.ShapeDtypeStruct((SEQ, D_MODEL), F32),
        compiler_params=_params("parallel"),
        name="moe_unsort_final_norm",
    )(h, y_ls, dest, gfin)


def _l1_mixer_kernel(h_ref, y_ls_ref, dest_ref, g_ref, w_in_ref, vg_ref, vb_ref, ws_ref, bs_ref,
                     x_ref, o_ref, w_in_bf):
    tm = h_ref.shape[0]

    @pl.when(pl.program_id(0) == 0)
    def _():
        w_in_bf[...] = w_in_ref[...].astype(BF16)

    x = _add_unsorted(h_ref, y_ls_ref, dest_ref)
    x_ref[...] = x
    xn = _rmsnorm(x, g_ref[...]).astype(BF16)
    p = _dot(xn, w_in_bf[...])
    p = 0.5 * p * (1.0 + lax.erf(p * (2.0 ** -0.5)))
    u = p[:, :D_MODEL]
    v = _layernorm(p[:, D_MODEL:], vg_ref[...], vb_ref[...]).astype(BF16)
    t_idx = lax.broadcasted_iota(jnp.int32, (GM_CHUNK, GM_CHUNK), 0)
    s_idx = lax.broadcasted_iota(jnp.int32, (GM_CHUNK, GM_CHUNK), 1)
    for g in range(GM_GROUPS):
        cols = slice(g * LANES, (g + 1) * LANES)
        wmix = jnp.where(t_idx >= s_idx, ws_ref[g], 0.0).astype(BF16)
        for c in range(tm // GM_CHUNK):
            rows = slice(c * GM_CHUNK, (c + 1) * GM_CHUNK)
            mixed = _dot(wmix, v[rows, cols]) + bs_ref[g]
            o_ref[rows, cols] = (u[rows, cols] * mixed).astype(BF16)


def _l1_mixer(h, y_ls, dest, g, w_in, vg, vb, ws, bs_b):
    row = pl.BlockSpec((ROW_TILE, D_MODEL), lambda i: (i, 0))
    return pl.pallas_call(
        _l1_mixer_kernel,
        grid=(N_ROW_TILES,),
        in_specs=_unsort_specs() + [
            _full((1, D_MODEL)),
            pl.BlockSpec((None, D_MODEL, 2 * D_MODEL), lambda i: (0, 0, 0)),
            _full((1, D_MODEL)), _full((1, D_MODEL)), _full((GM_GROUPS, GM_CHUNK, GM_CHUNK)),
            _full((GM_GROUPS, GM_CHUNK, LANES))],
        out_specs=[row, row],
        out_shape=[jax.ShapeDtypeStruct((SEQ, D_MODEL), F32),
                   jax.ShapeDtypeStruct((SEQ, D_MODEL), BF16)],
        scratch_shapes=[pltpu.VMEM((D_MODEL, 2 * D_MODEL), BF16)],
        compiler_params=_params("arbitrary"),
        name="l1_mixer",
    )(h, y_ls, dest, g, w_in, vg, vb, ws, bs_b)


def _router_weights(we, wg, be, bg):
    pad = LANES - N_EXPERTS - N_GROUPS
    w = jnp.concatenate([we, wg, jnp.zeros((D_MODEL, pad), F32)], axis=1)
    b = jnp.concatenate([be, bg, jnp.zeros((pad,), F32)])[None, :]
    w_hi = w.astype(BF16)
    w_lo = (w - w_hi.astype(F32)).astype(BF16)
    return jnp.concatenate([w_hi, w_lo], axis=1), b


def kernel(x, mem, mem_norm_g, norm_mix, norm_xa, norm_ffn, final_norm_g, ab_w_in, ab_conv_w, ab_conv_b, ab_cnorm_g, ab_cnorm_b, ab_w_out, c_w_in, c_vnorm_g, c_vnorm_b, c_ws, c_bs, c_w_out, xa_wq, xa_wk, xa_wv, xa_wo, rt_wg, rt_bg, rt_we, rt_be, ex_w_gate, ex_w_up, ex_w_down):
    assert x.shape == (1, SEQ, D_MODEL) and mem.shape == (1, MEM_LEN, D_MODEL)
    bf = lambda a: a.astype(BF16)
    r1 = lambda a: a.reshape(1, -1)
    h = x[0]

    memory_len = XA_HEADS * MEM_LEN
    qk_all = _mem_weights(_mem_qk_kernel, "mem_qk", mem[0], r1(mem_norm_g), xa_wk, xa_wq,
                          D_MODEL, memory_len)
    vo_all = _mem_weights(_mem_vo_kernel, "mem_vo", mem[0], r1(mem_norm_g), xa_wv, xa_wo,
                          memory_len, D_MODEL)

    def tail(h, parts, weights, i):
        wr, br = _router_weights(rt_we[i], rt_wg[i], rt_be[i], rt_bg[i])
        h2, xn_ls, dest, units = _out_xattn(
            h, parts, weights, r1(norm_xa[i]), qk_all, vo_all, r1(norm_ffn[i]), wr, br, i)
        unit_of, tile_group = _route_index(units[:, 0, :])
        y_ls = _moe(unit_of, tile_group, xn_ls, ex_w_gate, ex_w_up, ex_w_down, i)
        return h2, y_ls, dest

    conv_out, q, kt, v = _l0_inproj(
        h, r1(norm_mix[0]), ab_w_in, ab_conv_w[0], r1(ab_conv_b[0]), r1(ab_cnorm_g[0]),
        r1(ab_cnorm_b[0]))
    sb_out = _sb_attention(q, kt, v)
    w_out = bf(ab_w_out[0])
    h2, y_ls, dest = tail(h, [conv_out, sb_out], [w_out[:CONV_CH], w_out[CONV_CH:]], 0)

    bs_b = jnp.broadcast_to(c_bs[0][:, :, None], (GM_GROUPS, GM_CHUNK, LANES))
    h, gated = _l1_mixer(h2, y_ls, dest, r1(norm_mix[1]), c_w_in, r1(c_vnorm_g[0]),
                         r1(c_vnorm_b[0]), c_ws[0], bs_b)
    h2, y_ls, dest = tail(h, [gated], [bf(c_w_out[0])], 1)
    return _unsort_final(h2, y_ls, dest, r1(final_norm_g))[None]
```

```python
import functools

import jax
import jax.numpy as jnp
from jax import lax
from jax.experimental import pallas as pl
from jax.experimental.pallas import tpu as pltpu

D_MODEL = 1024
SEQ = 16384
MEM_LEN = 256
EPS = 1e-6
CONV_CH = 512
CONV_WIDTH = 31
SB_HEADS = 8
SB_HEAD_DIM = 64
SB_DIM = 512
GM_GROUPS = 8
GM_CHUNK = 128
XA_HEADS = 4
XA_HEAD_DIM = 256
N_GROUPS = 4
EXPERTS_PER_GROUP = 4
N_EXPERTS = 16
D_EXPERT = 256

LANES = 128
SUBLANES = 8
ROW_TILE = 512
SB_TQ = 256
SB_TK = 256
SB_SUB = 2
SB_LOG_UNDERFLOW = -104.0
SB_HIDDEN_SCORE = -1e30
CONV_HALO = 32
CONV_ROWS = 64
MOE_UNIT = 16
N_ROW_TILES = SEQ // ROW_TILE
LS_TILE = ROW_TILE + N_GROUPS * MOE_UNIT
LS_UNITS = LS_TILE // MOE_UNIT
SORT_UNITS = ROW_TILE // MOE_UNIT
N_SORT_TILES = N_ROW_TILES * LS_UNITS // SORT_UNITS + N_GROUPS
GATE_TERM_STRIDE = 32
LS_WIDTH = D_MODEL + LANES
VMEM_LIMIT = 56 * 1024 * 1024

BF16 = jnp.bfloat16
F32 = jnp.float32


def _params(*semantics):
    return pltpu.CompilerParams(dimension_semantics=semantics, vmem_limit_bytes=VMEM_LIMIT)


def _dot(a, b):
    return jnp.dot(a, b, preferred_element_type=F32)


def _rmsnorm(x, g):
    return x * lax.rsqrt(jnp.mean(x * x, axis=-1, keepdims=True) + EPS) * g


def _layernorm(x, g, b):
    mu = jnp.mean(x, axis=-1, keepdims=True)
    xc = x - mu
    var = jnp.mean(xc * xc, axis=-1, keepdims=True)
    return xc * lax.rsqrt(var + EPS) * g + b


def _full(shape):
    return pl.BlockSpec(shape, lambda *_: (0,) * len(shape))


def _mem_qk_kernel(mem_ref, g_ref, wk_ref, wq_ref, qk_ref):
    memn = _rmsnorm(mem_ref[...], g_ref[...]).astype(BF16)
    k = _dot(memn, wk_ref[0].astype(BF16)).astype(BF16)
    wq = wq_ref[0].astype(BF16)
    for hd in range(XA_HEADS):
        sl = slice(hd * XA_HEAD_DIM, (hd + 1) * XA_HEAD_DIM)
        qk = lax.dot_general(wq[:, sl], k[:, sl], (((1,), (1,)), ((), ())),
                             preferred_element_type=F32)
        qk_ref[0, :, hd * MEM_LEN:(hd + 1) * MEM_LEN] = (qk * (XA_HEAD_DIM ** -0.5)).astype(BF16)


def _mem_vo_kernel(mem_ref, g_ref, wv_ref, wo_ref, vo_ref):
    memn = _rmsnorm(mem_ref[...], g_ref[...]).astype(BF16)
    v = _dot(memn, wv_ref[0].astype(BF16)).astype(BF16)
    wo = wo_ref[0].astype(BF16)
    for hd in range(XA_HEADS):
        sl = slice(hd * XA_HEAD_DIM, (hd + 1) * XA_HEAD_DIM)
        vo_ref[0, hd * MEM_LEN:(hd + 1) * MEM_LEN, :] = _dot(v[:, sl], wo[sl, :]).astype(BF16)


def _mem_weights(kernel_fn, name, mem, g, w_mem, w_attn, out_rows, out_cols):
    depth = w_mem.shape[0]
    wspec = pl.BlockSpec((1, D_MODEL, D_MODEL), lambda i: (i, 0, 0))
    return pl.pallas_call(
        kernel_fn,
        grid=(depth,),
        in_specs=[_full((MEM_LEN, D_MODEL)), _full((1, D_MODEL)), wspec, wspec],
        out_specs=pl.BlockSpec((1, out_rows, out_cols), lambda i: (i, 0, 0)),
        out_shape=jax.ShapeDtypeStruct((depth, out_rows, out_cols), BF16),
        compiler_params=_params("arbitrary"),
        name=name,
    )(mem, g, w_mem, w_attn)


def _l0_inproj_kernel(x_ref, g_ref, w_in_ref, cw_ref, cb_ref, lg_ref, lb_ref,
                      conv_ref, q_ref, kt_ref, v_ref,
                      ext_ref, y_ref, w_ag_ref, w_q_ref, w_kt_ref, w_v_ref):
    tm = x_ref.shape[0]

    @pl.when(pl.program_id(0) == 0)
    def _():
        ext_ref[0:CONV_HALO, :] = jnp.zeros((CONV_HALO, CONV_CH), F32)
        q_lo, k_lo, v_lo = 2 * CONV_CH, 2 * CONV_CH + SB_DIM, 2 * CONV_CH + 2 * SB_DIM
        w_ag_ref[...] = w_in_ref[:, :q_lo].astype(BF16)
        w_q_ref[...] = w_in_ref[:, q_lo:k_lo].astype(BF16)
        w_kt_ref[...] = jnp.transpose(w_in_ref[:, k_lo:v_lo]).astype(BF16)
        w_v_ref[...] = w_in_ref[:, v_lo:].astype(BF16)

    xn = _rmsnorm(x_ref[...], g_ref[...]).astype(BF16)
    ag = _dot(xn, w_ag_ref[...])
    ext_ref[CONV_HALO:, :] = ag[:, :CONV_CH] * jax.nn.sigmoid(ag[:, CONV_CH:])
    q_ref[...] = (_dot(xn, w_q_ref[...]) * (SB_HEAD_DIM ** -0.5)).astype(BF16)
    kt_ref[...] = lax.dot_general(w_kt_ref[...], xn, (((1,), (1,)), ((), ())),
                                  preferred_element_type=F32).astype(BF16)
    v_ref[...] = _dot(xn, w_v_ref[...]).astype(BF16)

    first_tap = CONV_HALO - (CONV_WIDTH - 1)
    for r0 in range(0, tm, CONV_ROWS):
        for j in range(CONV_CH // LANES):
            lanes = slice(j * LANES, (j + 1) * LANES)
            win = ext_ref[r0:r0 + CONV_ROWS + CONV_HALO, lanes]
            acc = jnp.zeros((CONV_ROWS, LANES), F32) + cb_ref[:, lanes]
            for sub in range(SUBLANES):
                offsets = [o for o in range(first_tap, first_tap + CONV_WIDTH)
                           if o % SUBLANES == sub]
                shifted = pltpu.roll(win, win.shape[0] - sub, axis=0) if sub else win
                for o in offsets:
                    tap = cw_ref[o - first_tap:o - first_tap + 1, lanes]
                    acc = acc + shifted[o - sub:o - sub + CONV_ROWS, :] * tap
            y_ref[r0:r0 + CONV_ROWS, lanes] = acc
    y = _layernorm(y_ref[...], lg_ref[...], lb_ref[...])
    conv_ref[...] = (y * jax.nn.sigmoid(y)).astype(BF16)
    ext_ref[0:CONV_HALO, :] = ext_ref[tm:tm + CONV_HALO, :]


def _l0_inproj(x, g, w_in, cw, cb, lg, lb):
    tm = ROW_TILE
    row = lambda n: pl.BlockSpec((tm, n), lambda i: (i, 0))
    return pl.pallas_call(
        _l0_inproj_kernel,
        grid=(N_ROW_TILES,),
        in_specs=[row(D_MODEL), _full((1, D_MODEL)),
                  pl.BlockSpec((None, D_MODEL, 2 * CONV_CH + 3 * SB_DIM), lambda i: (0, 0, 0)),
                  _full((CONV_WIDTH, CONV_CH)), _full((1, CONV_CH)),
                  _full((1, CONV_CH)), _full((1, CONV_CH))],
        out_specs=[row(CONV_CH), row(SB_DIM), pl.BlockSpec((SB_DIM, tm), lambda i: (0, i)),
                   row(SB_DIM)],
        out_shape=[jax.ShapeDtypeStruct((SEQ, CONV_CH), BF16),
                   jax.ShapeDtypeStruct((SEQ, SB_DIM), BF16),
                   jax.ShapeDtypeStruct((SB_DIM, SEQ), BF16),
                   jax.ShapeDtypeStruct((SEQ, SB_DIM), BF16)],
        scratch_shapes=[pltpu.VMEM((tm + CONV_HALO, CONV_CH), F32),
                        pltpu.VMEM((tm, CONV_CH), F32),
                        pltpu.VMEM((D_MODEL, 2 * CONV_CH), BF16),
                        pltpu.VMEM((D_MODEL, SB_DIM), BF16),
                        pltpu.VMEM((SB_DIM, D_MODEL), BF16),
                        pltpu.VMEM((D_MODEL, SB_DIM), BF16)],
        compiler_params=_params("arbitrary"),
        name="l0_inproj_conv",
    )(x, g, w_in, cw, cb, lg, lb)


def _sb_kernel(q_ref, kt_ref, v_ref, o_ref, acc_ref, carry_ref):
    tq, tk = SB_TQ, SB_TK
    lane = lax.broadcasted_iota(jnp.int32, (tq, LANES), 1)

    def heads(q2):
        zero = jnp.zeros_like(q2)
        return (jnp.where(lane < SB_HEAD_DIM, q2, zero), jnp.where(lane >= SB_HEAD_DIM, q2, zero))

    q_blocks = [pl.program_id(1) * SB_SUB + b for b in range(SB_SUB)]
    q_heads = [heads(q_ref[b * tq:(b + 1) * tq, :]) for b in range(SB_SUB)]
    jj = lax.broadcasted_iota(jnp.int32, (tk, tk), 0)
    ss = lax.broadcasted_iota(jnp.int32, (tk, tk), 1)
    minus_later = jnp.where(jj > ss, -1.0, 0.0).astype(BF16)
    minus_later2 = jnp.concatenate([minus_later, minus_later], axis=0)

    def scores(q_head, kb, valid):
        z = _dot(q_head, kt_ref[:, pl.ds(pl.multiple_of(kb * tk, tk), tk)])
        return z if valid is None else jnp.where(valid, z, SB_HIDDEN_SCORE)

    def stay(z):
        sp = jnp.maximum(z, 0.0) + jnp.log(1.0 + jnp.exp(-jnp.abs(z)))
        hi = sp.astype(BF16)
        return sp, hi, (sp - hi.astype(F32)).astype(BF16)

    def later_sum(hi, lo):
        return _dot(jnp.concatenate([hi, lo], axis=1), minus_later2)

    def weighted_values(z, sp, after, kb):
        w = jnp.exp((z - sp) + after)
        return _dot(w.astype(BF16), v_ref[pl.ds(pl.multiple_of(kb * tk, tk), tk), :])

    def block(q_head, kb):
        z = scores(q_head, kb, None)
        sp, hi, lo = stay(z)
        pv = weighted_values(z, sp, later_sum(hi, lo), kb)
        return pv, jnp.sum(sp, axis=1, keepdims=True)

    row = lax.broadcasted_iota(jnp.int32, (tq, tk), 0)
    col = lax.broadcasted_iota(jnp.int32, (tq, tk), 1)
    chains = [(b, h, kb, valid) for b, qb in enumerate(q_blocks) for h in range(2)
              for kb, valid in ((qb, col < row), (jnp.maximum(qb - 1, 0), None))]
    zs = [scores(q_heads[b][h], kb, valid) for b, h, kb, valid in chains]
    stays = [stay(z) for z in zs]
    afters = [later_sum(hi, lo) for _, hi, lo in stays]
    pvs = [weighted_values(z, sp, after, kb)
           for z, (sp, _, _), after, (_, _, kb, _) in zip(zs, stays, afters, chains)]
    sums = [jnp.sum(sp, axis=1, keepdims=True) for sp, _, _ in stays]
    for b, qb in enumerate(q_blocks):
        for h in range(2):
            c = 4 * b + 2 * h
            (pv_diag, pv_prev), (sp_diag, sp_prev) = pvs[c:c + 2], sums[c:c + 2]
            acc_ref[b, h] = pv_diag + jnp.where(qb > 0, jnp.exp(-sp_diag), 0.0) * pv_prev
            carry_ref[b, h] = jnp.broadcast_to(-(sp_diag + sp_prev), (tq, LANES))

    for b, qb in enumerate(q_blocks):
        def alive():
            return jnp.max(carry_ref[b]) > SB_LOG_UNDERFLOW

        def cond(state):
            kb, go = state
            return (kb >= 0) & go

        def body(state):
            kb, _ = state
            for h in range(2):
                pv, sp_sum = block(q_heads[b][h], kb)
                carry = carry_ref[b, h]
                acc_ref[b, h] += jnp.exp(carry) * pv
                carry_ref[b, h] = carry - sp_sum
            return kb - 1, alive()

        lax.while_loop(cond, body, (qb - 2, alive()))
        o_ref[b * tq:(b + 1) * tq, :] = jnp.where(lane < SB_HEAD_DIM, acc_ref[b, 0],
                                                  acc_ref[b, 1]).astype(BF16)


def _sb_attention(q, kt, v):
    rows = SB_SUB * SB_TQ
    state = pltpu.VMEM((SB_SUB, 2, SB_TQ, LANES), F32)
    return pl.pallas_call(
        _sb_kernel,
        grid=(SB_DIM // LANES, SEQ // rows),
        in_specs=[pl.BlockSpec((rows, LANES), lambda p, i: (i, p)),
                  pl.BlockSpec((LANES, SEQ), lambda p, i: (p, 0)),
                  pl.BlockSpec((SEQ, LANES), lambda p, i: (0, p))],
        out_specs=pl.BlockSpec((rows, LANES), lambda p, i: (i, p)),
        out_shape=jax.ShapeDtypeStruct((SEQ, SB_DIM), BF16),
        scratch_shapes=[state, state],
        compiler_params=_params("parallel", "parallel"),
        name="sb_attention",
    )(q, kt, v)


def _router_gates(logits):
    col = lax.broadcasted_iota(jnp.int32, logits.shape, 1)
    colf = col.astype(F32)
    ninf = -jnp.inf
    first = lambda hit: jnp.min(jnp.where(hit, colf, float(LANES)), axis=1, keepdims=True)

    is_group = (col >= N_EXPERTS) & (col < N_EXPERTS + N_GROUPS)
    lg = jnp.where(is_group, logits, ninf)
    gmax = jnp.max(lg, axis=1, keepdims=True)
    pg_top = 1.0 / jnp.sum(jnp.exp(lg - gmax), axis=1, keepdims=True)
    g_idx = first(lg == gmax) - float(N_EXPERTS)

    in_group = (col < N_EXPERTS) & ((col // EXPERTS_PER_GROUP).astype(F32) == g_idx)
    le = jnp.where(in_group, logits, ninf)
    m1 = jnp.max(le, axis=1, keepdims=True)
    i1 = first(le == m1)
    le2 = jnp.where(colf == i1, ninf, le)
    m2 = jnp.max(le2, axis=1, keepdims=True)
    i2 = first(le2 == m2)
    ee = jnp.exp(le - m1)
    pe = ee / jnp.sum(ee, axis=1, keepdims=True)
    p1 = jnp.sum(jnp.where(colf == i1, pe, 0.0), axis=1, keepdims=True)
    p2 = jnp.sum(jnp.where(colf == i2, pe, 0.0), axis=1, keepdims=True)
    den = p1 + p2
    gates = jnp.where(colf == i1, p1 / den * pg_top,
                      jnp.where(colf == i2, p2 / den * pg_top, 0.0))
    return gates, g_idx


def _local_sort(g_idx):
    tm = g_idx.shape[0]
    lane = lax.broadcasted_iota(jnp.int32, (tm, LANES), 1)
    member = jnp.where(lane.astype(F32) == g_idx, 1.0, 0.0)
    r_i = lax.broadcasted_iota(jnp.int32, (tm, tm), 0)
    c_i = lax.broadcasted_iota(jnp.int32, (tm, tm), 1)
    before = jnp.where(c_i < r_i, 1.0, 0.0).astype(BF16)
    rank = jnp.sum(member * _dot(before, member.astype(BF16)), axis=1, keepdims=True)
    count = jnp.sum(member, axis=0, keepdims=True)
    padded = jnp.floor((count + (MOE_UNIT - 1)) * (1.0 / MOE_UNIT)) * MOE_UNIT
    lane1 = lax.broadcasted_iota(jnp.int32, (1, LANES), 1)
    offset = jnp.zeros((1, LANES), F32)
    start = jnp.zeros((1, 1), F32)
    for g in range(N_GROUPS):
        offset = offset + jnp.where((lane1 == g) | (lane1 == N_GROUPS + g), start, 0.0)
        start = start + jnp.sum(jnp.where(lane1 == g, padded, 0.0), axis=1, keepdims=True)
    dest = jnp.sum(member * offset, axis=1, keepdims=True) + rank
    rows = jnp.where(lane1 == N_GROUPS - 1, LS_TILE - offset, padded)
    units = jnp.where(lane1 < N_GROUPS, rows, offset) * (1.0 / MOE_UNIT)
    return dest, units.astype(jnp.int32)


def _pack_gates(gates):
    hi = gates.astype(BF16).astype(F32)
    rest = gates - hi
    mid = rest.astype(BF16).astype(F32)
    lo = rest - mid
    packed = hi + pltpu.roll(mid, GATE_TERM_STRIDE, axis=1) + pltpu.roll(lo, 2 * GATE_TERM_STRIDE, axis=1)
    return packed.astype(BF16)


def _unpack_gate(packed, expert):
    lane = lax.broadcasted_iota(jnp.int32, packed.shape, 1)
    terms = jnp.where(lane % GATE_TERM_STRIDE == expert, packed.astype(F32), 0.0)
    return jnp.sum(terms, axis=1, keepdims=True)


def _out_xattn_kernel(n_parts, *refs):
    h_ref = refs[0]
    part_refs = refs[1:1 + n_parts]
    w_refs = refs[1 + n_parts:1 + 2 * n_parts]
    (gx_ref, qk_ref, vo_ref, gf_ref, wr_ref, br_ref,
     h2_ref, xn_ls_ref, dest_ref, units_ref) = refs[1 + 2 * n_parts:]

    h1 = h_ref[...]
    for p_ref, w_ref in zip(part_refs, w_refs):
        h1 = h1 + _dot(p_ref[...], w_ref[...])

    xn = _rmsnorm(h1, gx_ref[...]).astype(BF16)
    s_all = _dot(xn, qk_ref[0])
    probs = []
    for hd in range(XA_HEADS):
        s = s_all[:, hd * MEM_LEN:(hd + 1) * MEM_LEN]
        e = jnp.exp(s - jnp.max(s, axis=1, keepdims=True))
        probs.append((e / jnp.sum(e, axis=1, keepdims=True)).astype(BF16))
    h2 = h1 + _dot(jnp.concatenate(probs, axis=1), vo_ref[0])
    h2_ref[...] = h2

    xf = _rmsnorm(h2, gf_ref[...])
    x_hi = xf.astype(BF16)
    x_lo = (xf - x_hi.astype(F32)).astype(BF16)
    both = _dot(x_hi, wr_ref[...])
    logits = both[:, :LANES] + both[:, LANES:] + _dot(x_lo, wr_ref[:, :LANES]) + br_ref[...]

    gates, g_idx = _router_gates(logits)
    dest, units = _local_sort(g_idx)
    tm = dest.shape[0]
    dest_ref[...] = jnp.broadcast_to(dest, (tm, LANES))
    units_ref[0] = jnp.broadcast_to(units, (SUBLANES, LANES))
    dest_row = jnp.transpose(jnp.broadcast_to(dest, (tm, LANES)))[0:1, :]
    slot = lax.broadcasted_iota(jnp.int32, (LS_TILE, tm), 0).astype(F32)
    place = jnp.where(slot == dest_row, 1.0, 0.0).astype(BF16)
    routed = jnp.concatenate([x_hi, _pack_gates(gates)], axis=1)
    xn_ls_ref[...] = _dot(place, routed).astype(BF16)


def _out_xattn(h, parts, weights, gx, qk_all, vo_all, gf, wr, br, layer):
    tm = ROW_TILE
    row = lambda n, rows=tm: pl.BlockSpec((rows, n), lambda i: (i, 0))
    qk_spec = pl.BlockSpec((1, D_MODEL, XA_HEADS * MEM_LEN), lambda i: (layer, 0, 0))
    vo_spec = pl.BlockSpec((1, XA_HEADS * MEM_LEN, D_MODEL), lambda i: (layer, 0, 0))
    in_specs = ([row(D_MODEL)] + [row(p.shape[1]) for p in parts]
                + [_full(w.shape) for w in weights]
                + [_full((1, D_MODEL)), qk_spec, vo_spec, _full((1, D_MODEL)),
                   _full((D_MODEL, 2 * LANES)), _full((1, LANES))])
    return pl.pallas_call(
        functools.partial(_out_xattn_kernel, len(parts)),
        grid=(N_ROW_TILES,),
        in_specs=in_specs,
        out_specs=[row(D_MODEL), row(LS_WIDTH, LS_TILE), row(LANES),
                   pl.BlockSpec((1, SUBLANES, LANES), lambda i: (i, 0, 0))],
        out_shape=[jax.ShapeDtypeStruct((SEQ, D_MODEL), F32),
                   jax.ShapeDtypeStruct((N_ROW_TILES * LS_TILE, LS_WIDTH), BF16),
                   jax.ShapeDtypeStruct((SEQ, LANES), F32),
                   jax.ShapeDtypeStruct((N_ROW_TILES, SUBLANES, LANES), jnp.int32)],
        compiler_params=_params("parallel"),
        name="out_xattn_router",
    )(h, *parts, *weights, gx, qk_all, vo_all, gf, wr, br)


def _route_index_kernel(units_ref, unit_of_ref, tile_group_ref):
    def no_unit(k, _):
        unit_of_ref[k] = -1
        return 0

    def no_group(j, _):
        tile_group_ref[j] = -1
        return 0

    pos = 0
    for g in range(N_GROUPS):
        first_tile = pos // SORT_UNITS

        def per_row_tile(i, pos):
            n = units_ref[i, g]
            base = i * LS_UNITS + units_ref[i, N_GROUPS + g]

            def per_unit(k, _):
                unit_of_ref[pos + k] = base + k
                return 0

            lax.fori_loop(0, n, per_unit, 0)
            return pos + n

        pos = lax.fori_loop(0, N_ROW_TILES, per_row_tile, pos)
        end_tile = (pos + SORT_UNITS - 1) // SORT_UNITS

        def mark_tile(j, _):
            tile_group_ref[j] = g
            return 0

        lax.fori_loop(first_tile, end_tile, mark_tile, 0)
        lax.fori_loop(pos, end_tile * SORT_UNITS, no_unit, 0)
        pos = end_tile * SORT_UNITS

    lax.fori_loop(pos, N_SORT_TILES * SORT_UNITS, no_unit, 0)
    lax.fori_loop(pos // SORT_UNITS, N_SORT_TILES, no_group, 0)


def _route_index(units):
    smem = lambda: pl.BlockSpec(memory_space=pltpu.SMEM)
    return pl.pallas_call(
        _route_index_kernel,
        in_specs=[smem()],
        out_specs=[smem(), smem()],
        out_shape=[jax.ShapeDtypeStruct((N_SORT_TILES * SORT_UNITS,), jnp.int32),
                   jax.ShapeDtypeStruct((N_SORT_TILES,), jnp.int32)],
        name="moe_route_index",
    )(units)


def _moe_kernel(unit_of_ref, tile_group_ref, xn_hbm, wg_ref, wu_ref, wd_ref, y_hbm,
                xbuf, ybuf, acc_ref, wg_bf, wu_bf, wd_bf, gather_sem, scatter_sem):
    j = pl.program_id(0)
    n_tiles = pl.num_programs(0)
    slot = j % 2

    def unit_rows(k):
        return pl.ds(pl.multiple_of(k * MOE_UNIT, MOE_UNIT), MOE_UNIT)

    def gather_copy(s, k, u):
        return pltpu.make_async_copy(xn_hbm.at[unit_rows(u), :], xbuf.at[s, unit_rows(k), :],
                                     gather_sem.at[s])

    def scatter_copy(s, k, u):
        return pltpu.make_async_copy(ybuf.at[s, unit_rows(k), :], y_hbm.at[unit_rows(u), :],
                                     scatter_sem.at[s])

    def is_full(tile):
        return unit_of_ref[tile * SORT_UNITS + SORT_UNITS - 1] >= 0

    def for_units(tile, full_fn, partial_fn):
        def run(fn, unroll):
            def unit(k, _):
                fn(k, unit_of_ref[tile * SORT_UNITS + k])
                return 0

            lax.fori_loop(0, SORT_UNITS, unit, 0, unroll=unroll)

        @pl.when(is_full(tile))
        def _():
            run(full_fn, 8)

        @pl.when(jnp.logical_not(is_full(tile)))
        def _():
            def guarded(k, u):
                @pl.when(u >= 0)
                def _():
                    full_fn(k, u)

                if partial_fn is not None:
                    @pl.when(u < 0)
                    def _():
                        partial_fn(k)

            run(guarded, 1)

    def start_gather(tile, s):
        def zero_fill(k):
            xbuf[s, unit_rows(k), :] = jnp.zeros((MOE_UNIT, LS_WIDTH), BF16)

        for_units(tile, lambda k, u: gather_copy(s, k, u).start(), zero_fill)

    def start_scatter(tile, s):
        for_units(tile, lambda k, u: scatter_copy(s, k, u).start(), None)

    def wait_units(tile, whole_copy, unit_copy):
        @pl.when(is_full(tile))
        def _():
            whole_copy.wait()

        @pl.when(jnp.logical_not(is_full(tile)))
        def _():
            def unit(k, _):
                u = unit_of_ref[tile * SORT_UNITS + k]

                @pl.when(u >= 0)
                def _():
                    unit_copy(k, u).wait()

                return 0

            lax.fori_loop(0, SORT_UNITS, unit, 0)

    def wait_gather(tile, s):
        whole = pltpu.make_async_copy(xn_hbm.at[pl.ds(0, ROW_TILE), :], xbuf.at[s], gather_sem.at[s])
        wait_units(tile, whole, lambda k, u: gather_copy(s, k, u))

    def wait_scatter(tile, s):
        whole = pltpu.make_async_copy(ybuf.at[s], y_hbm.at[pl.ds(0, ROW_TILE), :], scatter_sem.at[s])
        wait_units(tile, whole, lambda k, u: scatter_copy(s, k, u))

    @pl.when(j == 0)
    def _():
        start_gather(0, 0)

    @pl.when(j + 1 < n_tiles)
    def _():
        start_gather(j + 1, 1 - slot)

    grp = tile_group_ref[j]
    new_group = (j == 0) | (grp != tile_group_ref[jnp.maximum(j - 1, 0)])

    @pl.when((grp >= 0) & new_group)
    def _():
        wg_bf[...] = wg_ref[...].astype(BF16)
        wu_bf[...] = wu_ref[...].astype(BF16)
        wd_bf[...] = wd_ref[...].astype(BF16)

    wait_gather(j, slot)

    @pl.when(j >= 2)
    def _():
        wait_scatter(j - 2, slot)

    @pl.when(grp >= 0)
    def _():
        xn = xbuf[slot, :, :D_MODEL]
        packed_gates = xbuf[slot, :, D_MODEL:]
        for e in range(EXPERTS_PER_GROUP):
            gate = _unpack_gate(packed_gates, grp * EXPERTS_PER_GROUP + e)
            hg = _dot(xn, wg_bf[e])
            hu = _dot(xn, wu_bf[e])
            act = (hg * jax.nn.sigmoid(hg)) * hu * gate
            out = _dot(act.astype(BF16), wd_bf[e])
            if e == 0:
                acc_ref[...] = out
            else:
                acc_ref[...] += out
        y = acc_ref[...]
        y_hi = y.astype(BF16)
        ybuf[slot, :, :D_MODEL] = y_hi
        ybuf[slot, :, D_MODEL:] = (y - y_hi.astype(F32)).astype(BF16)

    start_scatter(j, slot)

    @pl.when(j == n_tiles - 1)
    def _():
        wait_scatter(j - 1, 1 - slot)
        wait_scatter(j, slot)


def _moe(unit_of, tile_group, xn_ls, wg, wu, wd, layer):
    group_of = lambda j, unit_of, tile_group: jnp.where(tile_group[j] < 0, N_GROUPS - 1,
                                                        tile_group[j])
    up_spec = pl.BlockSpec((None, EXPERTS_PER_GROUP, D_MODEL, D_EXPERT),
                           lambda j, u, t: (layer, group_of(j, u, t), 0, 0))
    down_spec = pl.BlockSpec((None, EXPERTS_PER_GROUP, D_EXPERT, D_MODEL),
                             lambda j, u, t: (layer, group_of(j, u, t), 0, 0))
    any_spec = pl.BlockSpec(memory_space=pl.ANY)
    return pl.pallas_call(
        _moe_kernel,
        grid_spec=pltpu.PrefetchScalarGridSpec(
            num_scalar_prefetch=2,
            grid=(N_SORT_TILES,),
            in_specs=[any_spec, up_spec, up_spec, down_spec],
            out_specs=any_spec,
            scratch_shapes=[pltpu.VMEM((2, ROW_TILE, LS_WIDTH), BF16),
                            pltpu.VMEM((2, ROW_TILE, 2 * D_MODEL), BF16),
                            pltpu.VMEM((ROW_TILE, D_MODEL), F32),
                            pltpu.VMEM((EXPERTS_PER_GROUP, D_MODEL, D_EXPERT), BF16),
                            pltpu.VMEM((EXPERTS_PER_GROUP, D_MODEL, D_EXPERT), BF16),
                            pltpu.VMEM((EXPERTS_PER_GROUP, D_EXPERT, D_MODEL), BF16),
                            pltpu.SemaphoreType.DMA((2,)),
                            pltpu.SemaphoreType.DMA((2,))]),
        out_shape=jax.ShapeDtypeStruct((N_ROW_TILES * LS_TILE, 2 * D_MODEL), BF16),
        compiler_params=_params("arbitrary"),
        name="moe_experts",
    )(unit_of, tile_group, xn_ls, wg, wu, wd)


def _add_unsorted(h_ref, y_ls_ref, dest_ref):
    tm = h_ref.shape[0]
    slot = lax.broadcasted_iota(jnp.int32, (tm, LS_TILE), 1).astype(F32)
    pick = jnp.where(slot == dest_ref[:, 0:1], 1.0, 0.0).astype(BF16)
    y = _dot(pick, y_ls_ref[...])
    return h_ref[...] + (y[:, :D_MODEL] + y[:, D_MODEL:])


def _unsort_final_kernel(h_ref, y_ls_ref, dest_ref, gfin_ref, o_ref):
    o_ref[...] = _rmsnorm(_add_unsorted(h_ref, y_ls_ref, dest_ref), gfin_ref[...])


def _unsort_specs():
    row = lambda n: pl.BlockSpec((ROW_TILE, n), lambda i: (i, 0))
    return [row(D_MODEL), pl.BlockSpec((LS_TILE, 2 * D_MODEL), lambda i: (i, 0)), row(LANES)]


def _unsort_final(h, y_ls, dest, gfin):
    return pl.pallas_call(
        _unsort_final_kernel,
        grid=(N_ROW_TILES,),
        in_specs=_unsort_specs() + [_full((1, D_MODEL))],
        out_specs=pl.BlockSpec((ROW_TILE, D_MODEL), lambda i: (i, 0)),
        out_shape=jax.ShapeDtypeStruct((SEQ, D_MODEL), F32),
        compiler_params=_params("parallel"),
        name="moe_unsort_final_norm",
    )(h, y_ls, dest, gfin)


def _l1_mixer_kernel(h_ref, y_ls_ref, dest_ref, g_ref, w_in_ref, vg_ref, vb_ref, ws_ref, bs_ref,
                     x_ref, o_ref, w_in_bf):
    tm = h_ref.shape[0]

    @pl.when(pl.program_id(0) == 0)
    def _():
        w_in_bf[...] = w_in_ref[...].astype(BF16)

    x = _add_unsorted(h_ref, y_ls_ref, dest_ref)
    x_ref[...] = x
    xn = _rmsnorm(x, g_ref[...]).astype(BF16)
    p = _dot(xn, w_in_bf[...])
    p = 0.5 * p * (1.0 + lax.erf(p * (2.0 ** -0.5)))
    u = p[:, :D_MODEL]
    v = _layernorm(p[:, D_MODEL:], vg_ref[...], vb_ref[...]).astype(BF16)
    t_idx = lax.broadcasted_iota(jnp.int32, (GM_CHUNK, GM_CHUNK), 0)
    s_idx = lax.broadcasted_iota(jnp.int32, (GM_CHUNK, GM_CHUNK), 1)
    for g in range(GM_GROUPS):
        cols = slice(g * LANES, (g + 1) * LANES)
        wmix = jnp.where(t_idx >= s_idx, ws_ref[g], 0.0).astype(BF16)
        for c in range(tm // GM_CHUNK):
            rows = slice(c * GM_CHUNK, (c + 1) * GM_CHUNK)
            mixed = _dot(wmix, v[rows, cols]) + bs_ref[g]
            o_ref[rows, cols] = (u[rows, cols] * mixed).astype(BF16)


def _l1_mixer(h, y_ls, dest, g, w_in, vg, vb, ws, bs_b):
    row = pl.BlockSpec((ROW_TILE, D_MODEL), lambda i: (i, 0))
    return pl.pallas_call(
        _l1_mixer_kernel,
        grid=(N_ROW_TILES,),
        in_specs=_unsort_specs() + [
            _full((1, D_MODEL)),
            pl.BlockSpec((None, D_MODEL, 2 * D_MODEL), lambda i: (0, 0, 0)),
            _full((1, D_MODEL)), _full((1, D_MODEL)), _full((GM_GROUPS, GM_CHUNK, GM_CHUNK)),
            _full((GM_GROUPS, GM_CHUNK, LANES))],
        out_specs=[row, row],
        out_shape=[jax.ShapeDtypeStruct((SEQ, D_MODEL), F32),
                   jax.ShapeDtypeStruct((SEQ, D_MODEL), BF16)],
        scratch_shapes=[pltpu.VMEM((D_MODEL, 2 * D_MODEL), BF16)],
        compiler_params=_params("arbitrary"),
        name="l1_mixer",
    )(h, y_ls, dest, g, w_in, vg, vb, ws, bs_b)


def _router_weights(we, wg, be, bg):
    pad = LANES - N_EXPERTS - N_GROUPS
    w = jnp.concatenate([we, wg, jnp.zeros((D_MODEL, pad), F32)], axis=1)
    b = jnp.concatenate([be, bg, jnp.zeros((pad,), F32)])[None, :]
    w_hi = w.astype(BF16)
    w_lo = (w - w_hi.astype(F32)).astype(BF16)
    return jnp.concatenate([w_hi, w_lo], axis=1), b


def kernel(x, mem, mem_norm_g, norm_mix, norm_xa, norm_ffn, final_norm_g, ab_w_in, ab_conv_w, ab_conv_b, ab_cnorm_g, ab_cnorm_b, ab_w_out, c_w_in, c_vnorm_g, c_vnorm_b, c_ws, c_bs, c_w_out, xa_wq, xa_wk, xa_wv, xa_wo, rt_wg, rt_bg, rt_we, rt_be, ex_w_gate, ex_w_up, ex_w_down):
    assert x.shape == (1, SEQ, D_MODEL) and mem.shape == (1, MEM_LEN, D_MODEL)
    bf = lambda a: a.astype(BF16)
    r1 = lambda a: a.reshape(1, -1)
    h = x[0]

    memory_len = XA_HEADS * MEM_LEN
    qk_all = _mem_weights(_mem_qk_kernel, "mem_qk", mem[0], r1(mem_norm_g), xa_wk, xa_wq,
                          D_MODEL, memory_len)
    vo_all = _mem_weights(_mem_vo_kernel, "mem_vo", mem[0], r1(mem_norm_g), xa_wv, xa_wo,
                          memory_len, D_MODEL)

    def tail(h, parts, weights, i):
        wr, br = _router_weights(rt_we[i], rt_wg[i], rt_be[i], rt_bg[i])
        h2, xn_ls, dest, units = _out_xattn(
            h, parts, weights, r1(norm_xa[i]), qk_all, vo_all, r1(norm_ffn[i]), wr, br, i)
        unit_of, tile_group = _route_index(units[:, 0, :])
        y_ls = _moe(unit_of, tile_group, xn_ls, ex_w_gate, ex_w_up, ex_w_down, i)
        return h2, y_ls, dest

    conv_out, q, kt, v = _l0_inproj(
        h, r1(norm_mix[0]), ab_w_in, ab_conv_w[0], r1(ab_conv_b[0]), r1(ab_cnorm_g[0]),
        r1(ab_cnorm_b[0]))
    sb_out = _sb_attention(q, kt, v)
    w_out = bf(ab_w_out[0])
    h2, y_ls, dest = tail(h, [conv_out, sb_out], [w_out[:CONV_CH], w_out[CONV_CH:]], 0)

    bs_b = jnp.broadcast_to(c_bs[0][:, :, None], (GM_GROUPS, GM_CHUNK, LANES))
    h, gated = _l1_mixer(h2, y_ls, dest, r1(norm_mix[1]), c_w_in, r1(c_vnorm_g[0]),
                         r1(c_vnorm_b[0]), c_ws[0], bs_b)
    h2, y_ls, dest = tail(h, [gated], [bf(c_w_out[0])], 1)
    return _unsort_final(h2, y_ls, dest, r1(final_norm_g))[None]
```

```python
import functools

import jax
import jax.numpy as jnp
from jax import lax
from jax.experimental import pallas as pl
from jax.experimental.pallas import tpu as pltpu

D_MODEL = 1024
SEQ = 16384
MEM_LEN = 256
EPS = 1e-6
CONV_CH = 512
CONV_WIDTH = 31
SB_HEADS = 8
SB_HEAD_DIM = 64
SB_DIM = 512
GM_GROUPS = 8
GM_CHUNK = 128
XA_HEADS = 4
XA_HEAD_DIM = 256
N_GROUPS = 4
EXPERTS_PER_GROUP = 4
N_EXPERTS = 16
D_EXPERT = 256

LANES = 128
SUBLANES = 8
ROW_TILE = 512
SB_TQ = 256
SB_TK = 256
SB_SUB = 2
SB_LOG_UNDERFLOW = -104.0
SB_HIDDEN_SCORE = -1e30
CONV_HALO = 32
CONV_ROWS = 64
MOE_UNIT = 16
N_ROW_TILES = SEQ // ROW_TILE
LS_TILE = ROW_TILE + N_GROUPS * MOE_UNIT
LS_UNITS = LS_TILE // MOE_UNIT
SORT_UNITS = ROW_TILE // MOE_UNIT
N_SORT_TILES = N_ROW_TILES * LS_UNITS // SORT_UNITS + N_GROUPS
GATE_TERM_STRIDE = 32
LS_WIDTH = D_MODEL + LANES
UNSORT_SLOTS = 3
VMEM_LIMIT = 56 * 1024 * 1024

BF16 = jnp.bfloat16
F32 = jnp.float32


def _params(*semantics):
    return pltpu.CompilerParams(dimension_semantics=semantics, vmem_limit_bytes=VMEM_LIMIT)


def _dot(a, b):
    return jnp.dot(a, b, preferred_element_type=F32)


def _rmsnorm(x, g):
    return x * lax.rsqrt(jnp.mean(x * x, axis=-1, keepdims=True) + EPS) * g


def _layernorm(x, g, b):
    mu = jnp.mean(x, axis=-1, keepdims=True)
    xc = x - mu
    var = jnp.mean(xc * xc, axis=-1, keepdims=True)
    return xc * lax.rsqrt(var + EPS) * g + b


def _full(shape):
    return pl.BlockSpec(shape, lambda *_: (0,) * len(shape))


def _mem_qk_kernel(mem_ref, g_ref, wk_ref, wq_ref, qk_ref):
    memn = _rmsnorm(mem_ref[...], g_ref[...]).astype(BF16)
    k = _dot(memn, wk_ref[0].astype(BF16)).astype(BF16)
    wq = wq_ref[0].astype(BF16)
    for hd in range(XA_HEADS):
        sl = slice(hd * XA_HEAD_DIM, (hd + 1) * XA_HEAD_DIM)
        qk = lax.dot_general(wq[:, sl], k[:, sl], (((1,), (1,)), ((), ())),
                             preferred_element_type=F32)
        qk_ref[0, :, hd * MEM_LEN:(hd + 1) * MEM_LEN] = (qk * (XA_HEAD_DIM ** -0.5)).astype(BF16)


def _mem_vo_kernel(mem_ref, g_ref, wv_ref, wo_ref, vo_ref):
    memn = _rmsnorm(mem_ref[...], g_ref[...]).astype(BF16)
    v = _dot(memn, wv_ref[0].astype(BF16)).astype(BF16)
    wo = wo_ref[0].astype(BF16)
    for hd in range(XA_HEADS):
        sl = slice(hd * XA_HEAD_DIM, (hd + 1) * XA_HEAD_DIM)
        vo_ref[0, hd * MEM_LEN:(hd + 1) * MEM_LEN, :] = _dot(v[:, sl], wo[sl, :]).astype(BF16)


def _mem_weights(kernel_fn, name, mem, g, w_mem, w_attn, out_rows, out_cols):
    depth = w_mem.shape[0]
    wspec = pl.BlockSpec((1, D_MODEL, D_MODEL), lambda i: (i, 0, 0))
    return pl.pallas_call(
        kernel_fn,
        grid=(depth,),
        in_specs=[_full((MEM_LEN, D_MODEL)), _full((1, D_MODEL)), wspec, wspec],
        out_specs=pl.BlockSpec((1, out_rows, out_cols), lambda i: (i, 0, 0)),
        out_shape=jax.ShapeDtypeStruct((depth, out_rows, out_cols), BF16),
        compiler_params=_params("arbitrary"),
        name=name,
    )(mem, g, w_mem, w_attn)


def _l0_inproj_kernel(x_ref, g_ref, w_in_ref, cw_ref, cb_ref, lg_ref, lb_ref,
                      conv_ref, q_ref, kt_ref, v_ref,
                      ext_ref, y_ref, w_ag_ref, w_q_ref, w_kt_ref, w_v_ref):
    tm = x_ref.shape[0]

    @pl.when(pl.program_id(0) == 0)
    def _():
        ext_ref[0:CONV_HALO, :] = jnp.zeros((CONV_HALO, CONV_CH), F32)
        q_lo, k_lo, v_lo = 2 * CONV_CH, 2 * CONV_CH + SB_DIM, 2 * CONV_CH + 2 * SB_DIM
        w_ag_ref[...] = w_in_ref[:, :q_lo].astype(BF16)
        w_q_ref[...] = w_in_ref[:, q_lo:k_lo].astype(BF16)
        w_kt_ref[...] = jnp.transpose(w_in_ref[:, k_lo:v_lo]).astype(BF16)
        w_v_ref[...] = w_in_ref[:, v_lo:].astype(BF16)

    xn = _rmsnorm(x_ref[...], g_ref[...]).astype(BF16)
    ag = _dot(xn, w_ag_ref[...])
    ext_ref[CONV_HALO:, :] = ag[:, :CONV_CH] * jax.nn.sigmoid(ag[:, CONV_CH:])
    q_ref[...] = (_dot(xn, w_q_ref[...]) * (SB_HEAD_DIM ** -0.5)).astype(BF16)
    kt_ref[...] = lax.dot_general(w_kt_ref[...], xn, (((1,), (1,)), ((), ())),
                                  preferred_element_type=F32).astype(BF16)
    v_ref[...] = _dot(xn, w_v_ref[...]).astype(BF16)

    first_tap = CONV_HALO - (CONV_WIDTH - 1)
    for r0 in range(0, tm, CONV_ROWS):
        for j in range(CONV_CH // LANES):
            lanes = slice(j * LANES, (j + 1) * LANES)
            win = ext_ref[r0:r0 + CONV_ROWS + CONV_HALO, lanes]
            acc = jnp.zeros((CONV_ROWS, LANES), F32) + cb_ref[:, lanes]
            for sub in range(SUBLANES):
                offsets = [o for o in range(first_tap, first_tap + CONV_WIDTH)
                           if o % SUBLANES == sub]
                shifted = pltpu.roll(win, win.shape[0] - sub, axis=0) if sub else win
                for o in offsets:
                    tap = cw_ref[o - first_tap:o - first_tap + 1, lanes]
                    acc = acc + shifted[o - sub:o - sub + CONV_ROWS, :] * tap
            y_ref[r0:r0 + CONV_ROWS, lanes] = acc
    y = _layernorm(y_ref[...], lg_ref[...], lb_ref[...])
    conv_ref[...] = (y * jax.nn.sigmoid(y)).astype(BF16)
    ext_ref[0:CONV_HALO, :] = ext_ref[tm:tm + CONV_HALO, :]


def _l0_inproj(x, g, w_in, cw, cb, lg, lb):
    tm = ROW_TILE
    row = lambda n: pl.BlockSpec((tm, n), lambda i: (i, 0))
    return pl.pallas_call(
        _l0_inproj_kernel,
        grid=(N_ROW_TILES,),
        in_specs=[row(D_MODEL), _full((1, D_MODEL)),
                  pl.BlockSpec((None, D_MODEL, 2 * CONV_CH + 3 * SB_DIM), lambda i: (0, 0, 0)),
                  _full((CONV_WIDTH, CONV_CH)), _full((1, CONV_CH)),
                  _full((1, CONV_CH)), _full((1, CONV_CH))],
        out_specs=[row(CONV_CH), row(SB_DIM), pl.BlockSpec((SB_DIM, tm), lambda i: (0, i)),
                   row(SB_DIM)],
        out_shape=[jax.ShapeDtypeStruct((SEQ, CONV_CH), BF16),
                   jax.ShapeDtypeStruct((SEQ, SB_DIM), BF16),
                   jax.ShapeDtypeStruct((SB_DIM, SEQ), BF16),
                   jax.ShapeDtypeStruct((SEQ, SB_DIM), BF16)],
        scratch_shapes=[pltpu.VMEM((tm + CONV_HALO, CONV_CH), F32),
                        pltpu.VMEM((tm, CONV_CH), F32),
                        pltpu.VMEM((D_MODEL, 2 * CONV_CH), BF16),
                        pltpu.VMEM((D_MODEL, SB_DIM), BF16),
                        pltpu.VMEM((SB_DIM, D_MODEL), BF16),
                        pltpu.VMEM((D_MODEL, SB_DIM), BF16)],
        compiler_params=_params("arbitrary"),
        name="l0_inproj_conv",
    )(x, g, w_in, cw, cb, lg, lb)


def _sb_kernel(q_ref, kt_ref, v_ref, o_ref, acc_ref, carry_ref):
    tq, tk = SB_TQ, SB_TK
    lane = lax.broadcasted_iota(jnp.int32, (tq, LANES), 1)

    def heads(q2):
        zero = jnp.zeros_like(q2)
        return (jnp.where(lane < SB_HEAD_DIM, q2, zero), jnp.where(lane >= SB_HEAD_DIM, q2, zero))

    q_blocks = [pl.program_id(1) * SB_SUB + b for b in range(SB_SUB)]
    q_heads = [heads(q_ref[b * tq:(b + 1) * tq, :]) for b in range(SB_SUB)]
    jj = lax.broadcasted_iota(jnp.int32, (tk, tk), 0)
    ss = lax.broadcasted_iota(jnp.int32, (tk, tk), 1)
    minus_later = jnp.where(jj > ss, -1.0, 0.0).astype(BF16)
    minus_later2 = jnp.concatenate([minus_later, minus_later], axis=0)

    def scores(q_head, kb, valid):
        z = _dot(q_head, kt_ref[:, pl.ds(pl.multiple_of(kb * tk, tk), tk)])
        return z if valid is None else jnp.where(valid, z, SB_HIDDEN_SCORE)

    def stay(z):
        sp = jnp.maximum(z, 0.0) + jnp.log(1.0 + jnp.exp(-jnp.abs(z)))
        hi = sp.astype(BF16)
        return sp, hi, (sp - hi.astype(F32)).astype(BF16)

    def later_sum(hi, lo):
        return _dot(jnp.concatenate([hi, lo], axis=1), minus_later2)

    def weighted_values(z, sp, after, kb):
        w = jnp.exp((z - sp) + after)
        return _dot(w.astype(BF16), v_ref[pl.ds(pl.multiple_of(kb * tk, tk), tk), :])

    def block(q_head, kb):
        z = scores(q_head, kb, None)
        sp, hi, lo = stay(z)
        pv = weighted_values(z, sp, later_sum(hi, lo), kb)
        return pv, jnp.sum(sp, axis=1, keepdims=True)

    row = lax.broadcasted_iota(jnp.int32, (tq, tk), 0)
    col = lax.broadcasted_iota(jnp.int32, (tq, tk), 1)
    chains = [(b, h, kb, valid) for b, qb in enumerate(q_blocks) for h in range(2)
              for kb, valid in ((qb, col < row), (jnp.maximum(qb - 1, 0), None))]
    zs = [scores(q_heads[b][h], kb, valid) for b, h, kb, valid in chains]
    stays = [stay(z) for z in zs]
    afters = [later_sum(hi, lo) for _, hi, lo in stays]
    pvs = [weighted_values(z, sp, after, kb)
           for z, (sp, _, _), after, (_, _, kb, _) in zip(zs, stays, afters, chains)]
    sums = [jnp.sum(sp, axis=1, keepdims=True) for sp, _, _ in stays]
    for b, qb in enumerate(q_blocks):
        for h in range(2):
            c = 4 * b + 2 * h
            (pv_diag, pv_prev), (sp_diag, sp_prev) = pvs[c:c + 2], sums[c:c + 2]
            acc_ref[b, h] = pv_diag + jnp.where(qb > 0, jnp.exp(-sp_diag), 0.0) * pv_prev
            carry_ref[b, h] = jnp.broadcast_to(-(sp_diag + sp_prev), (tq, LANES))

    for b, qb in enumerate(q_blocks):
        def alive():
            return jnp.max(carry_ref[b]) > SB_LOG_UNDERFLOW

        def cond(state):
            kb, go = state
            return (kb >= 0) & go

        def body(state):
            kb, _ = state
            for h in range(2):
                pv, sp_sum = block(q_heads[b][h], kb)
                carry = carry_ref[b, h]
                acc_ref[b, h] += jnp.exp(carry) * pv
                carry_ref[b, h] = carry - sp_sum
            return kb - 1, alive()

        lax.while_loop(cond, body, (qb - 2, alive()))
        o_ref[b * tq:(b + 1) * tq, :] = jnp.where(lane < SB_HEAD_DIM, acc_ref[b, 0],
                                                  acc_ref[b, 1]).astype(BF16)


def _sb_attention(q, kt, v):
    rows = SB_SUB * SB_TQ
    state = pltpu.VMEM((SB_SUB, 2, SB_TQ, LANES), F32)
    return pl.pallas_call(
        _sb_kernel,
        grid=(SB_DIM // LANES, SEQ // rows),
        in_specs=[pl.BlockSpec((rows, LANES), lambda p, i: (i, p)),
                  pl.BlockSpec((LANES, SEQ), lambda p, i: (p, 0)),
                  pl.BlockSpec((SEQ, LANES), lambda p, i: (0, p))],
        out_specs=pl.BlockSpec((rows, LANES), lambda p, i: (i, p)),
        out_shape=jax.ShapeDtypeStruct((SEQ, SB_DIM), BF16),
        scratch_shapes=[state, state],
        compiler_params=_params("parallel", "parallel"),
        name="sb_attention",
    )(q, kt, v)


def _router_gates(logits):
    col = lax.broadcasted_iota(jnp.int32, logits.shape, 1)
    colf = col.astype(F32)
    ninf = -jnp.inf
    first = lambda hit: jnp.min(jnp.where(hit, colf, float(LANES)), axis=1, keepdims=True)

    is_group = (col >= N_EXPERTS) & (col < N_EXPERTS + N_GROUPS)
    lg = jnp.where(is_group, logits, ninf)
    gmax = jnp.max(lg, axis=1, keepdims=True)
    pg_top = 1.0 / jnp.sum(jnp.exp(lg - gmax), axis=1, keepdims=True)
    g_idx = first(lg == gmax) - float(N_EXPERTS)

    in_group = (col < N_EXPERTS) & ((col // EXPERTS_PER_GROUP).astype(F32) == g_idx)
    le = jnp.where(in_group, logits, ninf)
    m1 = jnp.max(le, axis=1, keepdims=True)
    i1 = first(le == m1)
    le2 = jnp.where(colf == i1, ninf, le)
    m2 = jnp.max(le2, axis=1, keepdims=True)
    i2 = first(le2 == m2)
    ee = jnp.exp(le - m1)
    pe = ee / jnp.sum(ee, axis=1, keepdims=True)
    p1 = jnp.sum(jnp.where(colf == i1, pe, 0.0), axis=1, keepdims=True)
    p2 = jnp.sum(jnp.where(colf == i2, pe, 0.0), axis=1, keepdims=True)
    den = p1 + p2
    gates = jnp.where(colf == i1, p1 / den * pg_top,
                      jnp.where(colf == i2, p2 / den * pg_top, 0.0))
    return gates, g_idx


def _local_sort(g_idx):
    tm = g_idx.shape[0]
    lane = lax.broadcasted_iota(jnp.int32, (tm, LANES), 1)
    member = jnp.where(lane.astype(F32) == g_idx, 1.0, 0.0)
    r_i = lax.broadcasted_iota(jnp.int32, (tm, tm), 0)
    c_i = lax.broadcasted_iota(jnp.int32, (tm, tm), 1)
    before = jnp.where(c_i < r_i, 1.0, 0.0).astype(BF16)
    rank = jnp.sum(member * _dot(before, member.astype(BF16)), axis=1, keepdims=True)
    count = jnp.sum(member, axis=0, keepdims=True)
    padded = jnp.floor((count + (MOE_UNIT - 1)) * (1.0 / MOE_UNIT)) * MOE_UNIT
    lane1 = lax.broadcasted_iota(jnp.int32, (1, LANES), 1)
    offset = jnp.zeros((1, LANES), F32)
    start = jnp.zeros((1, 1), F32)
    for g in range(N_GROUPS):
        offset = offset + jnp.where((lane1 == g) | (lane1 == N_GROUPS + g), start, 0.0)
        start = start + jnp.sum(jnp.where(lane1 == g, padded, 0.0), axis=1, keepdims=True)
    dest = jnp.sum(member * offset, axis=1, keepdims=True) + rank
    rows = jnp.where(lane1 == N_GROUPS - 1, LS_TILE - offset, padded)
    units = jnp.where(lane1 < N_GROUPS, rows, offset) * (1.0 / MOE_UNIT)
    return dest, units.astype(jnp.int32)


def _pack_gates(gates):
    hi = gates.astype(BF16).astype(F32)
    rest = gates - hi
    mid = rest.astype(BF16).astype(F32)
    lo = rest - mid
    packed = hi + pltpu.roll(mid, GATE_TERM_STRIDE, axis=1) + pltpu.roll(lo, 2 * GATE_TERM_STRIDE, axis=1)
    return packed.astype(BF16)


def _unpack_gate(packed, expert):
    lane = lax.broadcasted_iota(jnp.int32, packed.shape, 1)
    terms = jnp.where(lane % GATE_TERM_STRIDE == expert, packed.astype(F32), 0.0)
    return jnp.sum(terms, axis=1, keepdims=True)


def _out_xattn_kernel(n_parts, *refs):
    h_ref = refs[0]
    part_refs = refs[1:1 + n_parts]
    w_refs = refs[1 + n_parts:1 + 2 * n_parts]
    (gx_ref, qk_ref, vo_ref, gf_ref, wr_ref, br_ref,
     h2_ref, xn_ls_ref, dest_ref, units_ref) = refs[1 + 2 * n_parts:]

    h1 = h_ref[...]
    for p_ref, w_ref in zip(part_refs, w_refs):
        h1 = h1 + _dot(p_ref[...], w_ref[...])

    xn = _rmsnorm(h1, gx_ref[...]).astype(BF16)
    s_all = _dot(xn, qk_ref[0])
    probs = []
    for hd in range(XA_HEADS):
        s = s_all[:, hd * MEM_LEN:(hd + 1) * MEM_LEN]
        e = jnp.exp(s - jnp.max(s, axis=1, keepdims=True))
        probs.append((e / jnp.sum(e, axis=1, keepdims=True)).astype(BF16))
    h2 = h1 + _dot(jnp.concatenate(probs, axis=1), vo_ref[0])
    h2_ref[...] = h2

    xf = _rmsnorm(h2, gf_ref[...])
    x_hi = xf.astype(BF16)
    x_lo = (xf - x_hi.astype(F32)).astype(BF16)
    both = _dot(x_hi, wr_ref[...])
    logits = both[:, :LANES] + both[:, LANES:] + _dot(x_lo, wr_ref[:, :LANES]) + br_ref[...]

    gates, g_idx = _router_gates(logits)
    dest, units = _local_sort(g_idx)
    tm = dest.shape[0]
    dest_ref[...] = jnp.broadcast_to(dest, (tm, LANES))
    units_ref[0] = jnp.broadcast_to(units, (SUBLANES, LANES))
    dest_row = jnp.transpose(jnp.broadcast_to(dest, (tm, LANES)))[0:1, :]
    slot = lax.broadcasted_iota(jnp.int32, (LS_TILE, tm), 0).astype(F32)
    place = jnp.where(slot == dest_row, 1.0, 0.0).astype(BF16)
    routed = jnp.concatenate([x_hi, _pack_gates(gates)], axis=1)
    xn_ls_ref[...] = _dot(place, routed).astype(BF16)


def _out_xattn(h, parts, weights, gx, qk_all, vo_all, gf, wr, br, layer):
    tm = ROW_TILE
    row = lambda n, rows=tm: pl.BlockSpec((rows, n), lambda i: (i, 0))
    qk_spec = pl.BlockSpec((1, D_MODEL, XA_HEADS * MEM_LEN), lambda i: (layer, 0, 0))
    vo_spec = pl.BlockSpec((1, XA_HEADS * MEM_LEN, D_MODEL), lambda i: (layer, 0, 0))
    in_specs = ([row(D_MODEL)] + [row(p.shape[1]) for p in parts]
                + [_full(w.shape) for w in weights]
                + [_full((1, D_MODEL)), qk_spec, vo_spec, _full((1, D_MODEL)),
                   _full((D_MODEL, 2 * LANES)), _full((1, LANES))])
    return pl.pallas_call(
        functools.partial(_out_xattn_kernel, len(parts)),
        grid=(N_ROW_TILES,),
        in_specs=in_specs,
        out_specs=[row(D_MODEL), row(LS_WIDTH, LS_TILE), row(LANES),
                   pl.BlockSpec((1, SUBLANES, LANES), lambda i: (i, 0, 0))],
        out_shape=[jax.ShapeDtypeStruct((SEQ, D_MODEL), F32),
                   jax.ShapeDtypeStruct((N_ROW_TILES * LS_TILE, LS_WIDTH), BF16),
                   jax.ShapeDtypeStruct((SEQ, LANES), F32),
                   jax.ShapeDtypeStruct((N_ROW_TILES, SUBLANES, LANES), jnp.int32)],
        compiler_params=_params("parallel"),
        name="out_xattn_router",
    )(h, *parts, *weights, gx, qk_all, vo_all, gf, wr, br)


def _route_index_kernel(units_ref, unit_of_ref, tile_group_ref):
    def no_unit(k, _):
        unit_of_ref[k] = -1
        return 0

    def no_group(j, _):
        tile_group_ref[j] = -1
        return 0

    pos = 0
    for g in range(N_GROUPS):
        first_tile = pos // SORT_UNITS

        def per_row_tile(i, pos):
            n = units_ref[i, g]
            base = i * LS_UNITS + units_ref[i, N_GROUPS + g]

            def per_unit(k, _):
                unit_of_ref[pos + k] = base + k
                return 0

            lax.fori_loop(0, n, per_unit, 0)
            return pos + n

        pos = lax.fori_loop(0, N_ROW_TILES, per_row_tile, pos)
        end_tile = (pos + SORT_UNITS - 1) // SORT_UNITS

        def mark_tile(j, _):
            tile_group_ref[j] = g
            return 0

        lax.fori_loop(first_tile, end_tile, mark_tile, 0)
        lax.fori_loop(pos, end_tile * SORT_UNITS, no_unit, 0)
        pos = end_tile * SORT_UNITS

    lax.fori_loop(pos, N_SORT_TILES * SORT_UNITS, no_unit, 0)
    lax.fori_loop(pos // SORT_UNITS, N_SORT_TILES, no_group, 0)


def _route_index(units):
    smem = lambda: pl.BlockSpec(memory_space=pltpu.SMEM)
    return pl.pallas_call(
        _route_index_kernel,
        in_specs=[smem()],
        out_specs=[smem(), smem()],
        out_shape=[jax.ShapeDtypeStruct((N_SORT_TILES * SORT_UNITS,), jnp.int32),
                   jax.ShapeDtypeStruct((N_SORT_TILES,), jnp.int32)],
        name="moe_route_index",
    )(units)


def _moe_kernel(unit_of_ref, tile_group_ref, xn_hbm, wg_ref, wu_ref, wd_ref, y_hbm,
                xbuf, ybuf, acc_ref, wg_bf, wu_bf, wd_bf, gather_sem, scatter_sem):
    j = pl.program_id(0)
    n_tiles = pl.num_programs(0)
    slot = j % 2

    def unit_rows(k):
        return pl.ds(pl.multiple_of(k * MOE_UNIT, MOE_UNIT), MOE_UNIT)

    def gather_copy(s, k, u):
        return pltpu.make_async_copy(xn_hbm.at[unit_rows(u), :], xbuf.at[s, unit_rows(k), :],
                                     gather_sem.at[s])

    def scatter_copy(s, k, u):
        return pltpu.make_async_copy(ybuf.at[s, unit_rows(k), :], y_hbm.at[unit_rows(u), :],
                                     scatter_sem.at[s])

    def is_full(tile):
        return unit_of_ref[tile * SORT_UNITS + SORT_UNITS - 1] >= 0

    def for_units(tile, full_fn, partial_fn):
        def run(fn, unroll):
            def unit(k, _):
                fn(k, unit_of_ref[tile * SORT_UNITS + k])
                return 0

            lax.fori_loop(0, SORT_UNITS, unit, 0, unroll=unroll)

        @pl.when(is_full(tile))
        def _():
            run(full_fn, 8)

        @pl.when(jnp.logical_not(is_full(tile)))
        def _():
            def guarded(k, u):
                @pl.when(u >= 0)
                def _():
                    full_fn(k, u)

                if partial_fn is not None:
                    @pl.when(u < 0)
                    def _():
                        partial_fn(k)

            run(guarded, 1)

    def start_gather(tile, s):
        def zero_fill(k):
            xbuf[s, unit_rows(k), :] = jnp.zeros((MOE_UNIT, LS_WIDTH), BF16)

        for_units(tile, lambda k, u: gather_copy(s, k, u).start(), zero_fill)

    def start_scatter(tile, s):
        for_units(tile, lambda k, u: scatter_copy(s, k, u).start(), None)

    def wait_units(tile, whole_copy, unit_copy):
        @pl.when(is_full(tile))
        def _():
            whole_copy.wait()

        @pl.when(jnp.logical_not(is_full(tile)))
        def _():
            def unit(k, _):
                u = unit_of_ref[tile * SORT_UNITS + k]

                @pl.when(u >= 0)
                def _():
                    unit_copy(k, u).wait()

                return 0

            lax.fori_loop(0, SORT_UNITS, unit, 0)

    def wait_gather(tile, s):
        whole = pltpu.make_async_copy(xn_hbm.at[pl.ds(0, ROW_TILE), :], xbuf.at[s], gather_sem.at[s])
        wait_units(tile, whole, lambda k, u: gather_copy(s, k, u))

    def wait_scatter(tile, s):
        whole = pltpu.make_async_copy(ybuf.at[s], y_hbm.at[pl.ds(0, ROW_TILE), :], scatter_sem.at[s])
        wait_units(tile, whole, lambda k, u: scatter_copy(s, k, u))

    @pl.when(j == 0)
    def _():
        start_gather(0, 0)

    @pl.when(j + 1 < n_tiles)
    def _():
        start_gather(j + 1, 1 - slot)

    grp = tile_group_ref[j]
    new_group = (j == 0) | (grp != tile_group_ref[jnp.maximum(j - 1, 0)])

    @pl.when((grp >= 0) & new_group)
    def _():
        wg_bf[...] = wg_ref[...].astype(BF16)
        wu_bf[...] = wu_ref[...].astype(BF16)
        wd_bf[...] = wd_ref[...].astype(BF16)

    wait_gather(j, slot)

    @pl.when(j >= 2)
    def _():
        wait_scatter(j - 2, slot)

    @pl.when(grp >= 0)
    def _():
        xn = xbuf[slot, :, :D_MODEL]
        packed_gates = xbuf[slot, :, D_MODEL:]
        for e in range(EXPERTS_PER_GROUP):
            gate = _unpack_gate(packed_gates, grp * EXPERTS_PER_GROUP + e)
            hg = _dot(xn, wg_bf[e])
            hu = _dot(xn, wu_bf[e])
            act = (hg * jax.nn.sigmoid(hg)) * hu * gate
            out = _dot(act.astype(BF16), wd_bf[e])
            if e == 0:
                acc_ref[...] = out
            else:
                acc_ref[...] += out
        y = acc_ref[...]
        y_hi = y.astype(BF16)
        ybuf[slot, :, :D_MODEL] = y_hi
        ybuf[slot, :, D_MODEL:] = (y - y_hi.astype(F32)).astype(BF16)

    start_scatter(j, slot)

    @pl.when(j == n_tiles - 1)
    def _():
        wait_scatter(j - 1, 1 - slot)
        wait_scatter(j, slot)


def _moe(unit_of, tile_group, xn_ls, wg, wu, wd, layer):
    group_of = lambda j, unit_of, tile_group: jnp.where(tile_group[j] < 0, N_GROUPS - 1,
                                                        tile_group[j])
    up_spec = pl.BlockSpec((None, EXPERTS_PER_GROUP, D_MODEL, D_EXPERT),
                           lambda j, u, t: (layer, group_of(j, u, t), 0, 0))
    down_spec = pl.BlockSpec((None, EXPERTS_PER_GROUP, D_EXPERT, D_MODEL),
                             lambda j, u, t: (layer, group_of(j, u, t), 0, 0))
    any_spec = pl.BlockSpec(memory_space=pl.ANY)
    return pl.pallas_call(
        _moe_kernel,
        grid_spec=pltpu.PrefetchScalarGridSpec(
            num_scalar_prefetch=2,
            grid=(N_SORT_TILES,),
            in_specs=[any_spec, up_spec, up_spec, down_spec],
            out_specs=any_spec,
            scratch_shapes=[pltpu.VMEM((2, ROW_TILE, LS_WIDTH), BF16),
                            pltpu.VMEM((2, ROW_TILE, 2 * D_MODEL), BF16),
                            pltpu.VMEM((ROW_TILE, D_MODEL), F32),
                            pltpu.VMEM((EXPERTS_PER_GROUP, D_MODEL, D_EXPERT), BF16),
                            pltpu.VMEM((EXPERTS_PER_GROUP, D_MODEL, D_EXPERT), BF16),
                            pltpu.VMEM((EXPERTS_PER_GROUP, D_EXPERT, D_MODEL), BF16),
                            pltpu.SemaphoreType.DMA((2,)),
                            pltpu.SemaphoreType.DMA((2,))]),
        out_shape=jax.ShapeDtypeStruct((N_ROW_TILES * LS_TILE, 2 * D_MODEL), BF16),
        compiler_params=_params("arbitrary"),
        name="moe_experts",
    )(unit_of, tile_group, xn_ls, wg, wu, wd)


def _add_unsorted(h_ref, y_ls_ref, dest_ref):
    tm = h_ref.shape[0]
    slot = lax.broadcasted_iota(jnp.int32, (tm, LS_TILE), 1).astype(F32)
    pick = jnp.where(slot == dest_ref[:, 0:1], 1.0, 0.0).astype(BF16)
    y = _dot(pick, y_ls_ref[...])
    return h_ref[...] + (y[:, :D_MODEL] + y[:, D_MODEL:])


def _unsort_final_kernel(h_hbm, y_ls_hbm, dest_ref, gfin_ref, o_ref, h_buf, y_buf, sem):
    i = pl.program_id(0)
    n = pl.num_programs(0)

    def copies(step, slot):
        return (pltpu.make_async_copy(h_hbm.at[pl.ds(step * ROW_TILE, ROW_TILE), :],
                                      h_buf.at[slot], sem.at[0, slot]),
                pltpu.make_async_copy(y_ls_hbm.at[pl.ds(step * LS_TILE, LS_TILE), :],
                                      y_buf.at[slot], sem.at[1, slot]))

    def start(step):
        for c in copies(step, step % UNSORT_SLOTS):
            c.start()

    @pl.when(i == 0)
    def _():
        for step in range(UNSORT_SLOTS - 1):
            start(step)

    @pl.when(i + UNSORT_SLOTS - 1 < n)
    def _():
        start(i + UNSORT_SLOTS - 1)

    slot = i % UNSORT_SLOTS
    for c in copies(i, slot):
        c.wait()
    out = _add_unsorted(h_buf.at[slot], y_buf.at[slot], dest_ref)
    o_ref[...] = _rmsnorm(out, gfin_ref[...])


def _unsort_specs():
    row = lambda n: pl.BlockSpec((ROW_TILE, n), lambda i: (i, 0))
    return [row(D_MODEL), pl.BlockSpec((LS_TILE, 2 * D_MODEL), lambda i: (i, 0)), row(LANES)]


def _unsort_final(h, y_ls, dest, gfin):
    any_spec = pl.BlockSpec(memory_space=pl.ANY)
    return pl.pallas_call(
        _unsort_final_kernel,
        grid=(N_ROW_TILES,),
        in_specs=[any_spec, any_spec, pl.BlockSpec((ROW_TILE, LANES), lambda i: (i, 0)),
                  _full((1, D_MODEL))],
        out_specs=pl.BlockSpec((ROW_TILE, D_MODEL), lambda i: (i, 0)),
        out_shape=jax.ShapeDtypeStruct((SEQ, D_MODEL), F32),
        scratch_shapes=[pltpu.VMEM((UNSORT_SLOTS, ROW_TILE, D_MODEL), F32),
                        pltpu.VMEM((UNSORT_SLOTS, LS_TILE, 2 * D_MODEL), BF16),
                        pltpu.SemaphoreType.DMA((2, UNSORT_SLOTS))],
        compiler_params=_params("arbitrary"),
        name="moe_unsort_final_norm",
    )(h, y_ls, dest, gfin)


def _l1_mixer_kernel(h_ref, y_ls_ref, dest_ref, g_ref, w_in_ref, vg_ref, vb_ref, ws_ref, bs_ref,
                     x_ref, o_ref, w_in_bf):
    tm = h_ref.shape[0]

    @pl.when(pl.program_id(0) == 0)
    def _():
        w_in_bf[...] = w_in_ref[...].astype(BF16)

    x = _add_unsorted(h_ref, y_ls_ref, dest_ref)
    x_ref[...] = x
    xn = _rmsnorm(x, g_ref[...]).astype(BF16)
    p = _dot(xn, w_in_bf[...])
    p = 0.5 * p * (1.0 + lax.erf(p * (2.0 ** -0.5)))
    u = p[:, :D_MODEL]
    v = _layernorm(p[:, D_MODEL:], vg_ref[...], vb_ref[...]).astype(BF16)
    t_idx = lax.broadcasted_iota(jnp.int32, (GM_CHUNK, GM_CHUNK), 0)
    s_idx = lax.broadcasted_iota(jnp.int32, (GM_CHUNK, GM_CHUNK), 1)
    for g in range(GM_GROUPS):
        cols = slice(g * LANES, (g + 1) * LANES)
        wmix = jnp.where(t_idx >= s_idx, ws_ref[g], 0.0).astype(BF16)
        for c in range(tm // GM_CHUNK):
            rows = slice(c * GM_CHUNK, (c + 1) * GM_CHUNK)
            mixed = _dot(wmix, v[rows, cols]) + bs_ref[g]
            o_ref[rows, cols] = (u[rows, cols] * mixed).astype(BF16)


def _l1_mixer(h, y_ls, dest, g, w_in, vg, vb, ws, bs_b):
    row = pl.BlockSpec((ROW_TILE, D_MODEL), lambda i: (i, 0))
    return pl.pallas_call(
        _l1_mixer_kernel,
        grid=(N_ROW_TILES,),
        in_specs=_unsort_specs() + [
            _full((1, D_MODEL)),
            pl.BlockSpec((None, D_MODEL, 2 * D_MODEL), lambda i: (0, 0, 0)),
            _full((1, D_MODEL)), _full((1, D_MODEL)), _full((GM_GROUPS, GM_CHUNK, GM_CHUNK)),
            _full((GM_GROUPS, GM_CHUNK, LANES))],
        out_specs=[row, row],
        out_shape=[jax.ShapeDtypeStruct((SEQ, D_MODEL), F32),
                   jax.ShapeDtypeStruct((SEQ, D_MODEL), BF16)],
        scratch_shapes=[pltpu.VMEM((D_MODEL, 2 * D_MODEL), BF16)],
        compiler_params=_params("arbitrary"),
        name="l1_mixer",
    )(h, y_ls, dest, g, w_in, vg, vb, ws, bs_b)


def _router_weights(we, wg, be, bg):
    pad = LANES - N_EXPERTS - N_GROUPS
    w = jnp.concatenate([we, wg, jnp.zeros((D_MODEL, pad), F32)], axis=1)
    b = jnp.concatenate([be, bg, jnp.zeros((pad,), F32)])[None, :]
    w_hi = w.astype(BF16)
    w_lo = (w - w_hi.astype(F32)).astype(BF16)
    return jnp.concatenate([w_hi, w_lo], axis=1), b


def kernel(x, mem, mem_norm_g, norm_mix, norm_xa, norm_ffn, final_norm_g, ab_w_in, ab_conv_w, ab_conv_b, ab_cnorm_g, ab_cnorm_b, ab_w_out, c_w_in, c_vnorm_g, c_vnorm_b, c_ws, c_bs, c_w_out, xa_wq, xa_wk, xa_wv, xa_wo, rt_wg, rt_bg, rt_we, rt_be, ex_w_gate, ex_w_up, ex_w_down):
    assert x.shape == (1, SEQ, D_MODEL) and mem.shape == (1, MEM_LEN, D_MODEL)
    bf = lambda a: a.astype(BF16)
    r1 = lambda a: a.reshape(1, -1)
    h = x[0]

    memory_len = XA_HEADS * MEM_LEN
    qk_all = _mem_weights(_mem_qk_kernel, "mem_qk", mem[0], r1(mem_norm_g), xa_wk, xa_wq,
                          D_MODEL, memory_len)
    vo_all = _mem_weights(_mem_vo_kernel, "mem_vo", mem[0], r1(mem_norm_g), xa_wv, xa_wo,
                          memory_len, D_MODEL)

    def tail(h, parts, weights, i):
        wr, br = _router_weights(rt_we[i], rt_wg[i], rt_be[i], rt_bg[i])
        h2, xn_ls, dest, units = _out_xattn(
            h, parts, weights, r1(norm_xa[i]), qk_all, vo_all, r1(norm_ffn[i]), wr, br, i)
        unit_of, tile_group = _route_index(units[:, 0, :])
        y_ls = _moe(unit_of, tile_group, xn_ls, ex_w_gate, ex_w_up, ex_w_down, i)
        return h2, y_ls, dest

    conv_out, q, kt, v = _l0_inproj(
        h, r1(norm_mix[0]), ab_w_in, ab_conv_w[0], r1(ab_conv_b[0]), r1(ab_cnorm_g[0]),
        r1(ab_cnorm_b[0]))
    sb_out = _sb_attention(q, kt, v)
    w_out = bf(ab_w_out[0])
    h2, y_ls, dest = tail(h, [conv_out, sb_out], [w_out[:CONV_CH], w_out[CONV_CH:]], 0)

    bs_b = jnp.broadcast_to(c_bs[0][:, :, None], (GM_GROUPS, GM_CHUNK, LANES))
    h, gated = _l1_mixer(h2, y_ls, dest, r1(norm_mix[1]), c_w_in, r1(c_vnorm_g[0]),
                         r1(c_vnorm_b[0]), c_ws[0], bs_b)
    h2, y_ls, dest = tail(h, [gated], [bf(c_w_out[0])], 1)
    return _unsort_final(h2, y_ls, dest, r1(final_norm_g))[None]
```

```python
import functools

import jax
import jax.numpy as jnp
from jax import lax
from jax.experimental import pallas as pl
from jax.experimental.pallas import tpu as pltpu

D_MODEL = 1024
SEQ = 16384
MEM_LEN = 256
EPS = 1e-6
CONV_CH = 512
CONV_WIDTH = 31
SB_HEADS = 8
SB_HEAD_DIM = 64
SB_DIM = 512
GM_GROUPS = 8
GM_CHUNK = 128
XA_HEADS = 4
XA_HEAD_DIM = 256
N_GROUPS = 4
EXPERTS_PER_GROUP = 4
N_EXPERTS = 16
D_EXPERT = 256

LANES = 128
SUBLANES = 8
ROW_TILE = 512
SB_TQ = 256
SB_TK = 256
SB_SUB = 4
SB_LOG_UNDERFLOW = -104.0
SB_HIDDEN_SCORE = -1e30
CONV_HALO = 32
CONV_ROWS = 64
MOE_UNIT = 16
N_ROW_TILES = SEQ // ROW_TILE
LS_TILE = ROW_TILE + N_GROUPS * MOE_UNIT
LS_UNITS = LS_TILE // MOE_UNIT
SORT_UNITS = ROW_TILE // MOE_UNIT
N_SORT_TILES = N_ROW_TILES * LS_UNITS // SORT_UNITS + N_GROUPS
GATE_TERM_STRIDE = 32
LS_WIDTH = D_MODEL + LANES
UNSORT_SLOTS = 3
VMEM_LIMIT = 56 * 1024 * 1024

BF16 = jnp.bfloat16
F32 = jnp.float32


def _params(*semantics):
    return pltpu.CompilerParams(dimension_semantics=semantics, vmem_limit_bytes=VMEM_LIMIT)


def _dot(a, b):
    return jnp.dot(a, b, preferred_element_type=F32)


def _rmsnorm(x, g):
    return x * lax.rsqrt(jnp.mean(x * x, axis=-1, keepdims=True) + EPS) * g


def _layernorm(x, g, b):
    mu = jnp.mean(x, axis=-1, keepdims=True)
    xc = x - mu
    var = jnp.mean(xc * xc, axis=-1, keepdims=True)
    return xc * lax.rsqrt(var + EPS) * g + b


def _full(shape):
    return pl.BlockSpec(shape, lambda *_: (0,) * len(shape))


def _mem_qk_kernel(mem_ref, g_ref, wk_ref, wq_ref, qk_ref):
    memn = _rmsnorm(mem_ref[...], g_ref[...]).astype(BF16)
    k = _dot(memn, wk_ref[0].astype(BF16)).astype(BF16)
    wq = wq_ref[0].astype(BF16)
    for hd in range(XA_HEADS):
        sl = slice(hd * XA_HEAD_DIM, (hd + 1) * XA_HEAD_DIM)
        qk = lax.dot_general(wq[:, sl], k[:, sl], (((1,), (1,)), ((), ())),
                             preferred_element_type=F32)
        qk_ref[0, :, hd * MEM_LEN:(hd + 1) * MEM_LEN] = (qk * (XA_HEAD_DIM ** -0.5)).astype(BF16)


def _mem_vo_kernel(mem_ref, g_ref, wv_ref, wo_ref, vo_ref):
    memn = _rmsnorm(mem_ref[...], g_ref[...]).astype(BF16)
    v = _dot(memn, wv_ref[0].astype(BF16)).astype(BF16)
    wo = wo_ref[0].astype(BF16)
    for hd in range(XA_HEADS):
        sl = slice(hd * XA_HEAD_DIM, (hd + 1) * XA_HEAD_DIM)
        vo_ref[0, hd * MEM_LEN:(hd + 1) * MEM_LEN, :] = _dot(v[:, sl], wo[sl, :]).astype(BF16)


def _mem_weights(kernel_fn, name, mem, g, w_mem, w_attn, out_rows, out_cols):
    depth = w_mem.shape[0]
    wspec = pl.BlockSpec((1, D_MODEL, D_MODEL), lambda i: (i, 0, 0))
    return pl.pallas_call(
        kernel_fn,
        grid=(depth,),
        in_specs=[_full((MEM_LEN, D_MODEL)), _full((1, D_MODEL)), wspec, wspec],
        out_specs=pl.BlockSpec((1, out_rows, out_cols), lambda i: (i, 0, 0)),
        out_shape=jax.ShapeDtypeStruct((depth, out_rows, out_cols), BF16),
        compiler_params=_params("arbitrary"),
        name=name,
    )(mem, g, w_mem, w_attn)


def _l0_inproj_kernel(x_ref, g_ref, w_in_ref, cw_ref, cb_ref, lg_ref, lb_ref,
                      conv_ref, q_ref, kt_ref, v_ref,
                      ext_ref, y_ref, w_ag_ref, w_q_ref, w_kt_ref, w_v_ref):
    tm = x_ref.shape[0]

    @pl.when(pl.program_id(0) == 0)
    def _():
        ext_ref[0:CONV_HALO, :] = jnp.zeros((CONV_HALO, CONV_CH), F32)
        q_lo, k_lo, v_lo = 2 * CONV_CH, 2 * CONV_CH + SB_DIM, 2 * CONV_CH + 2 * SB_DIM
        w_ag_ref[...] = w_in_ref[:, :q_lo].astype(BF16)
        w_q_ref[...] = w_in_ref[:, q_lo:k_lo].astype(BF16)
        w_kt_ref[...] = jnp.transpose(w_in_ref[:, k_lo:v_lo]).astype(BF16)
        w_v_ref[...] = w_in_ref[:, v_lo:].astype(BF16)

    xn = _rmsnorm(x_ref[...], g_ref[...]).astype(BF16)
    ag = _dot(xn, w_ag_ref[...])
    ext_ref[CONV_HALO:, :] = ag[:, :CONV_CH] * jax.nn.sigmoid(ag[:, CONV_CH:])
    q_ref[...] = (_dot(xn, w_q_ref[...]) * (SB_HEAD_DIM ** -0.5)).astype(BF16)
    kt_ref[...] = lax.dot_general(w_kt_ref[...], xn, (((1,), (1,)), ((), ())),
                                  preferred_element_type=F32).astype(BF16)
    v_ref[...] = _dot(xn, w_v_ref[...]).astype(BF16)

    first_tap = CONV_HALO - (CONV_WIDTH - 1)
    for r0 in range(0, tm, CONV_ROWS):
        for j in range(CONV_CH // LANES):
            lanes = slice(j * LANES, (j + 1) * LANES)
            win = ext_ref[r0:r0 + CONV_ROWS + CONV_HALO, lanes]
            acc = jnp.zeros((CONV_ROWS, LANES), F32) + cb_ref[:, lanes]
            for sub in range(SUBLANES):
                offsets = [o for o in range(first_tap, first_tap + CONV_WIDTH)
                           if o % SUBLANES == sub]
                shifted = pltpu.roll(win, win.shape[0] - sub, axis=0) if sub else win
                for o in offsets:
                    tap = cw_ref[o - first_tap:o - first_tap + 1, lanes]
                    acc = acc + shifted[o - sub:o - sub + CONV_ROWS, :] * tap
            y_ref[r0:r0 + CONV_ROWS, lanes] = acc
    y = _layernorm(y_ref[...], lg_ref[...], lb_ref[...])
    conv_ref[...] = (y * jax.nn.sigmoid(y)).astype(BF16)
    ext_ref[0:CONV_HALO, :] = ext_ref[tm:tm + CONV_HALO, :]


def _l0_inproj(x, g, w_in, cw, cb, lg, lb):
    tm = ROW_TILE
    row = lambda n: pl.BlockSpec((tm, n), lambda i: (i, 0))
    return pl.pallas_call(
        _l0_inproj_kernel,
        grid=(N_ROW_TILES,),
        in_specs=[row(D_MODEL), _full((1, D_MODEL)),
                  pl.BlockSpec((None, D_MODEL, 2 * CONV_CH + 3 * SB_DIM), lambda i: (0, 0, 0)),
                  _full((CONV_WIDTH, CONV_CH)), _full((1, CONV_CH)),
                  _full((1, CONV_CH)), _full((1, CONV_CH))],
        out_specs=[row(CONV_CH), row(SB_DIM), pl.BlockSpec((SB_DIM, tm), lambda i: (0, i)),
                   row(SB_DIM)],
        out_shape=[jax.ShapeDtypeStruct((SEQ, CONV_CH), BF16),
                   jax.ShapeDtypeStruct((SEQ, SB_DIM), BF16),
                   jax.ShapeDtypeStruct((SB_DIM, SEQ), BF16),
                   jax.ShapeDtypeStruct((SEQ, SB_DIM), BF16)],
        scratch_shapes=[pltpu.VMEM((tm + CONV_HALO, CONV_CH), F32),
                        pltpu.VMEM((tm, CONV_CH), F32),
                        pltpu.VMEM((D_MODEL, 2 * CONV_CH), BF16),
                        pltpu.VMEM((D_MODEL, SB_DIM), BF16),
                        pltpu.VMEM((SB_DIM, D_MODEL), BF16),
                        pltpu.VMEM((D_MODEL, SB_DIM), BF16)],
        compiler_params=_params("arbitrary"),
        name="l0_inproj_conv",
    )(x, g, w_in, cw, cb, lg, lb)


def _sb_kernel(q_ref, kt_ref, v_ref, o_ref, acc_ref, carry_ref):
    tq, tk = SB_TQ, SB_TK
    lane = lax.broadcasted_iota(jnp.int32, (tq, LANES), 1)

    def heads(q2):
        zero = jnp.zeros_like(q2)
        return (jnp.where(lane < SB_HEAD_DIM, q2, zero), jnp.where(lane >= SB_HEAD_DIM, q2, zero))

    q_blocks = [pl.program_id(1) * SB_SUB + b for b in range(SB_SUB)]
    q_heads = [heads(q_ref[b * tq:(b + 1) * tq, :]) for b in range(SB_SUB)]
    jj = lax.broadcasted_iota(jnp.int32, (tk, tk), 0)
    ss = lax.broadcasted_iota(jnp.int32, (tk, tk), 1)
    minus_later = jnp.where(jj > ss, -1.0, 0.0).astype(BF16)
    minus_later2 = jnp.concatenate([minus_later, minus_later], axis=0)

    def scores(q_head, kb, valid):
        z = _dot(q_head, kt_ref[:, pl.ds(pl.multiple_of(kb * tk, tk), tk)])
        return z if valid is None else jnp.where(valid, z, SB_HIDDEN_SCORE)

    def stay(z):
        sp = jnp.maximum(z, 0.0) + jnp.log(1.0 + jnp.exp(-jnp.abs(z)))
        hi = sp.astype(BF16)
        return sp, hi, (sp - hi.astype(F32)).astype(BF16)

    def later_sum(hi, lo):
        return _dot(jnp.concatenate([hi, lo], axis=1), minus_later2)

    def weighted_values(z, sp, after, kb):
        w = jnp.exp((z - sp) + after)
        return _dot(w.astype(BF16), v_ref[pl.ds(pl.multiple_of(kb * tk, tk), tk), :])

    def block(q_head, kb):
        z = scores(q_head, kb, None)
        sp, hi, lo = stay(z)
        pv = weighted_values(z, sp, later_sum(hi, lo), kb)
        return pv, jnp.sum(sp, axis=1, keepdims=True)

    row = lax.broadcasted_iota(jnp.int32, (tq, tk), 0)
    col = lax.broadcasted_iota(jnp.int32, (tq, tk), 1)
    chains = [(b, h, kb, valid) for b, qb in enumerate(q_blocks) for h in range(2)
              for kb, valid in ((qb, col < row), (jnp.maximum(qb - 1, 0), None))]
    zs = [scores(q_heads[b][h], kb, valid) for b, h, kb, valid in chains]
    stays = [stay(z) for z in zs]
    afters = [later_sum(hi, lo) for _, hi, lo in stays]
    pvs = [weighted_values(z, sp, after, kb)
           for z, (sp, _, _), after, (_, _, kb, _) in zip(zs, stays, afters, chains)]
    sums = [jnp.sum(sp, axis=1, keepdims=True) for sp, _, _ in stays]
    for b, qb in enumerate(q_blocks):
        for h in range(2):
            c = 4 * b + 2 * h
            (pv_diag, pv_prev), (sp_diag, sp_prev) = pvs[c:c + 2], sums[c:c + 2]
            acc_ref[b, h] = pv_diag + jnp.where(qb > 0, jnp.exp(-sp_diag), 0.0) * pv_prev
            carry_ref[b, h] = jnp.broadcast_to(-(sp_diag + sp_prev), (tq, LANES))

    for b, qb in enumerate(q_blocks):
        def alive():
            return jnp.max(carry_ref[b]) > SB_LOG_UNDERFLOW

        def cond(state):
            kb, go = state
            return (kb >= 0) & go

        def body(state):
            kb, _ = state
            for h in range(2):
                pv, sp_sum = block(q_heads[b][h], kb)
                carry = carry_ref[b, h]
                acc_ref[b, h] += jnp.exp(carry) * pv
                carry_ref[b, h] = carry - sp_sum
            return kb - 1, alive()

        lax.while_loop(cond, body, (qb - 2, alive()))
        o_ref[b * tq:(b + 1) * tq, :] = jnp.where(lane < SB_HEAD_DIM, acc_ref[b, 0],
                                                  acc_ref[b, 1]).astype(BF16)


def _sb_attention(q, kt, v):
    rows = SB_SUB * SB_TQ
    state = pltpu.VMEM((SB_SUB, 2, SB_TQ, LANES), F32)
    return pl.pallas_call(
        _sb_kernel,
        grid=(SB_DIM // LANES, SEQ // rows),
        in_specs=[pl.BlockSpec((rows, LANES), lambda p, i: (i, p)),
                  pl.BlockSpec((LANES, SEQ), lambda p, i: (p, 0)),
                  pl.BlockSpec((SEQ, LANES), lambda p, i: (0, p))],
        out_specs=pl.BlockSpec((rows, LANES), lambda p, i: (i, p)),
        out_shape=jax.ShapeDtypeStruct((SEQ, SB_DIM), BF16),
        scratch_shapes=[state, state],
        compiler_params=_params("parallel", "parallel"),
        name="sb_attention",
    )(q, kt, v)


def _router_gates(logits):
    col = lax.broadcasted_iota(jnp.int32, logits.shape, 1)
    colf = col.astype(F32)
    ninf = -jnp.inf
    first = lambda hit: jnp.min(jnp.where(hit, colf, float(LANES)), axis=1, keepdims=True)

    is_group = (col >= N_EXPERTS) & (col < N_EXPERTS + N_GROUPS)
    lg = jnp.where(is_group, logits, ninf)
    gmax = jnp.max(lg, axis=1, keepdims=True)
    pg_top = 1.0 / jnp.sum(jnp.exp(lg - gmax), axis=1, keepdims=True)
    g_idx = first(lg == gmax) - float(N_EXPERTS)

    in_group = (col < N_EXPERTS) & ((col // EXPERTS_PER_GROUP).astype(F32) == g_idx)
    le = jnp.where(in_group, logits, ninf)
    m1 = jnp.max(le, axis=1, keepdims=True)
    i1 = first(le == m1)
    le2 = jnp.where(colf == i1, ninf, le)
    m2 = jnp.max(le2, axis=1, keepdims=True)
    i2 = first(le2 == m2)
    ee = jnp.exp(le - m1)
    pe = ee / jnp.sum(ee, axis=1, keepdims=True)
    p1 = jnp.sum(jnp.where(colf == i1, pe, 0.0), axis=1, keepdims=True)
    p2 = jnp.sum(jnp.where(colf == i2, pe, 0.0), axis=1, keepdims=True)
    den = p1 + p2
    gates = jnp.where(colf == i1, p1 / den * pg_top,
                      jnp.where(colf == i2, p2 / den * pg_top, 0.0))
    return gates, g_idx


def _local_sort(g_idx):
    tm = g_idx.shape[0]
    lane = lax.broadcasted_iota(jnp.int32, (tm, LANES), 1)
    member = jnp.where(lane.astype(F32) == g_idx, 1.0, 0.0)
    r_i = lax.broadcasted_iota(jnp.int32, (tm, tm), 0)
    c_i = lax.broadcasted_iota(jnp.int32, (tm, tm), 1)
    before = jnp.where(c_i < r_i, 1.0, 0.0).astype(BF16)
    rank = jnp.sum(member * _dot(before, member.astype(BF16)), axis=1, keepdims=True)
    count = jnp.sum(member, axis=0, keepdims=True)
    padded = jnp.floor((count + (MOE_UNIT - 1)) * (1.0 / MOE_UNIT)) * MOE_UNIT
    lane1 = lax.broadcasted_iota(jnp.int32, (1, LANES), 1)
    offset = jnp.zeros((1, LANES), F32)
    start = jnp.zeros((1, 1), F32)
    for g in range(N_GROUPS):
        offset = offset + jnp.where((lane1 == g) | (lane1 == N_GROUPS + g), start, 0.0)
        start = start + jnp.sum(jnp.where(lane1 == g, padded, 0.0), axis=1, keepdims=True)
    dest = jnp.sum(member * offset, axis=1, keepdims=True) + rank
    rows = jnp.where(lane1 == N_GROUPS - 1, LS_TILE - offset, padded)
    units = jnp.where(lane1 < N_GROUPS, rows, offset) * (1.0 / MOE_UNIT)
    return dest, units.astype(jnp.int32)


def _pack_gates(gates):
    hi = gates.astype(BF16).astype(F32)
    rest = gates - hi
    mid = rest.astype(BF16).astype(F32)
    lo = rest - mid
    packed = hi + pltpu.roll(mid, GATE_TERM_STRIDE, axis=1) + pltpu.roll(lo, 2 * GATE_TERM_STRIDE, axis=1)
    return packed.astype(BF16)


def _unpack_gate(packed, expert):
    lane = lax.broadcasted_iota(jnp.int32, packed.shape, 1)
    terms = jnp.where(lane % GATE_TERM_STRIDE == expert, packed.astype(F32), 0.0)
    return jnp.sum(terms, axis=1, keepdims=True)


def _out_xattn_kernel(n_parts, *refs):
    h_ref = refs[0]
    part_refs = refs[1:1 + n_parts]
    w_refs = refs[1 + n_parts:1 + 2 * n_parts]
    (gx_ref, qk_ref, vo_ref, gf_ref, wr_ref, br_ref,
     h2_ref, xn_ls_ref, dest_ref, units_ref) = refs[1 + 2 * n_parts:]

    h1 = h_ref[...]
    for p_ref, w_ref in zip(part_refs, w_refs):
        h1 = h1 + _dot(p_ref[...], w_ref[...])

    xn = _rmsnorm(h1, gx_ref[...]).astype(BF16)
    s_all = _dot(xn, qk_ref[0])
    probs = []
    for hd in range(XA_HEADS):
        s = s_all[:, hd * MEM_LEN:(hd + 1) * MEM_LEN]
        e = jnp.exp(s - jnp.max(s, axis=1, keepdims=True))
        probs.append((e / jnp.sum(e, axis=1, keepdims=True)).astype(BF16))
    h2 = h1 + _dot(jnp.concatenate(probs, axis=1), vo_ref[0])
    h2_ref[...] = h2

    xf = _rmsnorm(h2, gf_ref[...])
    x_hi = xf.astype(BF16)
    x_lo = (xf - x_hi.astype(F32)).astype(BF16)
    both = _dot(x_hi, wr_ref[...])
    logits = both[:, :LANES] + both[:, LANES:] + _dot(x_lo, wr_ref[:, :LANES]) + br_ref[...]

    gates, g_idx = _router_gates(logits)
    dest, units = _local_sort(g_idx)
    tm = dest.shape[0]
    dest_ref[...] = jnp.broadcast_to(dest, (tm, LANES))
    units_ref[0] = jnp.broadcast_to(units, (SUBLANES, LANES))
    dest_row = jnp.transpose(jnp.broadcast_to(dest, (tm, LANES)))[0:1, :]
    slot = lax.broadcasted_iota(jnp.int32, (LS_TILE, tm), 0).astype(F32)
    place = jnp.where(slot == dest_row, 1.0, 0.0).astype(BF16)
    routed = jnp.concatenate([x_hi, _pack_gates(gates)], axis=1)
    xn_ls_ref[...] = _dot(place, routed).astype(BF16)


def _out_xattn(h, parts, weights, gx, qk_all, vo_all, gf, wr, br, layer):
    tm = ROW_TILE
    row = lambda n, rows=tm: pl.BlockSpec((rows, n), lambda i: (i, 0))
    qk_spec = pl.BlockSpec((1, D_MODEL, XA_HEADS * MEM_LEN), lambda i: (layer, 0, 0))
    vo_spec = pl.BlockSpec((1, XA_HEADS * MEM_LEN, D_MODEL), lambda i: (layer, 0, 0))
    in_specs = ([row(D_MODEL)] + [row(p.shape[1]) for p in parts]
                + [_full(w.shape) for w in weights]
                + [_full((1, D_MODEL)), qk_spec, vo_spec, _full((1, D_MODEL)),
                   _full((D_MODEL, 2 * LANES)), _full((1, LANES))])
    return pl.pallas_call(
        functools.partial(_out_xattn_kernel, len(parts)),
        grid=(N_ROW_TILES,),
        in_specs=in_specs,
        out_specs=[row(D_MODEL), row(LS_WIDTH, LS_TILE), row(LANES),
                   pl.BlockSpec((1, SUBLANES, LANES), lambda i: (i, 0, 0))],
        out_shape=[jax.ShapeDtypeStruct((SEQ, D_MODEL), F32),
                   jax.ShapeDtypeStruct((N_ROW_TILES * LS_TILE, LS_WIDTH), BF16),
                   jax.ShapeDtypeStruct((SEQ, LANES), F32),
                   jax.ShapeDtypeStruct((N_ROW_TILES, SUBLANES, LANES), jnp.int32)],
        compiler_params=_params("parallel"),
        name="out_xattn_router",
    )(h, *parts, *weights, gx, qk_all, vo_all, gf, wr, br)


def _route_index_kernel(units_ref, unit_of_ref, tile_group_ref):
    def no_unit(k, _):
        unit_of_ref[k] = -1
        return 0

    def no_group(j, _):
        tile_group_ref[j] = -1
        return 0

    pos = 0
    for g in range(N_GROUPS):
        first_tile = pos // SORT_UNITS

        def per_row_tile(i, pos):
            n = units_ref[i, g]
            base = i * LS_UNITS + units_ref[i, N_GROUPS + g]

            def per_unit(k, _):
                unit_of_ref[pos + k] = base + k
                return 0

            lax.fori_loop(0, n, per_unit, 0)
            return pos + n

        pos = lax.fori_loop(0, N_ROW_TILES, per_row_tile, pos)
        end_tile = (pos + SORT_UNITS - 1) // SORT_UNITS

        def mark_tile(j, _):
            tile_group_ref[j] = g
            return 0

        lax.fori_loop(first_tile, end_tile, mark_tile, 0)
        lax.fori_loop(pos, end_tile * SORT_UNITS, no_unit, 0)
        pos = end_tile * SORT_UNITS

    lax.fori_loop(pos, N_SORT_TILES * SORT_UNITS, no_unit, 0)
    lax.fori_loop(pos // SORT_UNITS, N_SORT_TILES, no_group, 0)


def _route_index(units):
    smem = lambda: pl.BlockSpec(memory_space=pltpu.SMEM)
    return pl.pallas_call(
        _route_index_kernel,
        in_specs=[smem()],
        out_specs=[smem(), smem()],
        out_shape=[jax.ShapeDtypeStruct((N_SORT_TILES * SORT_UNITS,), jnp.int32),
                   jax.ShapeDtypeStruct((N_SORT_TILES,), jnp.int32)],
        name="moe_route_index",
    )(units)


def _moe_kernel(unit_of_ref, tile_group_ref, xn_hbm, wg_ref, wu_ref, wd_ref, y_hbm,
                xbuf, ybuf, acc_ref, wg_bf, wu_bf, wd_bf, gather_sem, scatter_sem):
    j = pl.program_id(0)
    n_tiles = pl.num_programs(0)
    slot = j % 2

    def unit_rows(k):
        return pl.ds(pl.multiple_of(k * MOE_UNIT, MOE_UNIT), MOE_UNIT)

    def gather_copy(s, k, u):
        return pltpu.make_async_copy(xn_hbm.at[unit_rows(u), :], xbuf.at[s, unit_rows(k), :],
                                     gather_sem.at[s])

    def scatter_copy(s, k, u):
        return pltpu.make_async_copy(ybuf.at[s, unit_rows(k), :], y_hbm.at[unit_rows(u), :],
                                     scatter_sem.at[s])

    def is_full(tile):
        return unit_of_ref[tile * SORT_UNITS + SORT_UNITS - 1] >= 0

    def for_units(tile, full_fn, partial_fn):
        def run(fn, unroll):
            def unit(k, _):
                fn(k, unit_of_ref[tile * SORT_UNITS + k])
                return 0

            lax.fori_loop(0, SORT_UNITS, unit, 0, unroll=unroll)

        @pl.when(is_full(tile))
        def _():
            run(full_fn, 8)

        @pl.when(jnp.logical_not(is_full(tile)))
        def _():
            def guarded(k, u):
                @pl.when(u >= 0)
                def _():
                    full_fn(k, u)

                if partial_fn is not None:
                    @pl.when(u < 0)
                    def _():
                        partial_fn(k)

            run(guarded, 1)

    def start_gather(tile, s):
        def zero_fill(k):
            xbuf[s, unit_rows(k), :] = jnp.zeros((MOE_UNIT, LS_WIDTH), BF16)

        for_units(tile, lambda k, u: gather_copy(s, k, u).start(), zero_fill)

    def start_scatter(tile, s):
        for_units(tile, lambda k, u: scatter_copy(s, k, u).start(), None)

    def wait_units(tile, whole_copy, unit_copy):
        @pl.when(is_full(tile))
        def _():
            whole_copy.wait()

        @pl.when(jnp.logical_not(is_full(tile)))
        def _():
            def unit(k, _):
                u = unit_of_ref[tile * SORT_UNITS + k]

                @pl.when(u >= 0)
                def _():
                    unit_copy(k, u).wait()

                return 0

            lax.fori_loop(0, SORT_UNITS, unit, 0)

    def wait_gather(tile, s):
        whole = pltpu.make_async_copy(xn_hbm.at[pl.ds(0, ROW_TILE), :], xbuf.at[s], gather_sem.at[s])
        wait_units(tile, whole, lambda k, u: gather_copy(s, k, u))

    def wait_scatter(tile, s):
        whole = pltpu.make_async_copy(ybuf.at[s], y_hbm.at[pl.ds(0, ROW_TILE), :], scatter_sem.at[s])
        wait_units(tile, whole, lambda k, u: scatter_copy(s, k, u))

    @pl.when(j == 0)
    def _():
        start_gather(0, 0)

    @pl.when(j + 1 < n_tiles)
    def _():
        start_gather(j + 1, 1 - slot)

    grp = tile_group_ref[j]
    new_group = (j == 0) | (grp != tile_group_ref[jnp.maximum(j - 1, 0)])

    @pl.when((grp >= 0) & new_group)
    def _():
        wg_bf[...] = wg_ref[...].astype(BF16)
        wu_bf[...] = wu_ref[...].astype(BF16)
        wd_bf[...] = wd_ref[...].astype(BF16)

    wait_gather(j, slot)

    @pl.when(j >= 2)
    def _():
        wait_scatter(j - 2, slot)

    @pl.when(grp >= 0)
    def _():
        xn = xbuf[slot, :, :D_MODEL]
        packed_gates = xbuf[slot, :, D_MODEL:]
        for e in range(EXPERTS_PER_GROUP):
            gate = _unpack_gate(packed_gates, grp * EXPERTS_PER_GROUP + e)
            hg = _dot(xn, wg_bf[e])
            hu = _dot(xn, wu_bf[e])
            act = (hg * jax.nn.sigmoid(hg)) * hu * gate
            out = _dot(act.astype(BF16), wd_bf[e])
            if e == 0:
                acc_ref[...] = out
            else:
                acc_ref[...] += out
        y = acc_ref[...]
        y_hi = y.astype(BF16)
        ybuf[slot, :, :D_MODEL] = y_hi
        ybuf[slot, :, D_MODEL:] = (y - y_hi.astype(F32)).astype(BF16)

    start_scatter(j, slot)

    @pl.when(j == n_tiles - 1)
    def _():
        wait_scatter(j - 1, 1 - slot)
        wait_scatter(j, slot)


def _moe(unit_of, tile_group, xn_ls, wg, wu, wd, layer):
    group_of = lambda j, unit_of, tile_group: jnp.where(tile_group[j] < 0, N_GROUPS - 1,
                                                        tile_group[j])
    up_spec = pl.BlockSpec((None, EXPERTS_PER_GROUP, D_MODEL, D_EXPERT),
                           lambda j, u, t: (layer, group_of(j, u, t), 0, 0))
    down_spec = pl.BlockSpec((None, EXPERTS_PER_GROUP, D_EXPERT, D_MODEL),
                             lambda j, u, t: (layer, group_of(j, u, t), 0, 0))
    any_spec = pl.BlockSpec(memory_space=pl.ANY)
    return pl.pallas_call(
        _moe_kernel,
        grid_spec=pltpu.PrefetchScalarGridSpec(
            num_scalar_prefetch=2,
            grid=(N_SORT_TILES,),
            in_specs=[any_spec, up_spec, up_spec, down_spec],
            out_specs=any_spec,
            scratch_shapes=[pltpu.VMEM((2, ROW_TILE, LS_WIDTH), BF16),
                            pltpu.VMEM((2, ROW_TILE, 2 * D_MODEL), BF16),
                            pltpu.VMEM((ROW_TILE, D_MODEL), F32),
                            pltpu.VMEM((EXPERTS_PER_GROUP, D_MODEL, D_EXPERT), BF16),
                            pltpu.VMEM((EXPERTS_PER_GROUP, D_MODEL, D_EXPERT), BF16),
                            pltpu.VMEM((EXPERTS_PER_GROUP, D_EXPERT, D_MODEL), BF16),
                            pltpu.SemaphoreType.DMA((2,)),
                            pltpu.SemaphoreType.DMA((2,))]),
        out_shape=jax.ShapeDtypeStruct((N_ROW_TILES * LS_TILE, 2 * D_MODEL), BF16),
        compiler_params=_params("arbitrary"),
        name="moe_experts",
    )(unit_of, tile_group, xn_ls, wg, wu, wd)


def _add_unsorted(h_ref, y_ls_ref, dest_ref):
    tm = h_ref.shape[0]
    slot = lax.broadcasted_iota(jnp.int32, (tm, LS_TILE), 1).astype(F32)
    pick = jnp.where(slot == dest_ref[:, 0:1], 1.0, 0.0).astype(BF16)
    y = _dot(pick, y_ls_ref[...])
    return h_ref[...] + (y[:, :D_MODEL] + y[:, D_MODEL:])


def _unsort_final_kernel(h_hbm, y_ls_hbm, dest_ref, gfin_ref, o_ref, h_buf, y_buf, sem):
    i = pl.program_id(0)
    n = pl.num_programs(0)

    def copies(step, slot):
        return (pltpu.make_async_copy(h_hbm.at[pl.ds(step * ROW_TILE, ROW_TILE), :],
                                      h_buf.at[slot], sem.at[0, slot]),
                pltpu.make_async_copy(y_ls_hbm.at[pl.ds(step * LS_TILE, LS_TILE), :],
                                      y_buf.at[slot], sem.at[1, slot]))

    def start(step):
        for c in copies(step, step % UNSORT_SLOTS):
            c.start()

    @pl.when(i == 0)
    def _():
        for step in range(UNSORT_SLOTS - 1):
            start(step)

    @pl.when(i + UNSORT_SLOTS - 1 < n)
    def _():
        start(i + UNSORT_SLOTS - 1)

    slot = i % UNSORT_SLOTS
    for c in copies(i, slot):
        c.wait()
    out = _add_unsorted(h_buf.at[slot], y_buf.at[slot], dest_ref)
    o_ref[...] = _rmsnorm(out, gfin_ref[...])


def _unsort_specs():
    row = lambda n: pl.BlockSpec((ROW_TILE, n), lambda i: (i, 0))
    return [row(D_MODEL), pl.BlockSpec((LS_TILE, 2 * D_MODEL), lambda i: (i, 0)), row(LANES)]


def _unsort_final(h, y_ls, dest, gfin):
    any_spec = pl.BlockSpec(memory_space=pl.ANY)
    return pl.pallas_call(
        _unsort_final_kernel,
        grid=(N_ROW_TILES,),
        in_specs=[any_spec, any_spec, pl.BlockSpec((ROW_TILE, LANES), lambda i: (i, 0)),
                  _full((1, D_MODEL))],
        out_specs=pl.BlockSpec((ROW_TILE, D_MODEL), lambda i: (i, 0)),
        out_shape=jax.ShapeDtypeStruct((SEQ, D_MODEL), F32),
        scratch_shapes=[pltpu.VMEM((UNSORT_SLOTS, ROW_TILE, D_MODEL), F32),
                        pltpu.VMEM((UNSORT_SLOTS, LS_TILE, 2 * D_MODEL), BF16),
                        pltpu.SemaphoreType.DMA((2, UNSORT_SLOTS))],
        compiler_params=_params("arbitrary"),
        name="moe_unsort_final_norm",
    )(h, y_ls, dest, gfin)


def _l1_mixer_kernel(h_ref, y_ls_ref, dest_ref, g_ref, w_in_ref, vg_ref, vb_ref, ws_ref, bs_ref,
                     x_ref, o_ref, w_in_bf):
    tm = h_ref.shape[0]

    @pl.when(pl.program_id(0) == 0)
    def _():
        w_in_bf[...] = w_in_ref[...].astype(BF16)

    x = _add_unsorted(h_ref, y_ls_ref, dest_ref)
    x_ref[...] = x
    xn = _rmsnorm(x, g_ref[...]).astype(BF16)
    p = _dot(xn, w_in_bf[...])
    p = 0.5 * p * (1.0 + lax.erf(p * (2.0 ** -0.5)))
    u = p[:, :D_MODEL]
    v = _layernorm(p[:, D_MODEL:], vg_ref[...], vb_ref[...]).astype(BF16)
    t_idx = lax.broadcasted_iota(jnp.int32, (GM_CHUNK, GM_CHUNK), 0)
    s_idx = lax.broadcasted_iota(jnp.int32, (GM_CHUNK, GM_CHUNK), 1)
    for g in range(GM_GROUPS):
        cols = slice(g * LANES, (g + 1) * LANES)
        wmix = jnp.where(t_idx >= s_idx, ws_ref[g], 0.0).astype(BF16)
        for c in range(tm // GM_CHUNK):
            rows = slice(c * GM_CHUNK, (c + 1) * GM_CHUNK)
            mixed = _dot(wmix, v[rows, cols]) + bs_ref[g]
            o_ref[rows, cols] = (u[rows, cols] * mixed).astype(BF16)


def _l1_mixer(h, y_ls, dest, g, w_in, vg, vb, ws, bs_b):
    row = pl.BlockSpec((ROW_TILE, D_MODEL), lambda i: (i, 0))
    return pl.pallas_call(
        _l1_mixer_kernel,
        grid=(N_ROW_TILES,),
        in_specs=_unsort_specs() + [
            _full((1, D_MODEL)),
            pl.BlockSpec((None, D_MODEL, 2 * D_MODEL), lambda i: (0, 0, 0)),
            _full((1, D_MODEL)), _full((1, D_MODEL)), _full((GM_GROUPS, GM_CHUNK, GM_CHUNK)),
            _full((GM_GROUPS, GM_CHUNK, LANES))],
        out_specs=[row, row],
        out_shape=[jax.ShapeDtypeStruct((SEQ, D_MODEL), F32),
                   jax.ShapeDtypeStruct((SEQ, D_MODEL), BF16)],
        scratch_shapes=[pltpu.VMEM((D_MODEL, 2 * D_MODEL), BF16)],
        compiler_params=_params("arbitrary"),
        name="l1_mixer",
    )(h, y_ls, dest, g, w_in, vg, vb, ws, bs_b)


def _router_weights(we, wg, be, bg):
    pad = LANES - N_EXPERTS - N_GROUPS
    w = jnp.concatenate([we, wg, jnp.zeros((D_MODEL, pad), F32)], axis=1)
    b = jnp.concatenate([be, bg, jnp.zeros((pad,), F32)])[None, :]
    w_hi = w.astype(BF16)
    w_lo = (w - w_hi.astype(F32)).astype(BF16)
    return jnp.concatenate([w_hi, w_lo], axis=1), b


def kernel(x, mem, mem_norm_g, norm_mix, norm_xa, norm_ffn, final_norm_g, ab_w_in, ab_conv_w, ab_conv_b, ab_cnorm_g, ab_cnorm_b, ab_w_out, c_w_in, c_vnorm_g, c_vnorm_b, c_ws, c_bs, c_w_out, xa_wq, xa_wk, xa_wv, xa_wo, rt_wg, rt_bg, rt_we, rt_be, ex_w_gate, ex_w_up, ex_w_down):
    assert x.shape == (1, SEQ, D_MODEL) and mem.shape == (1, MEM_LEN, D_MODEL)
    bf = lambda a: a.astype(BF16)
    r1 = lambda a: a.reshape(1, -1)
    h = x[0]

    memory_len = XA_HEADS * MEM_LEN
    qk_all = _mem_weights(_mem_qk_kernel, "mem_qk", mem[0], r1(mem_norm_g), xa_wk, xa_wq,
                          D_MODEL, memory_len)
    vo_all = _mem_weights(_mem_vo_kernel, "mem_vo", mem[0], r1(mem_norm_g), xa_wv, xa_wo,
                          memory_len, D_MODEL)

    def tail(h, parts, weights, i):
        wr, br = _router_weights(rt_we[i], rt_wg[i], rt_be[i], rt_bg[i])
        h2, xn_ls, dest, units = _out_xattn(
            h, parts, weights, r1(norm_xa[i]), qk_all, vo_all, r1(norm_ffn[i]), wr, br, i)
        unit_of, tile_group = _route_index(units[:, 0, :])
        y_ls = _moe(unit_of, tile_group, xn_ls, ex_w_gate, ex_w_up, ex_w_down, i)
        return h2, y_ls, dest

    conv_out, q, kt, v = _l0_inproj(
        h, r1(norm_mix[0]), ab_w_in, ab_conv_w[0], r1(ab_conv_b[0]), r1(ab_cnorm_g[0]),
        r1(ab_cnorm_b[0]))
    sb_out = _sb_attention(q, kt, v)
    w_out = bf(ab_w_out[0])
    h2, y_ls, dest = tail(h, [conv_out, sb_out], [w_out[:CONV_CH], w_out[CONV_CH:]], 0)

    bs_b = jnp.broadcast_to(c_bs[0][:, :, None], (GM_GROUPS, GM_CHUNK, LANES))
    h, gated = _l1_mixer(h2, y_ls, dest, r1(norm_mix[1]), c_w_in, r1(c_vnorm_g[0]),
                         r1(c_vnorm_b[0]), c_ws[0], bs_b)
    h2, y_ls, dest = tail(h, [gated], [bf(c_w_out[0])], 1)
    return _unsort_final(h2, y_ls, dest, r1(final_norm_g))[None]
```

```python
import functools

import jax
import jax.numpy as jnp
from jax import lax
from jax.experimental import pallas as pl
from jax.experimental.pallas import tpu as pltpu

D_MODEL = 1024
SEQ = 16384
MEM_LEN = 256
EPS = 1e-6
CONV_CH = 512
CONV_WIDTH = 31
SB_HEADS = 8
SB_HEAD_DIM = 64
SB_DIM = 512
GM_GROUPS = 8
GM_CHUNK = 128
XA_HEADS = 4
XA_HEAD_DIM = 256
N_GROUPS = 4
EXPERTS_PER_GROUP = 4
N_EXPERTS = 16
D_EXPERT = 256

LANES = 128
SUBLANES = 8
ROW_TILE = 512
SB_TQ = 256
SB_TK = 256
SB_SUB = 4
SB_LOG_UNDERFLOW = -104.0
SB_HIDDEN_SCORE = -1e30
CONV_HALO = 32
CONV_ROWS = 64
MOE_UNIT = 16
N_ROW_TILES = SEQ // ROW_TILE
LS_TILE = ROW_TILE + N_GROUPS * MOE_UNIT
LS_UNITS = LS_TILE // MOE_UNIT
SORT_UNITS = ROW_TILE // MOE_UNIT
N_SORT_TILES = N_ROW_TILES * LS_UNITS // SORT_UNITS + N_GROUPS
GATE_TERM_STRIDE = 32
LS_WIDTH = D_MODEL + LANES
UNSORT_SLOTS = 3
VMEM_LIMIT = 56 * 1024 * 1024

BF16 = jnp.bfloat16
F32 = jnp.float32


def _params(*semantics):
    return pltpu.CompilerParams(dimension_semantics=semantics, vmem_limit_bytes=VMEM_LIMIT)


def _dot(a, b):
    return jnp.dot(a, b, preferred_element_type=F32)


def _rmsnorm(x, g):
    return x * lax.rsqrt(jnp.mean(x * x, axis=-1, keepdims=True) + EPS) * g


def _layernorm(x, g, b):
    mu = jnp.mean(x, axis=-1, keepdims=True)
    xc = x - mu
    var = jnp.mean(xc * xc, axis=-1, keepdims=True)
    return xc * lax.rsqrt(var + EPS) * g + b


def _full(shape):
    return pl.BlockSpec(shape, lambda *_: (0,) * len(shape))


def _mem_qk_kernel(mem_ref, g_ref, wk_ref, wq_ref, qk_ref):
    memn = _rmsnorm(mem_ref[...], g_ref[...]).astype(BF16)
    k = _dot(memn, wk_ref[0].astype(BF16)).astype(BF16)
    wq = wq_ref[0].astype(BF16)
    for hd in range(XA_HEADS):
        sl = slice(hd * XA_HEAD_DIM, (hd + 1) * XA_HEAD_DIM)
        qk = lax.dot_general(wq[:, sl], k[:, sl], (((1,), (1,)), ((), ())),
                             preferred_element_type=F32)
        qk_ref[0, :, hd * MEM_LEN:(hd + 1) * MEM_LEN] = (qk * (XA_HEAD_DIM ** -0.5)).astype(BF16)


def _mem_vo_kernel(mem_ref, g_ref, wv_ref, wo_ref, vo_ref):
    memn = _rmsnorm(mem_ref[...], g_ref[...]).astype(BF16)
    v = _dot(memn, wv_ref[0].astype(BF16)).astype(BF16)
    wo = wo_ref[0].astype(BF16)
    for hd in range(XA_HEADS):
        sl = slice(hd * XA_HEAD_DIM, (hd + 1) * XA_HEAD_DIM)
        vo_ref[0, hd * MEM_LEN:(hd + 1) * MEM_LEN, :] = _dot(v[:, sl], wo[sl, :]).astype(BF16)


def _mem_weights(kernel_fn, name, mem, g, w_mem, w_attn, out_rows, out_cols):
    depth = w_mem.shape[0]
    wspec = pl.BlockSpec((1, D_MODEL, D_MODEL), lambda i: (i, 0, 0))
    return pl.pallas_call(
        kernel_fn,
        grid=(depth,),
        in_specs=[_full((MEM_LEN, D_MODEL)), _full((1, D_MODEL)), wspec, wspec],
        out_specs=pl.BlockSpec((1, out_rows, out_cols), lambda i: (i, 0, 0)),
        out_shape=jax.ShapeDtypeStruct((depth, out_rows, out_cols), BF16),
        compiler_params=_params("arbitrary"),
        name=name,
    )(mem, g, w_mem, w_attn)


def _l0_inproj_kernel(x_ref, g_ref, w_in_ref, cw_ref, cb_ref, lg_ref, lb_ref,
                      conv_ref, q_ref, kt_ref, v_ref,
                      ext_ref, y_ref, w_ag_ref, w_q_ref, w_kt_ref, w_v_ref):
    tm = x_ref.shape[0]

    @pl.when(pl.program_id(0) == 0)
    def _():
        ext_ref[0:CONV_HALO, :] = jnp.zeros((CONV_HALO, CONV_CH), F32)
        q_lo, k_lo, v_lo = 2 * CONV_CH, 2 * CONV_CH + SB_DIM, 2 * CONV_CH + 2 * SB_DIM
        w_ag_ref[...] = w_in_ref[:, :q_lo].astype(BF16)
        w_q_ref[...] = w_in_ref[:, q_lo:k_lo].astype(BF16)
        w_kt_ref[...] = jnp.transpose(w_in_ref[:, k_lo:v_lo]).astype(BF16)
        w_v_ref[...] = w_in_ref[:, v_lo:].astype(BF16)

    xn = _rmsnorm(x_ref[...], g_ref[...]).astype(BF16)
    ag = _dot(xn, w_ag_ref[...])
    ext_ref[CONV_HALO:, :] = ag[:, :CONV_CH] * jax.nn.sigmoid(ag[:, CONV_CH:])
    q_ref[...] = (_dot(xn, w_q_ref[...]) * (SB_HEAD_DIM ** -0.5)).astype(BF16)
    kt_ref[...] = lax.dot_general(w_kt_ref[...], xn, (((1,), (1,)), ((), ())),
                                  preferred_element_type=F32).astype(BF16)
    v_ref[...] = _dot(xn, w_v_ref[...]).astype(BF16)

    first_tap = CONV_HALO - (CONV_WIDTH - 1)
    for r0 in range(0, tm, CONV_ROWS):
        for j in range(CONV_CH // LANES):
            lanes = slice(j * LANES, (j + 1) * LANES)
            win = ext_ref[r0:r0 + CONV_ROWS + CONV_HALO, lanes]
            acc = jnp.zeros((CONV_ROWS, LANES), F32) + cb_ref[:, lanes]
            for sub in range(SUBLANES):
                offsets = [o for o in range(first_tap, first_tap + CONV_WIDTH)
                           if o % SUBLANES == sub]
                shifted = pltpu.roll(win, win.shape[0] - sub, axis=0) if sub else win
                for o in offsets:
                    tap = cw_ref[o - first_tap:o - first_tap + 1, lanes]
                    acc = acc + shifted[o - sub:o - sub + CONV_ROWS, :] * tap
            y_ref[r0:r0 + CONV_ROWS, lanes] = acc
    y = _layernorm(y_ref[...], lg_ref[...], lb_ref[...])
    conv_ref[...] = (y * jax.nn.sigmoid(y)).astype(BF16)
    ext_ref[0:CONV_HALO, :] = ext_ref[tm:tm + CONV_HALO, :]


def _l0_inproj(x, g, w_in, cw, cb, lg, lb):
    tm = ROW_TILE
    row = lambda n: pl.BlockSpec((tm, n), lambda i: (i, 0))
    return pl.pallas_call(
        _l0_inproj_kernel,
        grid=(N_ROW_TILES,),
        in_specs=[row(D_MODEL), _full((1, D_MODEL)),
                  pl.BlockSpec((None, D_MODEL, 2 * CONV_CH + 3 * SB_DIM), lambda i: (0, 0, 0)),
                  _full((CONV_WIDTH, CONV_CH)), _full((1, CONV_CH)),
                  _full((1, CONV_CH)), _full((1, CONV_CH))],
        out_specs=[row(CONV_CH), row(SB_DIM), pl.BlockSpec((SB_DIM, tm), lambda i: (0, i)),
                   row(SB_DIM)],
        out_shape=[jax.ShapeDtypeStruct((SEQ, CONV_CH), BF16),
                   jax.ShapeDtypeStruct((SEQ, SB_DIM), BF16),
                   jax.ShapeDtypeStruct((SB_DIM, SEQ), BF16),
                   jax.ShapeDtypeStruct((SEQ, SB_DIM), BF16)],
        scratch_shapes=[pltpu.VMEM((tm + CONV_HALO, CONV_CH), F32),
                        pltpu.VMEM((tm, CONV_CH), F32),
                        pltpu.VMEM((D_MODEL, 2 * CONV_CH), BF16),
                        pltpu.VMEM((D_MODEL, SB_DIM), BF16),
                        pltpu.VMEM((SB_DIM, D_MODEL), BF16),
                        pltpu.VMEM((D_MODEL, SB_DIM), BF16)],
        compiler_params=_params("arbitrary"),
        name="l0_inproj_conv",
    )(x, g, w_in, cw, cb, lg, lb)


def _sb_kernel(q_ref, kt_ref, v_ref, o_ref, acc_ref, carry_ref):
    tq, tk = SB_TQ, SB_TK
    lane = lax.broadcasted_iota(jnp.int32, (tq, LANES), 1)

    def heads(q2):
        zero = jnp.zeros_like(q2)
        return (jnp.where(lane < SB_HEAD_DIM, q2, zero), jnp.where(lane >= SB_HEAD_DIM, q2, zero))

    q_blocks = [pl.program_id(1) * SB_SUB + b for b in range(SB_SUB)]
    q_heads = [heads(q_ref[b * tq:(b + 1) * tq, :]) for b in range(SB_SUB)]
    jj = lax.broadcasted_iota(jnp.int32, (tk, tk), 0)
    ss = lax.broadcasted_iota(jnp.int32, (tk, tk), 1)
    minus_later = jnp.where(jj > ss, -1.0, 0.0).astype(BF16)
    minus_later2 = jnp.concatenate([minus_later, minus_later], axis=0)

    def scores(q_head, kb, valid):
        z = _dot(q_head, kt_ref[:, pl.ds(pl.multiple_of(kb * tk, tk), tk)])
        return z if valid is None else jnp.where(valid, z, SB_HIDDEN_SCORE)

    def stay(z):
        sp = jnp.maximum(z, 0.0) + jnp.log(1.0 + jnp.exp(-jnp.abs(z)))
        hi = sp.astype(BF16)
        return sp, hi, (sp - hi.astype(F32)).astype(BF16)

    def later_sum(hi, lo):
        return _dot(jnp.concatenate([hi, lo], axis=1), minus_later2)

    def weighted_values(z, sp, after, kb):
        w = jnp.exp((z - sp) + after)
        return _dot(w.astype(BF16), v_ref[pl.ds(pl.multiple_of(kb * tk, tk), tk), :])

    def block(q_head, kb):
        z = scores(q_head, kb, None)
        sp, hi, lo = stay(z)
        pv = weighted_values(z, sp, later_sum(hi, lo), kb)
        return pv, jnp.sum(sp, axis=1, keepdims=True)

    row = lax.broadcasted_iota(jnp.int32, (tq, tk), 0)
    col = lax.broadcasted_iota(jnp.int32, (tq, tk), 1)
    chains = [(b, h, kb, valid) for b, qb in enumerate(q_blocks) for h in range(2)
              for kb, valid in ((qb, col < row), (jnp.maximum(qb - 1, 0), None))]
    zs = [scores(q_heads[b][h], kb, valid) for b, h, kb, valid in chains]
    stays = [stay(z) for z in zs]
    afters = [later_sum(hi, lo) for _, hi, lo in stays]
    pvs = [weighted_values(z, sp, after, kb)
           for z, (sp, _, _), after, (_, _, kb, _) in zip(zs, stays, afters, chains)]
    sums = [jnp.sum(sp, axis=1, keepdims=True) for sp, _, _ in stays]
    for b, qb in enumerate(q_blocks):
        for h in range(2):
            c = 4 * b + 2 * h
            (pv_diag, pv_prev), (sp_diag, sp_prev) = pvs[c:c + 2], sums[c:c + 2]
            acc_ref[b, h] = pv_diag + jnp.where(qb > 0, jnp.exp(-sp_diag), 0.0) * pv_prev
            carry_ref[b, h] = jnp.broadcast_to(-(sp_diag + sp_prev), (tq, LANES))

    for b, qb in enumerate(q_blocks):
        def alive():
            return jnp.max(carry_ref[b]) > SB_LOG_UNDERFLOW

        def cond(state):
            kb, go = state
            return (kb >= 0) & go

        def body(state):
            kb, _ = state
            for h in range(2):
                pv, sp_sum = block(q_heads[b][h], kb)
                carry = carry_ref[b, h]
                acc_ref[b, h] += jnp.exp(carry) * pv
                carry_ref[b, h] = carry - sp_sum
            return kb - 1, alive()

        lax.while_loop(cond, body, (qb - 2, alive()))
        o_ref[b * tq:(b + 1) * tq, :] = jnp.where(lane < SB_HEAD_DIM, acc_ref[b, 0],
                                                  acc_ref[b, 1]).astype(BF16)


def _sb_attention(q, kt, v):
    rows = SB_SUB * SB_TQ
    state = pltpu.VMEM((SB_SUB, 2, SB_TQ, LANES), F32)
    return pl.pallas_call(
        _sb_kernel,
        grid=(SB_DIM // LANES, SEQ // rows),
        in_specs=[pl.BlockSpec((rows, LANES), lambda p, i: (i, p)),
                  pl.BlockSpec((LANES, SEQ), lambda p, i: (p, 0)),
                  pl.BlockSpec((SEQ, LANES), lambda p, i: (0, p))],
        out_specs=pl.BlockSpec((rows, LANES), lambda p, i: (i, p)),
        out_shape=jax.ShapeDtypeStruct((SEQ, SB_DIM), BF16),
        scratch_shapes=[state, state],
        compiler_params=_params("parallel", "parallel"),
        name="sb_attention",
    )(q, kt, v)


def _router_gates(logits):
    col = lax.broadcasted_iota(jnp.int32, logits.shape, 1)
    colf = col.astype(F32)
    ninf = -jnp.inf
    first = lambda hit: jnp.min(jnp.where(hit, colf, float(LANES)), axis=1, keepdims=True)

    is_group = (col >= N_EXPERTS) & (col < N_EXPERTS + N_GROUPS)
    lg = jnp.where(is_group, logits, ninf)
    gmax = jnp.max(lg, axis=1, keepdims=True)
    pg_top = 1.0 / jnp.sum(jnp.exp(lg - gmax), axis=1, keepdims=True)
    g_idx = first(lg == gmax) - float(N_EXPERTS)

    in_group = (col < N_EXPERTS) & ((col // EXPERTS_PER_GROUP).astype(F32) == g_idx)
    le = jnp.where(in_group, logits, ninf)
    m1 = jnp.max(le, axis=1, keepdims=True)
    i1 = first(le == m1)
    le2 = jnp.where(colf == i1, ninf, le)
    m2 = jnp.max(le2, axis=1, keepdims=True)
    i2 = first(le2 == m2)
    ee = jnp.exp(le - m1)
    pe = ee / jnp.sum(ee, axis=1, keepdims=True)
    p1 = jnp.sum(jnp.where(colf == i1, pe, 0.0), axis=1, keepdims=True)
    p2 = jnp.sum(jnp.where(colf == i2, pe, 0.0), axis=1, keepdims=True)
    den = p1 + p2
    gates = jnp.where(colf == i1, p1 / den * pg_top,
                      jnp.where(colf == i2, p2 / den * pg_top, 0.0))
    return gates, g_idx


def _local_sort(g_idx):
    tm = g_idx.shape[0]
    lane = lax.broadcasted_iota(jnp.int32, (tm, LANES), 1)
    member = jnp.where(lane.astype(F32) == g_idx, 1.0, 0.0)
    r_i = lax.broadcasted_iota(jnp.int32, (tm, tm), 0)
    c_i = lax.broadcasted_iota(jnp.int32, (tm, tm), 1)
    before = jnp.where(c_i < r_i, 1.0, 0.0).astype(BF16)
    rank = jnp.sum(member * _dot(before, member.astype(BF16)), axis=1, keepdims=True)
    count = jnp.sum(member, axis=0, keepdims=True)
    padded = jnp.floor((count + (MOE_UNIT - 1)) * (1.0 / MOE_UNIT)) * MOE_UNIT
    lane1 = lax.broadcasted_iota(jnp.int32, (1, LANES), 1)
    offset = jnp.zeros((1, LANES), F32)
    start = jnp.zeros((1, 1), F32)
    for g in range(N_GROUPS):
        offset = offset + jnp.where((lane1 == g) | (lane1 == N_GROUPS + g), start, 0.0)
        start = start + jnp.sum(jnp.where(lane1 == g, padded, 0.0), axis=1, keepdims=True)
    dest = jnp.sum(member * offset, axis=1, keepdims=True) + rank
    rows = jnp.where(lane1 == N_GROUPS - 1, LS_TILE - offset, padded)
    units = jnp.where(lane1 < N_GROUPS, rows, offset) * (1.0 / MOE_UNIT)
    return dest, units.astype(jnp.int32)


def _pack_gates(gates):
    hi = gates.astype(BF16).astype(F32)
    rest = gates - hi
    mid = rest.astype(BF16).astype(F32)
    lo = rest - mid
    packed = hi + pltpu.roll(mid, GATE_TERM_STRIDE, axis=1) + pltpu.roll(lo, 2 * GATE_TERM_STRIDE, axis=1)
    return packed.astype(BF16)


def _unpack_gate(packed, expert):
    lane = lax.broadcasted_iota(jnp.int32, packed.shape, 1)
    terms = jnp.where(lane % GATE_TERM_STRIDE == expert, packed.astype(F32), 0.0)
    return jnp.sum(terms, axis=1, keepdims=True)


def _out_xattn_kernel(n_parts, *refs):
    h_ref = refs[0]
    part_refs = refs[1:1 + n_parts]
    w_refs = refs[1 + n_parts:1 + 2 * n_parts]
    (gx_ref, qk_ref, vo_ref, gf_ref, wr_ref, br_ref,
     h2_ref, xn_ls_ref, dest_ref, units_ref) = refs[1 + 2 * n_parts:]

    h1 = h_ref[...]
    for p_ref, w_ref in zip(part_refs, w_refs):
        h1 = h1 + _dot(p_ref[...], w_ref[...])

    xn = _rmsnorm(h1, gx_ref[...]).astype(BF16)
    s_all = _dot(xn, qk_ref[0])
    probs = []
    for hd in range(XA_HEADS):
        s = s_all[:, hd * MEM_LEN:(hd + 1) * MEM_LEN]
        e = jnp.exp(s - jnp.max(s, axis=1, keepdims=True))
        probs.append((e / jnp.sum(e, axis=1, keepdims=True)).astype(BF16))
    h2 = h1 + _dot(jnp.concatenate(probs, axis=1), vo_ref[0])
    h2_ref[...] = h2

    xf = _rmsnorm(h2, gf_ref[...])
    x_hi = xf.astype(BF16)
    x_lo = (xf - x_hi.astype(F32)).astype(BF16)
    both = _dot(x_hi, wr_ref[...])
    logits = both[:, :LANES] + both[:, LANES:] + _dot(x_lo, wr_ref[:, :LANES]) + br_ref[...]

    gates, g_idx = _router_gates(logits)
    dest, units = _local_sort(g_idx)
    tm = dest.shape[0]
    dest_ref[...] = jnp.broadcast_to(dest, (tm, LANES))
    units_ref[0] = jnp.broadcast_to(units, (SUBLANES, LANES))
    dest_row = jnp.transpose(jnp.broadcast_to(dest, (tm, LANES)))[0:1, :]
    slot = lax.broadcasted_iota(jnp.int32, (LS_TILE, tm), 0).astype(F32)
    place = jnp.where(slot == dest_row, 1.0, 0.0).astype(BF16)
    routed = jnp.concatenate([x_hi, _pack_gates(gates)], axis=1)
    xn_ls_ref[...] = _dot(place, routed).astype(BF16)


def _out_xattn(h, parts, weights, gx, qk_all, vo_all, gf, wr, br, layer):
    tm = ROW_TILE
    row = lambda n, rows=tm: pl.BlockSpec((rows, n), lambda i: (i, 0))
    qk_spec = pl.BlockSpec((1, D_MODEL, XA_HEADS * MEM_LEN), lambda i: (layer, 0, 0))
    vo_spec = pl.BlockSpec((1, XA_HEADS * MEM_LEN, D_MODEL), lambda i: (layer, 0, 0))
    in_specs = ([row(D_MODEL)] + [row(p.shape[1]) for p in parts]
                + [_full(w.shape) for w in weights]
                + [_full((1, D_MODEL)), qk_spec, vo_spec, _full((1, D_MODEL)),
                   _full((D_MODEL, 2 * LANES)), _full((1, LANES))])
    return pl.pallas_call(
        functools.partial(_out_xattn_kernel, len(parts)),
        grid=(N_ROW_TILES,),
        in_specs=in_specs,
        out_specs=[row(D_MODEL), row(LS_WIDTH, LS_TILE), row(LANES),
                   pl.BlockSpec((1, SUBLANES, LANES), lambda i: (i, 0, 0))],
        out_shape=[jax.ShapeDtypeStruct((SEQ, D_MODEL), F32),
                   jax.ShapeDtypeStruct((N_ROW_TILES * LS_TILE, LS_WIDTH), BF16),
                   jax.ShapeDtypeStruct((SEQ, LANES), F32),
                   jax.ShapeDtypeStruct((N_ROW_TILES, SUBLANES, LANES), jnp.int32)],
        compiler_params=_params("parallel"),
        name="out_xattn_router",
    )(h, *parts, *weights, gx, qk_all, vo_all, gf, wr, br)


def _route_index_kernel(units_ref, unit_of_ref, tile_group_ref):
    u = units_ref[...].astype(F32)
    nt = u.shape[0]
    lane = lax.broadcasted_iota(jnp.int32, (nt, LANES), 1)
    lane1 = lax.broadcasted_iota(jnp.int32, (1, LANES), 1)
    count = jnp.where(lane < N_GROUPS, u, 0.0)
    offset = pltpu.roll(u, LANES - N_GROUPS, axis=1)
    r_i = lax.broadcasted_iota(jnp.int32, (nt, nt), 0)
    c_i = lax.broadcasted_iota(jnp.int32, (nt, nt), 1)
    before = _dot(jnp.where(c_i < r_i, 1.0, 0.0).astype(BF16), count.astype(BF16))
    total = jnp.sum(count, axis=0, keepdims=True)
    group_slots = jnp.floor((total + (SORT_UNITS - 1)) * (1.0 / SORT_UNITS)) * SORT_UNITS

    first_slot = jnp.zeros((1, LANES), F32)
    tile_first = (lane1 * SORT_UNITS).astype(F32)
    tile_group = jnp.full((1, LANES), -1.0, F32)
    start = jnp.zeros((1, 1), F32)
    for g in range(N_GROUPS):
        size = jnp.sum(jnp.where(lane1 == g, group_slots, 0.0), axis=1, keepdims=True)
        first_slot = first_slot + jnp.where(lane1 == g, start, 0.0)
        tile_group = jnp.where((tile_first >= start) & (tile_first < start + size),
                               float(g), tile_group)
        start = start + size
    tile_group_ref[...] = tile_group.astype(jnp.int32)

    tile_id = lax.broadcasted_iota(jnp.int32, (nt, LANES), 0).astype(F32)
    run_lo = first_slot + before
    shift = tile_id * LS_UNITS + offset - run_lo

    def column(a):
        return jnp.concatenate(
            [(pltpu.roll(a, LANES - g, axis=1) if g else a)[:, 0:1] for g in range(N_GROUPS)],
            axis=0)

    lo = column(run_lo)
    hi = lo + column(count)
    sh = column(shift)
    for chunk in range(unit_of_ref.shape[0]):
        slot = (lane1 + chunk * LANES).astype(F32)
        hit = (slot >= lo) & (slot < hi)
        unit = jnp.sum(jnp.where(hit, slot + sh, 0.0), axis=0, keepdims=True)
        found = jnp.sum(jnp.where(hit, 1.0, 0.0), axis=0, keepdims=True)
        unit_of_ref[chunk:chunk + 1, :] = jnp.where(found > 0.0, unit, -1.0).astype(jnp.int32)


def _route_index(units):
    n_slots = N_SORT_TILES * SORT_UNITS
    unit_of, tile_group = pl.pallas_call(
        _route_index_kernel,
        out_shape=[jax.ShapeDtypeStruct((n_slots // LANES, LANES), jnp.int32),
                   jax.ShapeDtypeStruct((1, LANES), jnp.int32)],
        name="moe_route_index",
    )(units)
    return unit_of.reshape(n_slots), tile_group[0, :N_SORT_TILES]


def _moe_kernel(unit_of_ref, tile_group_ref, xn_hbm, wg_ref, wu_ref, wd_ref, y_hbm,
                xbuf, ybuf, acc_ref, wg_bf, wu_bf, wd_bf, gather_sem, scatter_sem):
    j = pl.program_id(0)
    n_tiles = pl.num_programs(0)
    slot = j % 2

    def unit_rows(k):
        return pl.ds(pl.multiple_of(k * MOE_UNIT, MOE_UNIT), MOE_UNIT)

    def gather_copy(s, k, u):
        return pltpu.make_async_copy(xn_hbm.at[unit_rows(u), :], xbuf.at[s, unit_rows(k), :],
                                     gather_sem.at[s])

    def scatter_copy(s, k, u):
        return pltpu.make_async_copy(ybuf.at[s, unit_rows(k), :], y_hbm.at[unit_rows(u), :],
                                     scatter_sem.at[s])

    def is_full(tile):
        return unit_of_ref[tile * SORT_UNITS + SORT_UNITS - 1] >= 0

    def for_units(tile, full_fn, partial_fn):
        def run(fn, unroll):
            def unit(k, _):
                fn(k, unit_of_ref[tile * SORT_UNITS + k])
                return 0

            lax.fori_loop(0, SORT_UNITS, unit, 0, unroll=unroll)

        @pl.when(is_full(tile))
        def _():
            run(full_fn, 8)

        @pl.when(jnp.logical_not(is_full(tile)))
        def _():
            def guarded(k, u):
                @pl.when(u >= 0)
                def _():
                    full_fn(k, u)

                if partial_fn is not None:
                    @pl.when(u < 0)
                    def _():
                        partial_fn(k)

            run(guarded, 1)

    def start_gather(tile, s):
        def zero_fill(k):
            xbuf[s, unit_rows(k), :] = jnp.zeros((MOE_UNIT, LS_WIDTH), BF16)

        for_units(tile, lambda k, u: gather_copy(s, k, u).start(), zero_fill)

    def start_scatter(tile, s):
        for_units(tile, lambda k, u: scatter_copy(s, k, u).start(), None)

    def wait_units(tile, whole_copy, unit_copy):
        @pl.when(is_full(tile))
        def _():
            whole_copy.wait()

        @pl.when(jnp.logical_not(is_full(tile)))
        def _():
            def unit(k, _):
                u = unit_of_ref[tile * SORT_UNITS + k]

                @pl.when(u >= 0)
                def _():
                    unit_copy(k, u).wait()

                return 0

            lax.fori_loop(0, SORT_UNITS, unit, 0)

    def wait_gather(tile, s):
        whole = pltpu.make_async_copy(xn_hbm.at[pl.ds(0, ROW_TILE), :], xbuf.at[s], gather_sem.at[s])
        wait_units(tile, whole, lambda k, u: gather_copy(s, k, u))

    def wait_scatter(tile, s):
        whole = pltpu.make_async_copy(ybuf.at[s], y_hbm.at[pl.ds(0, ROW_TILE), :], scatter_sem.at[s])
        wait_units(tile, whole, lambda k, u: scatter_copy(s, k, u))

    @pl.when(j == 0)
    def _():
        start_gather(0, 0)

    @pl.when(j + 1 < n_tiles)
    def _():
        start_gather(j + 1, 1 - slot)

    grp = tile_group_ref[j]
    new_group = (j == 0) | (grp != tile_group_ref[jnp.maximum(j - 1, 0)])

    @pl.when((grp >= 0) & new_group)
    def _():
        wg_bf[...] = wg_ref[...].astype(BF16)
        wu_bf[...] = wu_ref[...].astype(BF16)
        wd_bf[...] = wd_ref[...].astype(BF16)

    wait_gather(j, slot)

    @pl.when(j >= 2)
    def _():
        wait_scatter(j - 2, slot)

    @pl.when(grp >= 0)
    def _():
        xn = xbuf[slot, :, :D_MODEL]
        packed_gates = xbuf[slot, :, D_MODEL:]
        for e in range(EXPERTS_PER_GROUP):
            gate = _unpack_gate(packed_gates, grp * EXPERTS_PER_GROUP + e)
            hg = _dot(xn, wg_bf[e])
            hu = _dot(xn, wu_bf[e])
            act = (hg * jax.nn.sigmoid(hg)) * hu * gate
            out = _dot(act.astype(BF16), wd_bf[e])
            if e == 0:
                acc_ref[...] = out
            else:
                acc_ref[...] += out
        y = acc_ref[...]
        y_hi = y.astype(BF16)
        ybuf[slot, :, :D_MODEL] = y_hi
        ybuf[slot, :, D_MODEL:] = (y - y_hi.astype(F32)).astype(BF16)

    start_scatter(j, slot)

    @pl.when(j == n_tiles - 1)
    def _():
        wait_scatter(j - 1, 1 - slot)
        wait_scatter(j, slot)


def _moe(unit_of, tile_group, xn_ls, wg, wu, wd, layer):
    group_of = lambda j, unit_of, tile_group: jnp.where(tile_group[j] < 0, N_GROUPS - 1,
                                                        tile_group[j])
    up_spec = pl.BlockSpec((None, EXPERTS_PER_GROUP, D_MODEL, D_EXPERT),
                           lambda j, u, t: (layer, group_of(j, u, t), 0, 0))
    down_spec = pl.BlockSpec((None, EXPERTS_PER_GROUP, D_EXPERT, D_MODEL),
                             lambda j, u, t: (layer, group_of(j, u, t), 0, 0))
    any_spec = pl.BlockSpec(memory_space=pl.ANY)
    return pl.pallas_call(
        _moe_kernel,
        grid_spec=pltpu.PrefetchScalarGridSpec(
            num_scalar_prefetch=2,
            grid=(N_SORT_TILES,),
            in_specs=[any_spec, up_spec, up_spec, down_spec],
            out_specs=any_spec,
            scratch_shapes=[pltpu.VMEM((2, ROW_TILE, LS_WIDTH), BF16),
                            pltpu.VMEM((2, ROW_TILE, 2 * D_MODEL), BF16),
                            pltpu.VMEM((ROW_TILE, D_MODEL), F32),
                            pltpu.VMEM((EXPERTS_PER_GROUP, D_MODEL, D_EXPERT), BF16),
                            pltpu.VMEM((EXPERTS_PER_GROUP, D_MODEL, D_EXPERT), BF16),
                            pltpu.VMEM((EXPERTS_PER_GROUP, D_EXPERT, D_MODEL), BF16),
                            pltpu.SemaphoreType.DMA((2,)),
                            pltpu.SemaphoreType.DMA((2,))]),
        out_shape=jax.ShapeDtypeStruct((N_ROW_TILES * LS_TILE, 2 * D_MODEL), BF16),
        compiler_params=_params("arbitrary"),
        name="moe_experts",
    )(unit_of, tile_group, xn_ls, wg, wu, wd)


def _add_unsorted(h_ref, y_ls_ref, dest_ref):
    tm = h_ref.shape[0]
    slot = lax.broadcasted_iota(jnp.int32, (tm, LS_TILE), 1).astype(F32)
    pick = jnp.where(slot == dest_ref[:, 0:1], 1.0, 0.0).astype(BF16)
    y = _dot(pick, y_ls_ref[...])
    return h_ref[...] + (y[:, :D_MODEL] + y[:, D_MODEL:])


def _unsort_final_kernel(h_hbm, y_ls_hbm, dest_ref, gfin_ref, o_ref, h_buf, y_buf, sem):
    i = pl.program_id(0)
    n = pl.num_programs(0)

    def copies(step, slot):
        return (pltpu.make_async_copy(h_hbm.at[pl.ds(step * ROW_TILE, ROW_TILE), :],
                                      h_buf.at[slot], sem.at[0, slot]),
                pltpu.make_async_copy(y_ls_hbm.at[pl.ds(step * LS_TILE, LS_TILE), :],
                                      y_buf.at[slot], sem.at[1, slot]))

    def start(step):
        for c in copies(step, step % UNSORT_SLOTS):
            c.start()

    @pl.when(i == 0)
    def _():
        for step in range(UNSORT_SLOTS - 1):
            start(step)

    @pl.when(i + UNSORT_SLOTS - 1 < n)
    def _():
        start(i + UNSORT_SLOTS - 1)

    slot = i % UNSORT_SLOTS
    for c in copies(i, slot):
        c.wait()
    out = _add_unsorted(h_buf.at[slot], y_buf.at[slot], dest_ref)
    o_ref[...] = _rmsnorm(out, gfin_ref[...])


def _unsort_specs():
    row = lambda n: pl.BlockSpec((ROW_TILE, n), lambda i: (i, 0))
    return [row(D_MODEL), pl.BlockSpec((LS_TILE, 2 * D_MODEL), lambda i: (i, 0)), row(LANES)]


def _unsort_final(h, y_ls, dest, gfin):
    any_spec = pl.BlockSpec(memory_space=pl.ANY)
    return pl.pallas_call(
        _unsort_final_kernel,
        grid=(N_ROW_TILES,),
        in_specs=[any_spec, any_spec, pl.BlockSpec((ROW_TILE, LANES), lambda i: (i, 0)),
                  _full((1, D_MODEL))],
        out_specs=pl.BlockSpec((ROW_TILE, D_MODEL), lambda i: (i, 0)),
        out_shape=jax.ShapeDtypeStruct((SEQ, D_MODEL), F32),
        scratch_shapes=[pltpu.VMEM((UNSORT_SLOTS, ROW_TILE, D_MODEL), F32),
                        pltpu.VMEM((UNSORT_SLOTS, LS_TILE, 2 * D_MODEL), BF16),
                        pltpu.SemaphoreType.DMA((2, UNSORT_SLOTS))],
        compiler_params=_params("arbitrary"),
        name="moe_unsort_final_norm",
    )(h, y_ls, dest, gfin)


def _l1_mixer_kernel(h_ref, y_ls_ref, dest_ref, g_ref, w_in_ref, vg_ref, vb_ref, ws_ref, bs_ref,
                     x_ref, o_ref, w_in_bf):
    tm = h_ref.shape[0]

    @pl.when(pl.program_id(0) == 0)
    def _():
        w_in_bf[...] = w_in_ref[...].astype(BF16)

    x = _add_unsorted(h_ref, y_ls_ref, dest_ref)
    x_ref[...] = x
    xn = _rmsnorm(x, g_ref[...]).astype(BF16)
    p = _dot(xn, w_in_bf[...])
    p = 0.5 * p * (1.0 + lax.erf(p * (2.0 ** -0.5)))
    u = p[:, :D_MODEL]
    v = _layernorm(p[:, D_MODEL:], vg_ref[...], vb_ref[...]).astype(BF16)
    t_idx = lax.broadcasted_iota(jnp.int32, (GM_CHUNK, GM_CHUNK), 0)
    s_idx = lax.broadcasted_iota(jnp.int32, (GM_CHUNK, GM_CHUNK), 1)
    for g in range(GM_GROUPS):
        cols = slice(g * LANES, (g + 1) * LANES)
        wmix = jnp.where(t_idx >= s_idx, ws_ref[g], 0.0).astype(BF16)
        for c in range(tm // GM_CHUNK):
            rows = slice(c * GM_CHUNK, (c + 1) * GM_CHUNK)
            mixed = _dot(wmix, v[rows, cols]) + bs_ref[g]
            o_ref[rows, cols] = (u[rows, cols] * mixed).astype(BF16)


def _l1_mixer(h, y_ls, dest, g, w_in, vg, vb, ws, bs_b):
    row = pl.BlockSpec((ROW_TILE, D_MODEL), lambda i: (i, 0))
    return pl.pallas_call(
        _l1_mixer_kernel,
        grid=(N_ROW_TILES,),
        in_specs=_unsort_specs() + [
            _full((1, D_MODEL)),
            pl.BlockSpec((None, D_MODEL, 2 * D_MODEL), lambda i: (0, 0, 0)),
            _full((1, D_MODEL)), _full((1, D_MODEL)), _full((GM_GROUPS, GM_CHUNK, GM_CHUNK)),
            _full((GM_GROUPS, GM_CHUNK, LANES))],
        out_specs=[row, row],
        out_shape=[jax.ShapeDtypeStruct((SEQ, D_MODEL), F32),
                   jax.ShapeDtypeStruct((SEQ, D_MODEL), BF16)],
        scratch_shapes=[pltpu.VMEM((D_MODEL, 2 * D_MODEL), BF16)],
        compiler_params=_params("arbitrary"),
        name="l1_mixer",
    )(h, y_ls, dest, g, w_in, vg, vb, ws, bs_b)


def _router_weights(we, wg, be, bg):
    pad = LANES - N_EXPERTS - N_GROUPS
    w = jnp.concatenate([we, wg, jnp.zeros((D_MODEL, pad), F32)], axis=1)
    b = jnp.concatenate([be, bg, jnp.zeros((pad,), F32)])[None, :]
    w_hi = w.astype(BF16)
    w_lo = (w - w_hi.astype(F32)).astype(BF16)
    return jnp.concatenate([w_hi, w_lo], axis=1), b


def kernel(x, mem, mem_norm_g, norm_mix, norm_xa, norm_ffn, final_norm_g, ab_w_in, ab_conv_w, ab_conv_b, ab_cnorm_g, ab_cnorm_b, ab_w_out, c_w_in, c_vnorm_g, c_vnorm_b, c_ws, c_bs, c_w_out, xa_wq, xa_wk, xa_wv, xa_wo, rt_wg, rt_bg, rt_we, rt_be, ex_w_gate, ex_w_up, ex_w_down):
    assert x.shape == (1, SEQ, D_MODEL) and mem.shape == (1, MEM_LEN, D_MODEL)
    bf = lambda a: a.astype(BF16)
    r1 = lambda a: a.reshape(1, -1)
    h = x[0]

    memory_len = XA_HEADS * MEM_LEN
    qk_all = _mem_weights(_mem_qk_kernel, "mem_qk", mem[0], r1(mem_norm_g), xa_wk, xa_wq,
                          D_MODEL, memory_len)
    vo_all = _mem_weights(_mem_vo_kernel, "mem_vo", mem[0], r1(mem_norm_g), xa_wv, xa_wo,
                          memory_len, D_MODEL)

    def tail(h, parts, weights, i):
        wr, br = _router_weights(rt_we[i], rt_wg[i], rt_be[i], rt_bg[i])
        h2, xn_ls, dest, units = _out_xattn(
            h, parts, weights, r1(norm_xa[i]), qk_all, vo_all, r1(norm_ffn[i]), wr, br, i)
        unit_of, tile_group = _route_index(units[:, 0, :])
        y_ls = _moe(unit_of, tile_group, xn_ls, ex_w_gate, ex_w_up, ex_w_down, i)
        return h2, y_ls, dest

    conv_out, q, kt, v = _l0_inproj(
        h, r1(norm_mix[0]), ab_w_in, ab_conv_w[0], r1(ab_conv_b[0]), r1(ab_cnorm_g[0]),
        r1(ab_cnorm_b[0]))
    sb_out = _sb_attention(q, kt, v)
    w_out = bf(ab_w_out[0])
    h2, y_ls, dest = tail(h, [conv_out, sb_out], [w_out[:CONV_CH], w_out[CONV_CH:]], 0)

    bs_b = jnp.broadcast_to(c_bs[0][:, :, None], (GM_GROUPS, GM_CHUNK, LANES))
    h, gated = _l1_mixer(h2, y_ls, dest, r1(norm_mix[1]), c_w_in, r1(c_vnorm_g[0]),
                         r1(c_vnorm_b[0]), c_ws[0], bs_b)
    h2, y_ls, dest = tail(h, [gated], [bf(c_w_out[0])], 1)
    return _unsort_final(h2, y_ls, dest, r1(final_norm_g))[None]
```

```python
import functools

import jax
import jax.numpy as jnp
from jax import lax
from jax.experimental import pallas as pl
from jax.experimental.pallas import tpu as pltpu

D_MODEL = 1024
SEQ = 16384
MEM_LEN = 256
EPS = 1e-6
CONV_CH = 512
CONV_WIDTH = 31
SB_HEADS = 8
SB_HEAD_DIM = 64
SB_DIM = SB_HEADS * SB_HEAD_DIM
GM_GROUPS = 8
GM_CHUNK = 128
XA_HEADS = 4
XA_HEAD_DIM = 256
N_GROUPS = 4
EXPERTS_PER_GROUP = 4
N_EXPERTS = N_GROUPS * EXPERTS_PER_GROUP
D_EXPERT = 256

LANES = 128
SUBLANES = 8
ROW_TILE = 512
SB_TQ = 256
SB_TK = 256
SB_SUB = 8
SB_LOG_UNDERFLOW = -104.0
SB_HIDDEN_SCORE = -1e30
CONV_HALO = 32
CONV_ROWS = 64
MOE_UNIT = 16
N_ROW_TILES = SEQ // ROW_TILE
LS_TILE = ROW_TILE + N_GROUPS * MOE_UNIT
LS_UNITS = LS_TILE // MOE_UNIT
SORT_UNITS = ROW_TILE // MOE_UNIT
N_SORT_TILES = N_ROW_TILES * LS_UNITS // SORT_UNITS + N_GROUPS
GATE_TERM_STRIDE = 32
LS_WIDTH = D_MODEL + LANES
UNSORT_SLOTS = 3
VMEM_LIMIT = 56 * 1024 * 1024

BF16 = jnp.bfloat16
F32 = jnp.float32


def _params(*semantics):
    return pltpu.CompilerParams(dimension_semantics=semantics, vmem_limit_bytes=VMEM_LIMIT)


def _dot(a, b):
    return jnp.dot(a, b, preferred_element_type=F32)


def _rmsnorm(x, g):
    return x * lax.rsqrt(jnp.mean(x * x, axis=-1, keepdims=True) + EPS) * g


def _layernorm(x, g, b):
    mu = jnp.mean(x, axis=-1, keepdims=True)
    xc = x - mu
    var = jnp.mean(xc * xc, axis=-1, keepdims=True)
    return xc * lax.rsqrt(var + EPS) * g + b


def _full(shape):
    return pl.BlockSpec(shape, lambda *_: (0,) * len(shape))


def _mem_qk_kernel(mem_ref, g_ref, wk_ref, wq_ref, qk_ref):
    memn = _rmsnorm(mem_ref[...], g_ref[...]).astype(BF16)
    k = _dot(memn, wk_ref[0].astype(BF16)).astype(BF16)
    wq = wq_ref[0].astype(BF16)
    for hd in range(XA_HEADS):
        sl = slice(hd * XA_HEAD_DIM, (hd + 1) * XA_HEAD_DIM)
        qk = lax.dot_general(wq[:, sl], k[:, sl], (((1,), (1,)), ((), ())),
                             preferred_element_type=F32)
        qk_ref[0, :, hd * MEM_LEN:(hd + 1) * MEM_LEN] = (qk * (XA_HEAD_DIM ** -0.5)).astype(BF16)


def _mem_vo_kernel(mem_ref, g_ref, wv_ref, wo_ref, vo_ref):
    memn = _rmsnorm(mem_ref[...], g_ref[...]).astype(BF16)
    v = _dot(memn, wv_ref[0].astype(BF16)).astype(BF16)
    wo = wo_ref[0].astype(BF16)
    for hd in range(XA_HEADS):
        sl = slice(hd * XA_HEAD_DIM, (hd + 1) * XA_HEAD_DIM)
        vo_ref[0, hd * MEM_LEN:(hd + 1) * MEM_LEN, :] = _dot(v[:, sl], wo[sl, :]).astype(BF16)


def _mem_weights(kernel_fn, name, mem, g, w_mem, w_attn, out_rows, out_cols):
    depth = w_mem.shape[0]
    wspec = pl.BlockSpec((1, D_MODEL, D_MODEL), lambda i: (i, 0, 0))
    return pl.pallas_call(
        kernel_fn,
        grid=(depth,),
        in_specs=[_full((MEM_LEN, D_MODEL)), _full((1, D_MODEL)), wspec, wspec],
        out_specs=pl.BlockSpec((1, out_rows, out_cols), lambda i: (i, 0, 0)),
        out_shape=jax.ShapeDtypeStruct((depth, out_rows, out_cols), BF16),
        compiler_params=_params("arbitrary"),
        name=name,
    )(mem, g, w_mem, w_attn)


def _l0_inproj_kernel(x_ref, g_ref, w_in_ref, cw_ref, cb_ref, lg_ref, lb_ref,
                      conv_ref, q_ref, kt_ref, v_ref,
                      ext_ref, y_ref, w_ag_ref, w_q_ref, w_kt_ref, w_v_ref):
    tm = x_ref.shape[0]

    @pl.when(pl.program_id(0) == 0)
    def _():
        ext_ref[0:CONV_HALO, :] = jnp.zeros((CONV_HALO, CONV_CH), F32)
        q_lo, k_lo, v_lo = 2 * CONV_CH, 2 * CONV_CH + SB_DIM, 2 * CONV_CH + 2 * SB_DIM
        w_ag_ref[...] = w_in_ref[:, :q_lo].astype(BF16)
        w_q_ref[...] = w_in_ref[:, q_lo:k_lo].astype(BF16)
        w_kt_ref[...] = jnp.transpose(w_in_ref[:, k_lo:v_lo]).astype(BF16)
        w_v_ref[...] = w_in_ref[:, v_lo:].astype(BF16)

    xn = _rmsnorm(x_ref[...], g_ref[...]).astype(BF16)
    ag = _dot(xn, w_ag_ref[...])
    ext_ref[CONV_HALO:, :] = ag[:, :CONV_CH] * jax.nn.sigmoid(ag[:, CONV_CH:])
    q_ref[...] = (_dot(xn, w_q_ref[...]) * (SB_HEAD_DIM ** -0.5)).astype(BF16)
    kt_ref[...] = lax.dot_general(w_kt_ref[...], xn, (((1,), (1,)), ((), ())),
                                  preferred_element_type=F32).astype(BF16)
    v_ref[...] = _dot(xn, w_v_ref[...]).astype(BF16)

    first_tap = CONV_HALO - (CONV_WIDTH - 1)
    for r0 in range(0, tm, CONV_ROWS):
        for j in range(CONV_CH // LANES):
            lanes = slice(j * LANES, (j + 1) * LANES)
            win = ext_ref[r0:r0 + CONV_ROWS + CONV_HALO, lanes]
            acc = jnp.zeros((CONV_ROWS, LANES), F32) + cb_ref[:, lanes]
            for sub in range(SUBLANES):
                offsets = [o for o in range(first_tap, first_tap + CONV_WIDTH)
                           if o % SUBLANES == sub]
                shifted = pltpu.roll(win, win.shape[0] - sub, axis=0) if sub else win
                for o in offsets:
                    tap = cw_ref[o - first_tap:o - first_tap + 1, lanes]
                    acc = acc + shifted[o - sub:o - sub + CONV_ROWS, :] * tap
            y_ref[r0:r0 + CONV_ROWS, lanes] = acc
    y = _layernorm(y_ref[...], lg_ref[...], lb_ref[...])
    conv_ref[...] = (y * jax.nn.sigmoid(y)).astype(BF16)
    ext_ref[0:CONV_HALO, :] = ext_ref[tm:tm + CONV_HALO, :]


def _l0_inproj(x, g, w_in, cw, cb, lg, lb):
    tm = ROW_TILE
    row = lambda n: pl.BlockSpec((tm, n), lambda i: (i, 0))
    return pl.pallas_call(
        _l0_inproj_kernel,
        grid=(N_ROW_TILES,),
        in_specs=[row(D_MODEL), _full((1, D_MODEL)),
                  pl.BlockSpec((None, D_MODEL, 2 * CONV_CH + 3 * SB_DIM), lambda i: (0, 0, 0)),
                  _full((CONV_WIDTH, CONV_CH)), _full((1, CONV_CH)),
                  _full((1, CONV_CH)), _full((1, CONV_CH))],
        out_specs=[row(CONV_CH), row(SB_DIM), pl.BlockSpec((SB_DIM, tm), lambda i: (0, i)),
                   row(SB_DIM)],
        out_shape=[jax.ShapeDtypeStruct((SEQ, CONV_CH), BF16),
                   jax.ShapeDtypeStruct((SEQ, SB_DIM), BF16),
                   jax.ShapeDtypeStruct((SB_DIM, SEQ), BF16),
                   jax.ShapeDtypeStruct((SEQ, SB_DIM), BF16)],
        scratch_shapes=[pltpu.VMEM((tm + CONV_HALO, CONV_CH), F32),
                        pltpu.VMEM((tm, CONV_CH), F32),
                        pltpu.VMEM((D_MODEL, 2 * CONV_CH), BF16),
                        pltpu.VMEM((D_MODEL, SB_DIM), BF16),
                        pltpu.VMEM((SB_DIM, D_MODEL), BF16),
                        pltpu.VMEM((D_MODEL, SB_DIM), BF16)],
        compiler_params=_params("arbitrary"),
        name="l0_inproj_conv",
    )(x, g, w_in, cw, cb, lg, lb)


def _sb_kernel(q_ref, kt_ref, v_ref, o_ref, acc_ref, carry_ref):
    tq, tk = SB_TQ, SB_TK
    lane = lax.broadcasted_iota(jnp.int32, (tq, LANES), 1)

    def heads(q2):
        zero = jnp.zeros_like(q2)
        return (jnp.where(lane < SB_HEAD_DIM, q2, zero), jnp.where(lane >= SB_HEAD_DIM, q2, zero))

    q_blocks = [pl.program_id(1) * SB_SUB + b for b in range(SB_SUB)]
    q_heads = [heads(q_ref[b * tq:(b + 1) * tq, :]) for b in range(SB_SUB)]
    jj = lax.broadcasted_iota(jnp.int32, (tk, tk), 0)
    ss = lax.broadcasted_iota(jnp.int32, (tk, tk), 1)
    minus_later = jnp.where(jj > ss, -1.0, 0.0).astype(BF16)
    minus_later2 = jnp.concatenate([minus_later, minus_later], axis=0)

    def scores(q_head, kb, valid):
        z = _dot(q_head, kt_ref[:, pl.ds(pl.multiple_of(kb * tk, tk), tk)])
        return z if valid is None else jnp.where(valid, z, SB_HIDDEN_SCORE)

    def stay(z):
        sp = jnp.maximum(z, 0.0) + jnp.log(1.0 + jnp.exp(-jnp.abs(z)))
        hi = sp.astype(BF16)
        return sp, hi, (sp - hi.astype(F32)).astype(BF16)

    def later_sum(hi, lo):
        return _dot(jnp.concatenate([hi, lo], axis=1), minus_later2)

    def weighted_values(z, sp, after, kb):
        w = jnp.exp((z - sp) + after)
        return _dot(w.astype(BF16), v_ref[pl.ds(pl.multiple_of(kb * tk, tk), tk), :])

    def block(q_head, kb):
        z = scores(q_head, kb, None)
        sp, hi, lo = stay(z)
        pv = weighted_values(z, sp, later_sum(hi, lo), kb)
        return pv, jnp.sum(sp, axis=1, keepdims=True)

    row = lax.broadcasted_iota(jnp.int32, (tq, tk), 0)
    col = lax.broadcasted_iota(jnp.int32, (tq, tk), 1)
    chains = [(b, h, kb, valid) for b, qb in enumerate(q_blocks) for h in range(2)
              for kb, valid in ((qb, col < row), (jnp.maximum(qb - 1, 0), None))]
    zs = [scores(q_heads[b][h], kb, valid) for b, h, kb, valid in chains]
    stays = [stay(z) for z in zs]
    afters = [later_sum(hi, lo) for _, hi, lo in stays]
    pvs = [weighted_values(z, sp, after, kb)
           for z, (sp, _, _), after, (_, _, kb, _) in zip(zs, stays, afters, chains)]
    sums = [jnp.sum(sp, axis=1, keepdims=True) for sp, _, _ in stays]
    for b, qb in enumerate(q_blocks):
        for h in range(2):
            c = 4 * b + 2 * h
            (pv_diag, pv_prev), (sp_diag, sp_prev) = pvs[c:c + 2], sums[c:c + 2]
            acc_ref[b, h] = pv_diag + jnp.where(qb > 0, jnp.exp(-sp_diag), 0.0) * pv_prev
            carry_ref[b, h] = jnp.broadcast_to(-(sp_diag + sp_prev), (tq, LANES))

    for b, qb in enumerate(q_blocks):
        def alive():
            return jnp.max(carry_ref[b]) > SB_LOG_UNDERFLOW

        def cond(state):
            kb, go = state
            return (kb >= 0) & go

        def body(state):
            kb, _ = state
            for h in range(2):
                pv, sp_sum = block(q_heads[b][h], kb)
                carry = carry_ref[b, h]
                acc_ref[b, h] += jnp.exp(carry) * pv
                carry_ref[b, h] = carry - sp_sum
            return kb - 1, alive()

        lax.while_loop(cond, body, (qb - 2, alive()))
        o_ref[b * tq:(b + 1) * tq, :] = jnp.where(lane < SB_HEAD_DIM, acc_ref[b, 0],
                                                  acc_ref[b, 1]).astype(BF16)


def _sb_attention(q, kt, v):
    rows = SB_SUB * SB_TQ
    state = pltpu.VMEM((SB_SUB, 2, SB_TQ, LANES), F32)
    return pl.pallas_call(
        _sb_kernel,
        grid=(SB_DIM // LANES, SEQ // rows),
        in_specs=[pl.BlockSpec((rows, LANES), lambda p, i: (i, p)),
                  pl.BlockSpec((LANES, SEQ), lambda p, i: (p, 0)),
                  pl.BlockSpec((SEQ, LANES), lambda p, i: (0, p))],
        out_specs=pl.BlockSpec((rows, LANES), lambda p, i: (i, p)),
        out_shape=jax.ShapeDtypeStruct((SEQ, SB_DIM), BF16),
        scratch_shapes=[state, state],
        compiler_params=_params("parallel", "parallel"),
        name="sb_attention",
    )(q, kt, v)


def _router_gates(logits):
    col = lax.broadcasted_iota(jnp.int32, logits.shape, 1)
    colf = col.astype(F32)
    ninf = -jnp.inf
    first = lambda hit: jnp.min(jnp.where(hit, colf, float(LANES)), axis=1, keepdims=True)

    is_group = (col >= N_EXPERTS) & (col < N_EXPERTS + N_GROUPS)
    lg = jnp.where(is_group, logits, ninf)
    gmax = jnp.max(lg, axis=1, keepdims=True)
    pg_top = 1.0 / jnp.sum(jnp.exp(lg - gmax), axis=1, keepdims=True)
    g_idx = first(lg == gmax) - float(N_EXPERTS)

    in_group = (col < N_EXPERTS) & ((col // EXPERTS_PER_GROUP).astype(F32) == g_idx)
    le = jnp.where(in_group, logits, ninf)
    m1 = jnp.max(le, axis=1, keepdims=True)
    i1 = first(le == m1)
    le2 = jnp.where(colf == i1, ninf, le)
    m2 = jnp.max(le2, axis=1, keepdims=True)
    i2 = first(le2 == m2)
    ee = jnp.exp(le - m1)
    pe = ee / jnp.sum(ee, axis=1, keepdims=True)
    p1 = jnp.sum(jnp.where(colf == i1, pe, 0.0), axis=1, keepdims=True)
    p2 = jnp.sum(jnp.where(colf == i2, pe, 0.0), axis=1, keepdims=True)
    den = p1 + p2
    gates = jnp.where(colf == i1, p1 / den * pg_top,
                      jnp.where(colf == i2, p2 / den * pg_top, 0.0))
    return gates, g_idx


def _local_sort(g_idx):
    tm = g_idx.shape[0]
    lane = lax.broadcasted_iota(jnp.int32, (tm, LANES), 1)
    member = jnp.where(lane.astype(F32) == g_idx, 1.0, 0.0)
    r_i = lax.broadcasted_iota(jnp.int32, (tm, tm), 0)
    c_i = lax.broadcasted_iota(jnp.int32, (tm, tm), 1)
    before = jnp.where(c_i < r_i, 1.0, 0.0).astype(BF16)
    rank = jnp.sum(member * _dot(before, member.astype(BF16)), axis=1, keepdims=True)
    count = jnp.sum(member, axis=0, keepdims=True)
    padded = jnp.floor((count + (MOE_UNIT - 1)) * (1.0 / MOE_UNIT)) * MOE_UNIT
    lane1 = lax.broadcasted_iota(jnp.int32, (1, LANES), 1)
    offset = jnp.zeros((1, LANES), F32)
    start = jnp.zeros((1, 1), F32)
    for g in range(N_GROUPS):
        offset = offset + jnp.where((lane1 == g) | (lane1 == N_GROUPS + g), start, 0.0)
        start = start + jnp.sum(jnp.where(lane1 == g, padded, 0.0), axis=1, keepdims=True)
    dest = jnp.sum(member * offset, axis=1, keepdims=True) + rank
    rows = jnp.where(lane1 == N_GROUPS - 1, LS_TILE - offset, padded)
    units = jnp.where(lane1 < N_GROUPS, rows, offset) * (1.0 / MOE_UNIT)
    return dest, units.astype(jnp.int32)


def _pack_gates(gates):
    hi = gates.astype(BF16).astype(F32)
    rest = gates - hi
    mid = rest.astype(BF16).astype(F32)
    lo = rest - mid
    packed = hi + pltpu.roll(mid, GATE_TERM_STRIDE, axis=1) + pltpu.roll(lo, 2 * GATE_TERM_STRIDE, axis=1)
    return packed.astype(BF16)


def _unpack_gate(packed, expert):
    lane = lax.broadcasted_iota(jnp.int32, packed.shape, 1)
    terms = jnp.where(lane % GATE_TERM_STRIDE == expert, packed.astype(F32), 0.0)
    return jnp.sum(terms, axis=1, keepdims=True)


def _out_xattn_kernel(n_parts, *refs):
    h_ref = refs[0]
    part_refs = refs[1:1 + n_parts]
    w_refs = refs[1 + n_parts:1 + 2 * n_parts]
    (gx_ref, qk_ref, vo_ref, gf_ref, wr_ref, br_ref,
     h2_ref, xn_ls_ref, dest_ref, units_ref) = refs[1 + 2 * n_parts:]

    h1 = h_ref[...]
    for p_ref, w_ref in zip(part_refs, w_refs):
        h1 = h1 + _dot(p_ref[...], w_ref[...])

    xn = _rmsnorm(h1, gx_ref[...]).astype(BF16)
    s_all = _dot(xn, qk_ref[0])
    probs = []
    for hd in range(XA_HEADS):
        s = s_all[:, hd * MEM_LEN:(hd + 1) * MEM_LEN]
        e = jnp.exp(s - jnp.max(s, axis=1, keepdims=True))
        probs.append((e / jnp.sum(e, axis=1, keepdims=True)).astype(BF16))
    h2 = h1 + _dot(jnp.concatenate(probs, axis=1), vo_ref[0])
    h2_ref[...] = h2

    xf = _rmsnorm(h2, gf_ref[...])
    x_hi = xf.astype(BF16)
    x_lo = (xf - x_hi.astype(F32)).astype(BF16)
    both = _dot(x_hi, wr_ref[...])
    logits = both[:, :LANES] + both[:, LANES:] + _dot(x_lo, wr_ref[:, :LANES]) + br_ref[...]

    gates, g_idx = _router_gates(logits)
    dest, units = _local_sort(g_idx)
    tm = dest.shape[0]
    dest_ref[...] = jnp.broadcast_to(dest, (tm, LANES))
    units_ref[0] = jnp.broadcast_to(units, (SUBLANES, LANES))
    dest_row = jnp.transpose(jnp.broadcast_to(dest, (tm, LANES)))[0:1, :]
    slot = lax.broadcasted_iota(jnp.int32, (LS_TILE, tm), 0).astype(F32)
    place = jnp.where(slot == dest_row, 1.0, 0.0).astype(BF16)
    routed = jnp.concatenate([x_hi, _pack_gates(gates)], axis=1)
    xn_ls_ref[...] = _dot(place, routed).astype(BF16)


def _out_xattn(h, parts, weights, gx, qk_all, vo_all, gf, wr, br, layer):
    tm = ROW_TILE
    row = lambda n, rows=tm: pl.BlockSpec((rows, n), lambda i: (i, 0))
    qk_spec = pl.BlockSpec((1, D_MODEL, XA_HEADS * MEM_LEN), lambda i: (layer, 0, 0))
    vo_spec = pl.BlockSpec((1, XA_HEADS * MEM_LEN, D_MODEL), lambda i: (layer, 0, 0))
    in_specs = ([row(D_MODEL)] + [row(p.shape[1]) for p in parts]
                + [_full(w.shape) for w in weights]
                + [_full((1, D_MODEL)), qk_spec, vo_spec, _full((1, D_MODEL)),
                   _full((D_MODEL, 2 * LANES)), _full((1, LANES))])
    return pl.pallas_call(
        functools.partial(_out_xattn_kernel, len(parts)),
        grid=(N_ROW_TILES,),
        in_specs=in_specs,
        out_specs=[row(D_MODEL), row(LS_WIDTH, LS_TILE), row(LANES),
                   pl.BlockSpec((1, SUBLANES, LANES), lambda i: (i, 0, 0))],
        out_shape=[jax.ShapeDtypeStruct((SEQ, D_MODEL), F32),
                   jax.ShapeDtypeStruct((N_ROW_TILES * LS_TILE, LS_WIDTH), BF16),
                   jax.ShapeDtypeStruct((SEQ, LANES), F32),
                   jax.ShapeDtypeStruct((N_ROW_TILES, SUBLANES, LANES), jnp.int32)],
        compiler_params=_params("parallel"),
        name="out_xattn_router",
    )(h, *parts, *weights, gx, qk_all, vo_all, gf, wr, br)


def _route_index_kernel(units_ref, unit_of_ref, tile_group_ref):
    u = units_ref[...].astype(F32)
    nt = u.shape[0]
    lane = lax.broadcasted_iota(jnp.int32, (nt, LANES), 1)
    lane1 = lax.broadcasted_iota(jnp.int32, (1, LANES), 1)
    count = jnp.where(lane < N_GROUPS, u, 0.0)
    offset = pltpu.roll(u, LANES - N_GROUPS, axis=1)
    r_i = lax.broadcasted_iota(jnp.int32, (nt, nt), 0)
    c_i = lax.broadcasted_iota(jnp.int32, (nt, nt), 1)
    before = _dot(jnp.where(c_i < r_i, 1.0, 0.0).astype(BF16), count.astype(BF16))
    total = jnp.sum(count, axis=0, keepdims=True)
    group_slots = jnp.floor((total + (SORT_UNITS - 1)) * (1.0 / SORT_UNITS)) * SORT_UNITS

    first_slot = jnp.zeros((1, LANES), F32)
    tile_first = (lane1 * SORT_UNITS).astype(F32)
    tile_group = jnp.full((1, LANES), -1.0, F32)
    start = jnp.zeros((1, 1), F32)
    for g in range(N_GROUPS):
        size = jnp.sum(jnp.where(lane1 == g, group_slots, 0.0), axis=1, keepdims=True)
        first_slot = first_slot + jnp.where(lane1 == g, start, 0.0)
        tile_group = jnp.where((tile_first >= start) & (tile_first < start + size),
                               float(g), tile_group)
        start = start + size
    tile_group_ref[...] = tile_group.astype(jnp.int32)

    tile_id = lax.broadcasted_iota(jnp.int32, (nt, LANES), 0).astype(F32)
    run_lo = first_slot + before
    shift = tile_id * LS_UNITS + offset - run_lo

    def column(a):
        return jnp.concatenate(
            [(pltpu.roll(a, LANES - g, axis=1) if g else a)[:, 0:1] for g in range(N_GROUPS)],
            axis=0)

    lo = column(run_lo)
    hi = lo + column(count)
    sh = column(shift)
    for chunk in range(unit_of_ref.shape[0]):
        slot = (lane1 + chunk * LANES).astype(F32)
        hit = (slot >= lo) & (slot < hi)
        unit = jnp.sum(jnp.where(hit, slot + sh, 0.0), axis=0, keepdims=True)
        found = jnp.sum(jnp.where(hit, 1.0, 0.0), axis=0, keepdims=True)
        unit_of_ref[chunk:chunk + 1, :] = jnp.where(found > 0.0, unit, -1.0).astype(jnp.int32)


def _route_index(units):
    n_slots = N_SORT_TILES * SORT_UNITS
    unit_of, tile_group = pl.pallas_call(
        _route_index_kernel,
        out_shape=[jax.ShapeDtypeStruct((n_slots // LANES, LANES), jnp.int32),
                   jax.ShapeDtypeStruct((1, LANES), jnp.int32)],
        name="moe_route_index",
    )(units)
    return unit_of.reshape(n_slots), tile_group[0, :N_SORT_TILES]


def _moe_kernel(unit_of_ref, tile_group_ref, xn_hbm, wg_ref, wu_ref, wd_ref, y_hbm,
                xbuf, ybuf, acc_ref, wg_bf, wu_bf, wd_bf, gather_sem, scatter_sem):
    j = pl.program_id(0)
    n_tiles = pl.num_programs(0)
    slot = j % 2

    def unit_rows(k):
        return pl.ds(pl.multiple_of(k * MOE_UNIT, MOE_UNIT), MOE_UNIT)

    def gather_copy(s, k, u):
        return pltpu.make_async_copy(xn_hbm.at[unit_rows(u), :], xbuf.at[s, unit_rows(k), :],
                                     gather_sem.at[s])

    def scatter_copy(s, k, u):
        return pltpu.make_async_copy(ybuf.at[s, unit_rows(k), :], y_hbm.at[unit_rows(u), :],
                                     scatter_sem.at[s])

    def is_full(tile):
        return unit_of_ref[tile * SORT_UNITS + SORT_UNITS - 1] >= 0

    def for_units(tile, full_fn, partial_fn):
        def run(fn, unroll):
            def unit(k, _):
                fn(k, unit_of_ref[tile * SORT_UNITS + k])
                return 0

            lax.fori_loop(0, SORT_UNITS, unit, 0, unroll=unroll)

        @pl.when(is_full(tile))
        def _():
            run(full_fn, 8)

        @pl.when(jnp.logical_not(is_full(tile)))
        def _():
            def guarded(k, u):
                @pl.when(u >= 0)
                def _():
                    full_fn(k, u)

                if partial_fn is not None:
                    @pl.when(u < 0)
                    def _():
                        partial_fn(k)

            run(guarded, 1)

    def start_gather(tile, s):
        def zero_fill(k):
            xbuf[s, unit_rows(k), :] = jnp.zeros((MOE_UNIT, LS_WIDTH), BF16)

        for_units(tile, lambda k, u: gather_copy(s, k, u).start(), zero_fill)

    def start_scatter(tile, s):
        for_units(tile, lambda k, u: scatter_copy(s, k, u).start(), None)

    def wait_units(tile, whole_copy, unit_copy):
        @pl.when(is_full(tile))
        def _():
            whole_copy.wait()

        @pl.when(jnp.logical_not(is_full(tile)))
        def _():
            def unit(k, _):
                u = unit_of_ref[tile * SORT_UNITS + k]

                @pl.when(u >= 0)
                def _():
                    unit_copy(k, u).wait()

                return 0

            lax.fori_loop(0, SORT_UNITS, unit, 0)

    def wait_gather(tile, s):
        whole = pltpu.make_async_copy(xn_hbm.at[pl.ds(0, ROW_TILE), :], xbuf.at[s], gather_sem.at[s])
        wait_units(tile, whole, lambda k, u: gather_copy(s, k, u))

    def wait_scatter(tile, s):
        whole = pltpu.make_async_copy(ybuf.at[s], y_hbm.at[pl.ds(0, ROW_TILE), :], scatter_sem.at[s])
        wait_units(tile, whole, lambda k, u: scatter_copy(s, k, u))

    @pl.when(j == 0)
    def _():
        start_gather(0, 0)

    @pl.when(j + 1 < n_tiles)
    def _():
        start_gather(j + 1, 1 - slot)

    grp = tile_group_ref[j]
    new_group = (j == 0) | (grp != tile_group_ref[jnp.maximum(j - 1, 0)])

    @pl.when((grp >= 0) & new_group)
    def _():
        wg_bf[...] = wg_ref[...].astype(BF16)
        wu_bf[...] = wu_ref[...].astype(BF16)
        wd_bf[...] = wd_ref[...].astype(BF16)

    wait_gather(j, slot)

    @pl.when(j >= 2)
    def _():
        wait_scatter(j - 2, slot)

    @pl.when(grp >= 0)
    def _():
        xn = xbuf[slot, :, :D_MODEL]
        packed_gates = xbuf[slot, :, D_MODEL:]
        for e in range(EXPERTS_PER_GROUP):
            gate = _unpack_gate(packed_gates, grp * EXPERTS_PER_GROUP + e)
            hg = _dot(xn, wg_bf[e])
            hu = _dot(xn, wu_bf[e])
            act = (hg * jax.nn.sigmoid(hg)) * hu * gate
            out = _dot(act.astype(BF16), wd_bf[e])
            if e == 0:
                acc_ref[...] = out
            else:
                acc_ref[...] += out
        y = acc_ref[...]
        y_hi = y.astype(BF16)
        ybuf[slot, :, :D_MODEL] = y_hi
        ybuf[slot, :, D_MODEL:] = (y - y_hi.astype(F32)).astype(BF16)

    start_scatter(j, slot)

    @pl.when(j == n_tiles - 1)
    def _():
        wait_scatter(j - 1, 1 - slot)
        wait_scatter(j, slot)


def _moe(unit_of, tile_group, xn_ls, wg, wu, wd, layer):
    group_of = lambda j, unit_of, tile_group: jnp.where(tile_group[j] < 0, N_GROUPS - 1,
                                                        tile_group[j])
    up_spec = pl.BlockSpec((None, EXPERTS_PER_GROUP, D_MODEL, D_EXPERT),
                           lambda j, u, t: (layer, group_of(j, u, t), 0, 0))
    down_spec = pl.BlockSpec((None, EXPERTS_PER_GROUP, D_EXPERT, D_MODEL),
                             lambda j, u, t: (layer, group_of(j, u, t), 0, 0))
    any_spec = pl.BlockSpec(memory_space=pl.ANY)
    return pl.pallas_call(
        _moe_kernel,
        grid_spec=pltpu.PrefetchScalarGridSpec(
            num_scalar_prefetch=2,
            grid=(N_SORT_TILES,),
            in_specs=[any_spec, up_spec, up_spec, down_spec],
            out_specs=any_spec,
            scratch_shapes=[pltpu.VMEM((2, ROW_TILE, LS_WIDTH), BF16),
                            pltpu.VMEM((2, ROW_TILE, 2 * D_MODEL), BF16),
                            pltpu.VMEM((ROW_TILE, D_MODEL), F32),
                            pltpu.VMEM((EXPERTS_PER_GROUP, D_MODEL, D_EXPERT), BF16),
                            pltpu.VMEM((EXPERTS_PER_GROUP, D_MODEL, D_EXPERT), BF16),
                            pltpu.VMEM((EXPERTS_PER_GROUP, D_EXPERT, D_MODEL), BF16),
                            pltpu.SemaphoreType.DMA((2,)),
                            pltpu.SemaphoreType.DMA((2,))]),
        out_shape=jax.ShapeDtypeStruct((N_ROW_TILES * LS_TILE, 2 * D_MODEL), BF16),
        compiler_params=_params("arbitrary"),
        name="moe_experts",
    )(unit_of, tile_group, xn_ls, wg, wu, wd)


def _add_unsorted(h_ref, y_ls_ref, dest_ref):
    tm = h_ref.shape[0]
    slot = lax.broadcasted_iota(jnp.int32, (tm, LS_TILE), 1).astype(F32)
    pick = jnp.where(slot == dest_ref[:, 0:1], 1.0, 0.0).astype(BF16)
    y = _dot(pick, y_ls_ref[...])
    return h_ref[...] + (y[:, :D_MODEL] + y[:, D_MODEL:])


def _unsort_final_kernel(h_hbm, y_ls_hbm, dest_ref, gfin_ref, o_ref, h_buf, y_buf, sem):
    i = pl.program_id(0)
    n = pl.num_programs(0)

    def copies(step, slot):
        return (pltpu.make_async_copy(h_hbm.at[pl.ds(step * ROW_TILE, ROW_TILE), :],
                                      h_buf.at[slot], sem.at[0, slot]),
                pltpu.make_async_copy(y_ls_hbm.at[pl.ds(step * LS_TILE, LS_TILE), :],
                                      y_buf.at[slot], sem.at[1, slot]))

    def start(step):
        for c in copies(step, step % UNSORT_SLOTS):
            c.start()

    @pl.when(i == 0)
    def _():
        for step in range(UNSORT_SLOTS - 1):
            start(step)

    @pl.when(i + UNSORT_SLOTS - 1 < n)
    def _():
        start(i + UNSORT_SLOTS - 1)

    slot = i % UNSORT_SLOTS
    for c in copies(i, slot):
        c.wait()
    out = _add_unsorted(h_buf.at[slot], y_buf.at[slot], dest_ref)
    o_ref[...] = _rmsnorm(out, gfin_ref[...])


def _unsort_specs():
    row = lambda n: pl.BlockSpec((ROW_TILE, n), lambda i: (i, 0))
    return [row(D_MODEL), pl.BlockSpec((LS_TILE, 2 * D_MODEL), lambda i: (i, 0)), row(LANES)]


def _unsort_final(h, y_ls, dest, gfin):
    any_spec = pl.BlockSpec(memory_space=pl.ANY)
    return pl.pallas_call(
        _unsort_final_kernel,
        grid=(N_ROW_TILES,),
        in_specs=[any_spec, any_spec, pl.BlockSpec((ROW_TILE, LANES), lambda i: (i, 0)),
                  _full((1, D_MODEL))],
        out_specs=pl.BlockSpec((ROW_TILE, D_MODEL), lambda i: (i, 0)),
        out_shape=jax.ShapeDtypeStruct((SEQ, D_MODEL), F32),
        scratch_shapes=[pltpu.VMEM((UNSORT_SLOTS, ROW_TILE, D_MODEL), F32),
                        pltpu.VMEM((UNSORT_SLOTS, LS_TILE, 2 * D_MODEL), BF16),
                        pltpu.SemaphoreType.DMA((2, UNSORT_SLOTS))],
        compiler_params=_params("arbitrary"),
        name="moe_unsort_final_norm",
    )(h, y_ls, dest, gfin)


def _l1_mixer_kernel(h_ref, y_ls_ref, dest_ref, g_ref, w_in_ref, vg_ref, vb_ref, ws_ref, bs_ref,
                     x_ref, o_ref, w_in_bf):
    tm = h_ref.shape[0]

    @pl.when(pl.program_id(0) == 0)
    def _():
        w_in_bf[...] = w_in_ref[...].astype(BF16)

    x = _add_unsorted(h_ref, y_ls_ref, dest_ref)
    x_ref[...] = x
    xn = _rmsnorm(x, g_ref[...]).astype(BF16)
    p = _dot(xn, w_in_bf[...])
    p = 0.5 * p * (1.0 + lax.erf(p * (2.0 ** -0.5)))
    u = p[:, :D_MODEL]
    v = _layernorm(p[:, D_MODEL:], vg_ref[...], vb_ref[...]).astype(BF16)
    t_idx = lax.broadcasted_iota(jnp.int32, (GM_CHUNK, GM_CHUNK), 0)
    s_idx = lax.broadcasted_iota(jnp.int32, (GM_CHUNK, GM_CHUNK), 1)
    for g in range(GM_GROUPS):
        cols = slice(g * LANES, (g + 1) * LANES)
        wmix = jnp.where(t_idx >= s_idx, ws_ref[g], 0.0).astype(BF16)
        for c in range(tm // GM_CHUNK):
            rows = slice(c * GM_CHUNK, (c + 1) * GM_CHUNK)
            mixed = _dot(wmix, v[rows, cols]) + bs_ref[g]
            o_ref[rows, cols] = (u[rows, cols] * mixed).astype(BF16)


def _l1_mixer(h, y_ls, dest, g, w_in, vg, vb, ws, bs_b):
    row = pl.BlockSpec((ROW_TILE, D_MODEL), lambda i: (i, 0))
    return pl.pallas_call(
        _l1_mixer_kernel,
        grid=(N_ROW_TILES,),
        in_specs=_unsort_specs() + [
            _full((1, D_MODEL)),
            pl.BlockSpec((None, D_MODEL, 2 * D_MODEL), lambda i: (0, 0, 0)),
            _full((1, D_MODEL)), _full((1, D_MODEL)), _full((GM_GROUPS, GM_CHUNK, GM_CHUNK)),
            _full((GM_GROUPS, GM_CHUNK, LANES))],
        out_specs=[row, row],
        out_shape=[jax.ShapeDtypeStruct((SEQ, D_MODEL), F32),
                   jax.ShapeDtypeStruct((SEQ, D_MODEL), BF16)],
        scratch_shapes=[pltpu.VMEM((D_MODEL, 2 * D_MODEL), BF16)],
        compiler_params=_params("arbitrary"),
        name="l1_mixer",
    )(h, y_ls, dest, g, w_in, vg, vb, ws, bs_b)


def _router_weights(we, wg, be, bg):
    pad = LANES - N_EXPERTS - N_GROUPS
    w = jnp.concatenate([we, wg, jnp.zeros((D_MODEL, pad), F32)], axis=1)
    b = jnp.concatenate([be, bg, jnp.zeros((pad,), F32)])[None, :]
    w_hi = w.astype(BF16)
    w_lo = (w - w_hi.astype(F32)).astype(BF16)
    return jnp.concatenate([w_hi, w_lo], axis=1), b


def kernel(x, mem, mem_norm_g, norm_mix, norm_xa, norm_ffn, final_norm_g, ab_w_in, ab_conv_w, ab_conv_b, ab_cnorm_g, ab_cnorm_b, ab_w_out, c_w_in, c_vnorm_g, c_vnorm_b, c_ws, c_bs, c_w_out, xa_wq, xa_wk, xa_wv, xa_wo, rt_wg, rt_bg, rt_we, rt_be, ex_w_gate, ex_w_up, ex_w_down):
    assert x.shape == (1, SEQ, D_MODEL) and mem.shape == (1, MEM_LEN, D_MODEL)
    bf = lambda a: a.astype(BF16)
    r1 = lambda a: a.reshape(1, -1)
    h = x[0]

    memory_len = XA_HEADS * MEM_LEN
    qk_all = _mem_weights(_mem_qk_kernel, "mem_qk", mem[0], r1(mem_norm_g), xa_wk, xa_wq,
                          D_MODEL, memory_len)
    vo_all = _mem_weights(_mem_vo_kernel, "mem_vo", mem[0], r1(mem_norm_g), xa_wv, xa_wo,
                          memory_len, D_MODEL)

    def tail(h, parts, weights, i):
        wr, br = _router_weights(rt_we[i], rt_wg[i], rt_be[i], rt_bg[i])
        h2, xn_ls, dest, units = _out_xattn(
            h, parts, weights, r1(norm_xa[i]), qk_all, vo_all, r1(norm_ffn[i]), wr, br, i)
        unit_of, tile_group = _route_index(units[:, 0, :])
        y_ls = _moe(unit_of, tile_group, xn_ls, ex_w_gate, ex_w_up, ex_w_down, i)
        return h2, y_ls, dest

    conv_out, q, kt, v = _l0_inproj(
        h, r1(norm_mix[0]), ab_w_in, ab_conv_w[0], r1(ab_conv_b[0]), r1(ab_cnorm_g[0]),
        r1(ab_cnorm_b[0]))
    sb_out = _sb_attention(q, kt, v)
    w_out = bf(ab_w_out[0])
    h2, y_ls, dest = tail(h, [conv_out, sb_out], [w_out[:CONV_CH], w_out[CONV_CH:]], 0)

    bs_b = jnp.broadcast_to(c_bs[0][:, :, None], (GM_GROUPS, GM_CHUNK, LANES))
    h, gated = _l1_mixer(h2, y_ls, dest, r1(norm_mix[1]), c_w_in, r1(c_vnorm_g[0]),
                         r1(c_vnorm_b[0]), c_ws[0], bs_b)
    h2, y_ls, dest = tail(h, [gated], [bf(c_w_out[0])], 1)
    return _unsort_final(h2, y_ls, dest, r1(final_norm_g))[None]
```

```python
import functools

import jax
import jax.numpy as jnp
from jax import lax
from jax.experimental import pallas as pl
from jax.experimental.pallas import tpu as pltpu

D_MODEL = 1024
SEQ = 16384
MEM_LEN = 256
EPS = 1e-6
CONV_CH = 512
CONV_WIDTH = 31
SB_HEADS = 8
SB_HEAD_DIM = 64
SB_DIM = SB_HEADS * SB_HEAD_DIM
GM_GROUPS = 8
GM_CHUNK = 128
XA_HEADS = 4
XA_HEAD_DIM = 256
N_GROUPS = 4
EXPERTS_PER_GROUP = 4
N_EXPERTS = N_GROUPS * EXPERTS_PER_GROUP
D_EXPERT = 256

LANES = 128
SUBLANES = 8
ROW_TILE = 512
SB_TQ = 256
SB_TK = 256
SB_SUB = 8
SB_LOG_UNDERFLOW = -104.0
SB_HIDDEN_SCORE = -1e30
CONV_HALO = 32
CONV_ROWS = 64
MOE_UNIT = 16
N_ROW_TILES = SEQ // ROW_TILE
LS_TILE = ROW_TILE + N_GROUPS * MOE_UNIT
LS_UNITS = LS_TILE // MOE_UNIT
SORT_UNITS = ROW_TILE // MOE_UNIT
N_SORT_TILES = N_ROW_TILES * LS_UNITS // SORT_UNITS + N_GROUPS
GATE_TERM_STRIDE = 32
LS_WIDTH = D_MODEL + LANES
UNSORT_SLOTS = 3
VMEM_LIMIT = 56 * 1024 * 1024

BF16 = jnp.bfloat16
F32 = jnp.float32


def _params(*semantics):
    return pltpu.CompilerParams(dimension_semantics=semantics, vmem_limit_bytes=VMEM_LIMIT)


def _dot(a, b):
    return jnp.dot(a, b, preferred_element_type=F32)


def _rmsnorm(x, g):
    return x * lax.rsqrt(jnp.mean(x * x, axis=-1, keepdims=True) + EPS) * g


def _layernorm(x, g, b):
    mu = jnp.mean(x, axis=-1, keepdims=True)
    xc = x - mu
    var = jnp.mean(xc * xc, axis=-1, keepdims=True)
    return xc * lax.rsqrt(var + EPS) * g + b


def _full(shape):
    return pl.BlockSpec(shape, lambda *_: (0,) * len(shape))


def _mem_qk_kernel(mem_ref, g_ref, wk_ref, wq_ref, qk_ref):
    memn = _rmsnorm(mem_ref[...], g_ref[...]).astype(BF16)
    k = _dot(memn, wk_ref[0].astype(BF16)).astype(BF16)
    wq = wq_ref[0].astype(BF16)
    for hd in range(XA_HEADS):
        sl = slice(hd * XA_HEAD_DIM, (hd + 1) * XA_HEAD_DIM)
        qk = lax.dot_general(wq[:, sl], k[:, sl], (((1,), (1,)), ((), ())),
                             preferred_element_type=F32)
        qk_ref[0, :, hd * MEM_LEN:(hd + 1) * MEM_LEN] = (qk * (XA_HEAD_DIM ** -0.5)).astype(BF16)


def _mem_vo_kernel(mem_ref, g_ref, wv_ref, wo_ref, vo_ref):
    memn = _rmsnorm(mem_ref[...], g_ref[...]).astype(BF16)
    v = _dot(memn, wv_ref[0].astype(BF16)).astype(BF16)
    wo = wo_ref[0].astype(BF16)
    for hd in range(XA_HEADS):
        sl = slice(hd * XA_HEAD_DIM, (hd + 1) * XA_HEAD_DIM)
        vo_ref[0, hd * MEM_LEN:(hd + 1) * MEM_LEN, :] = _dot(v[:, sl], wo[sl, :]).astype(BF16)


def _mem_weights(kernel_fn, name, mem, g, w_mem, w_attn, out_rows, out_cols):
    depth = w_mem.shape[0]
    wspec = pl.BlockSpec((1, D_MODEL, D_MODEL), lambda i: (i, 0, 0))
    return pl.pallas_call(
        kernel_fn,
        grid=(depth,),
        in_specs=[_full((MEM_LEN, D_MODEL)), _full((1, D_MODEL)), wspec, wspec],
        out_specs=pl.BlockSpec((1, out_rows, out_cols), lambda i: (i, 0, 0)),
        out_shape=jax.ShapeDtypeStruct((depth, out_rows, out_cols), BF16),
        compiler_params=_params("arbitrary"),
        name=name,
    )(mem, g, w_mem, w_attn)


def _l0_inproj_kernel(x_ref, g_ref, w_in_ref, cw_ref, cb_ref, lg_ref, lb_ref,
                      conv_ref, q_ref, kt_ref, v_ref,
                      ext_ref, y_ref, w_ag_ref, w_q_ref, w_kt_ref, w_v_ref):
    tm = x_ref.shape[0]

    @pl.when(pl.program_id(0) == 0)
    def _():
        ext_ref[0:CONV_HALO, :] = jnp.zeros((CONV_HALO, CONV_CH), F32)
        q_lo, k_lo, v_lo = 2 * CONV_CH, 2 * CONV_CH + SB_DIM, 2 * CONV_CH + 2 * SB_DIM
        w_ag_ref[...] = w_in_ref[:, :q_lo].astype(BF16)
        w_q_ref[...] = w_in_ref[:, q_lo:k_lo].astype(BF16)
        w_kt_ref[...] = jnp.transpose(w_in_ref[:, k_lo:v_lo]).astype(BF16)
        w_v_ref[...] = w_in_ref[:, v_lo:].astype(BF16)

    xn = _rmsnorm(x_ref[...], g_ref[...]).astype(BF16)
    ag = _dot(xn, w_ag_ref[...])
    ext_ref[CONV_HALO:, :] = ag[:, :CONV_CH] * jax.nn.sigmoid(ag[:, CONV_CH:])
    q_ref[...] = (_dot(xn, w_q_ref[...]) * (SB_HEAD_DIM ** -0.5)).astype(BF16)
    kt_ref[...] = lax.dot_general(w_kt_ref[...], xn, (((1,), (1,)), ((), ())),
                                  preferred_element_type=F32).astype(BF16)
    v_ref[...] = _dot(xn, w_v_ref[...]).astype(BF16)

    first_tap = CONV_HALO - (CONV_WIDTH - 1)
    for r0 in range(0, tm, CONV_ROWS):
        for j in range(CONV_CH // LANES):
            lanes = slice(j * LANES, (j + 1) * LANES)
            win = ext_ref[r0:r0 + CONV_ROWS + CONV_HALO, lanes]
            acc = jnp.zeros((CONV_ROWS, LANES), F32) + cb_ref[:, lanes]
            for sub in range(SUBLANES):
                offsets = [o for o in range(first_tap, first_tap + CONV_WIDTH)
                           if o % SUBLANES == sub]
                shifted = pltpu.roll(win, win.shape[0] - sub, axis=0) if sub else win
                for o in offsets:
                    tap = cw_ref[o - first_tap:o - first_tap + 1, lanes]
                    acc = acc + shifted[o - sub:o - sub + CONV_ROWS, :] * tap
            y_ref[r0:r0 + CONV_ROWS, lanes] = acc
    y = _layernorm(y_ref[...], lg_ref[...], lb_ref[...])
    conv_ref[...] = (y * jax.nn.sigmoid(y)).astype(BF16)
    ext_ref[0:CONV_HALO, :] = ext_ref[tm:tm + CONV_HALO, :]


def _l0_inproj(x, g, w_in, cw, cb, lg, lb):
    tm = ROW_TILE
    row = lambda n: pl.BlockSpec((tm, n), lambda i: (i, 0))
    return pl.pallas_call(
        _l0_inproj_kernel,
        grid=(N_ROW_TILES,),
        in_specs=[row(D_MODEL), _full((1, D_MODEL)),
                  pl.BlockSpec((None, D_MODEL, 2 * CONV_CH + 3 * SB_DIM), lambda i: (0, 0, 0)),
                  _full((CONV_WIDTH, CONV_CH)), _full((1, CONV_CH)),
                  _full((1, CONV_CH)), _full((1, CONV_CH))],
        out_specs=[row(CONV_CH), row(SB_DIM), pl.BlockSpec((SB_DIM, tm), lambda i: (0, i)),
                   row(SB_DIM)],
        out_shape=[jax.ShapeDtypeStruct((SEQ, CONV_CH), BF16),
                   jax.ShapeDtypeStruct((SEQ, SB_DIM), BF16),
                   jax.ShapeDtypeStruct((SB_DIM, SEQ), BF16),
                   jax.ShapeDtypeStruct((SEQ, SB_DIM), BF16)],
        scratch_shapes=[pltpu.VMEM((tm + CONV_HALO, CONV_CH), F32),
                        pltpu.VMEM((tm, CONV_CH), F32),
                        pltpu.VMEM((D_MODEL, 2 * CONV_CH), BF16),
                        pltpu.VMEM((D_MODEL, SB_DIM), BF16),
                        pltpu.VMEM((SB_DIM, D_MODEL), BF16),
                        pltpu.VMEM((D_MODEL, SB_DIM), BF16)],
        compiler_params=_params("arbitrary"),
        name="l0_inproj_conv",
    )(x, g, w_in, cw, cb, lg, lb)


def _sb_kernel(q_ref, kt_ref, v_ref, o_ref, acc_ref, carry_ref):
    tq, tk = SB_TQ, SB_TK
    lane = lax.broadcasted_iota(jnp.int32, (tq, LANES), 1)

    def heads(q2):
        zero = jnp.zeros_like(q2)
        return (jnp.where(lane < SB_HEAD_DIM, q2, zero), jnp.where(lane >= SB_HEAD_DIM, q2, zero))

    q_blocks = [pl.program_id(1) * SB_SUB + b for b in range(SB_SUB)]
    q_heads = [heads(q_ref[b * tq:(b + 1) * tq, :]) for b in range(SB_SUB)]
    jj = lax.broadcasted_iota(jnp.int32, (tk, tk), 0)
    ss = lax.broadcasted_iota(jnp.int32, (tk, tk), 1)
    minus_later = jnp.where(jj > ss, -1.0, 0.0).astype(BF16)
    minus_later2 = jnp.concatenate([minus_later, minus_later], axis=0)

    def scores(q_head, kb, valid):
        z = _dot(q_head, kt_ref[:, pl.ds(pl.multiple_of(kb * tk, tk), tk)])
        return z if valid is None else jnp.where(valid, z, SB_HIDDEN_SCORE)

    def stay(z):
        sp = jnp.maximum(z, 0.0) + jnp.log(1.0 + jnp.exp(-jnp.abs(z)))
        hi = sp.astype(BF16)
        return sp, hi, (sp - hi.astype(F32)).astype(BF16)

    def later_sum(hi, lo):
        return _dot(jnp.concatenate([hi, lo], axis=1), minus_later2)

    def weighted_values(z, sp, after, kb):
        w = jnp.exp((z - sp) + after)
        return _dot(w.astype(BF16), v_ref[pl.ds(pl.multiple_of(kb * tk, tk), tk), :])

    def block(q_head, kb):
        z = scores(q_head, kb, None)
        sp, hi, lo = stay(z)
        pv = weighted_values(z, sp, later_sum(hi, lo), kb)
        return pv, jnp.sum(sp, axis=1, keepdims=True)

    row = lax.broadcasted_iota(jnp.int32, (tq, tk), 0)
    col = lax.broadcasted_iota(jnp.int32, (tq, tk), 1)
    chains = [(b, h, kb, valid) for b, qb in enumerate(q_blocks) for h in range(2)
              for kb, valid in ((qb, col < row), (jnp.maximum(qb - 1, 0), None))]
    zs = [scores(q_heads[b][h], kb, valid) for b, h, kb, valid in chains]
    stays = [stay(z) for z in zs]
    afters = [later_sum(hi, lo) for _, hi, lo in stays]
    pvs = [weighted_values(z, sp, after, kb)
           for z, (sp, _, _), after, (_, _, kb, _) in zip(zs, stays, afters, chains)]
    sums = [jnp.sum(sp, axis=1, keepdims=True) for sp, _, _ in stays]
    for b, qb in enumerate(q_blocks):
        for h in range(2):
            c = 4 * b + 2 * h
            (pv_diag, pv_prev), (sp_diag, sp_prev) = pvs[c:c + 2], sums[c:c + 2]
            acc_ref[b, h] = pv_diag + jnp.where(qb > 0, jnp.exp(-sp_diag), 0.0) * pv_prev
            carry_ref[b, h] = jnp.broadcast_to(-(sp_diag + sp_prev), (tq, LANES))

    for b, qb in enumerate(q_blocks):
        def alive():
            return jnp.max(carry_ref[b]) > SB_LOG_UNDERFLOW

        def cond(state):
            kb, go = state
            return (kb >= 0) & go

        def body(state):
            kb, _ = state
            for h in range(2):
                pv, sp_sum = block(q_heads[b][h], kb)
                carry = carry_ref[b, h]
                acc_ref[b, h] += jnp.exp(carry) * pv
                carry_ref[b, h] = carry - sp_sum
            return kb - 1, alive()

        lax.while_loop(cond, body, (qb - 2, alive()))
        o_ref[b * tq:(b + 1) * tq, :] = jnp.where(lane < SB_HEAD_DIM, acc_ref[b, 0],
                                                  acc_ref[b, 1]).astype(BF16)


def _sb_attention(q, kt, v):
    rows = SB_SUB * SB_TQ
    state = pltpu.VMEM((SB_SUB, 2, SB_TQ, LANES), F32)
    return pl.pallas_call(
        _sb_kernel,
        grid=(SB_DIM // LANES, SEQ // rows),
        in_specs=[pl.BlockSpec((rows, LANES), lambda p, i: (i, p)),
                  pl.BlockSpec((LANES, SEQ), lambda p, i: (p, 0)),
                  pl.BlockSpec((SEQ, LANES), lambda p, i: (0, p))],
        out_specs=pl.BlockSpec((rows, LANES), lambda p, i: (i, p)),
        out_shape=jax.ShapeDtypeStruct((SEQ, SB_DIM), BF16),
        scratch_shapes=[state, state],
        compiler_params=_params("parallel", "parallel"),
        name="sb_attention",
    )(q, kt, v)


def _router_gates(logits):
    col = lax.broadcasted_iota(jnp.int32, logits.shape, 1)
    colf = col.astype(F32)
    ninf = -jnp.inf
    first = lambda hit: jnp.min(jnp.where(hit, colf, float(LANES)), axis=1, keepdims=True)

    is_group = (col >= N_EXPERTS) & (col < N_EXPERTS + N_GROUPS)
    lg = jnp.where(is_group, logits, ninf)
    gmax = jnp.max(lg, axis=1, keepdims=True)
    pg_top = 1.0 / jnp.sum(jnp.exp(lg - gmax), axis=1, keepdims=True)
    g_idx = first(lg == gmax) - float(N_EXPERTS)

    in_group = (col < N_EXPERTS) & ((col // EXPERTS_PER_GROUP).astype(F32) == g_idx)
    le = jnp.where(in_group, logits, ninf)
    m1 = jnp.max(le, axis=1, keepdims=True)
    i1 = first(le == m1)
    le2 = jnp.where(colf == i1, ninf, le)
    m2 = jnp.max(le2, axis=1, keepdims=True)
    i2 = first(le2 == m2)
    ee = jnp.exp(le - m1)
    pe = ee / jnp.sum(ee, axis=1, keepdims=True)
    p1 = jnp.sum(jnp.where(colf == i1, pe, 0.0), axis=1, keepdims=True)
    p2 = jnp.sum(jnp.where(colf == i2, pe, 0.0), axis=1, keepdims=True)
    den = p1 + p2
    gates = jnp.where(colf == i1, p1 / den * pg_top,
                      jnp.where(colf == i2, p2 / den * pg_top, 0.0))
    return gates, g_idx


def _local_sort(g_idx):
    tm = g_idx.shape[0]
    lane = lax.broadcasted_iota(jnp.int32, (tm, LANES), 1)
    member = jnp.where(lane.astype(F32) == g_idx, 1.0, 0.0)
    r_i = lax.broadcasted_iota(jnp.int32, (tm, tm), 0)
    c_i = lax.broadcasted_iota(jnp.int32, (tm, tm), 1)
    before = jnp.where(c_i < r_i, 1.0, 0.0).astype(BF16)
    rank = jnp.sum(member * _dot(before, member.astype(BF16)), axis=1, keepdims=True)
    count = jnp.sum(member, axis=0, keepdims=True)
    padded = jnp.floor((count + (MOE_UNIT - 1)) * (1.0 / MOE_UNIT)) * MOE_UNIT
    lane1 = lax.broadcasted_iota(jnp.int32, (1, LANES), 1)
    offset = jnp.zeros((1, LANES), F32)
    start = jnp.zeros((1, 1), F32)
    for g in range(N_GROUPS):
        offset = offset + jnp.where((lane1 == g) | (lane1 == N_GROUPS + g), start, 0.0)
        start = start + jnp.sum(jnp.where(lane1 == g, padded, 0.0), axis=1, keepdims=True)
    dest = jnp.sum(member * offset, axis=1, keepdims=True) + rank
    rows = jnp.where(lane1 == N_GROUPS - 1, LS_TILE - offset, padded)
    units = jnp.where(lane1 < N_GROUPS, rows, offset) * (1.0 / MOE_UNIT)
    return dest, units.astype(jnp.int32)


def _pack_gates(gates):
    hi = gates.astype(BF16).astype(F32)
    rest = gates - hi
    mid = rest.astype(BF16).astype(F32)
    lo = rest - mid
    packed = hi + pltpu.roll(mid, GATE_TERM_STRIDE, axis=1) + pltpu.roll(lo, 2 * GATE_TERM_STRIDE, axis=1)
    return packed.astype(BF16)


def _unpack_gate(packed, expert):
    lane = lax.broadcasted_iota(jnp.int32, packed.shape, 1)
    terms = jnp.where(lane % GATE_TERM_STRIDE == expert, packed.astype(F32), 0.0)
    return jnp.sum(terms, axis=1, keepdims=True)


def _out_xattn_kernel(n_parts, *refs):
    h_ref = refs[0]
    part_refs = refs[1:1 + n_parts]
    w_refs = refs[1 + n_parts:1 + 2 * n_parts]
    (gx_ref, qk_ref, vo_ref, gf_ref, wr_ref, br_ref,
     h2_ref, xn_ls_ref, dest_ref, units_ref) = refs[1 + 2 * n_parts:]

    h1 = h_ref[...]
    for p_ref, w_ref in zip(part_refs, w_refs):
        h1 = h1 + _dot(p_ref[...], w_ref[...])

    xn = _rmsnorm(h1, gx_ref[...]).astype(BF16)
    s_all = _dot(xn, qk_ref[0])
    probs = []
    for hd in range(XA_HEADS):
        s = s_all[:, hd * MEM_LEN:(hd + 1) * MEM_LEN]
        e = jnp.exp(s - jnp.max(s, axis=1, keepdims=True))
        probs.append((e / jnp.sum(e, axis=1, keepdims=True)).astype(BF16))
    h2 = h1 + _dot(jnp.concatenate(probs, axis=1), vo_ref[0])
    h2_ref[...] = h2

    xf = _rmsnorm(h2, gf_ref[...])
    x_hi = xf.astype(BF16)
    x_lo = (xf - x_hi.astype(F32)).astype(BF16)
    both = _dot(x_hi, wr_ref[...])
    logits = both[:, :LANES] + both[:, LANES:] + _dot(x_lo, wr_ref[:, :LANES]) + br_ref[...]

    gates, g_idx = _router_gates(logits)
    dest, units = _local_sort(g_idx)
    tm = dest.shape[0]
    dest_ref[...] = jnp.broadcast_to(dest, (tm, LANES))
    units_ref[0] = jnp.broadcast_to(units, (SUBLANES, LANES))
    dest_row = jnp.transpose(jnp.broadcast_to(dest, (tm, LANES)))[0:1, :]
    slot = lax.broadcasted_iota(jnp.int32, (LS_TILE, tm), 0).astype(F32)
    place = jnp.where(slot == dest_row, 1.0, 0.0).astype(BF16)
    routed = jnp.concatenate([x_hi, _pack_gates(gates)], axis=1)
    xn_ls_ref[...] = _dot(place, routed).astype(BF16)


def _out_xattn(h, parts, weights, gx, qk_all, vo_all, gf, wr, br, layer):
    tm = ROW_TILE
    row = lambda n, rows=tm: pl.BlockSpec((rows, n), lambda i: (i, 0))
    qk_spec = pl.BlockSpec((1, D_MODEL, XA_HEADS * MEM_LEN), lambda i: (layer, 0, 0))
    vo_spec = pl.BlockSpec((1, XA_HEADS * MEM_LEN, D_MODEL), lambda i: (layer, 0, 0))
    in_specs = ([row(D_MODEL)] + [row(p.shape[1]) for p in parts]
                + [_full(w.shape) for w in weights]
                + [_full((1, D_MODEL)), qk_spec, vo_spec, _full((1, D_MODEL)),
                   _full((D_MODEL, 2 * LANES)), _full((1, LANES))])
    return pl.pallas_call(
        functools.partial(_out_xattn_kernel, len(parts)),
        grid=(N_ROW_TILES,),
        in_specs=in_specs,
        out_specs=[row(D_MODEL), row(LS_WIDTH, LS_TILE), row(LANES),
                   pl.BlockSpec((1, SUBLANES, LANES), lambda i: (i, 0, 0))],
        out_shape=[jax.ShapeDtypeStruct((SEQ, D_MODEL), F32),
                   jax.ShapeDtypeStruct((N_ROW_TILES * LS_TILE, LS_WIDTH), BF16),
                   jax.ShapeDtypeStruct((SEQ, LANES), F32),
                   jax.ShapeDtypeStruct((N_ROW_TILES, SUBLANES, LANES), jnp.int32)],
        compiler_params=_params("parallel"),
        name="out_xattn_router",
    )(h, *parts, *weights, gx, qk_all, vo_all, gf, wr, br)


def _route_index_kernel(units_ref, unit_of_ref, tile_group_ref):
    u = units_ref[...].astype(F32)
    nt = u.shape[0]
    lane = lax.broadcasted_iota(jnp.int32, (nt, LANES), 1)
    lane1 = lax.broadcasted_iota(jnp.int32, (1, LANES), 1)
    count = jnp.where(lane < N_GROUPS, u, 0.0)
    offset = pltpu.roll(u, LANES - N_GROUPS, axis=1)
    r_i = lax.broadcasted_iota(jnp.int32, (nt, nt), 0)
    c_i = lax.broadcasted_iota(jnp.int32, (nt, nt), 1)
    before = _dot(jnp.where(c_i < r_i, 1.0, 0.0).astype(BF16), count.astype(BF16))
    total = jnp.sum(count, axis=0, keepdims=True)
    group_slots = jnp.floor((total + (SORT_UNITS - 1)) * (1.0 / SORT_UNITS)) * SORT_UNITS

    first_slot = jnp.zeros((1, LANES), F32)
    tile_first = (lane1 * SORT_UNITS).astype(F32)
    tile_group = jnp.full((1, LANES), -1.0, F32)
    start = jnp.zeros((1, 1), F32)
    for g in range(N_GROUPS):
        size = jnp.sum(jnp.where(lane1 == g, group_slots, 0.0), axis=1, keepdims=True)
        first_slot = first_slot + jnp.where(lane1 == g, start, 0.0)
        tile_group = jnp.where((tile_first >= start) & (tile_first < start + size),
                               float(g), tile_group)
        start = start + size
    tile_group_ref[...] = tile_group.astype(jnp.int32)

    tile_id = lax.broadcasted_iota(jnp.int32, (nt, LANES), 0).astype(F32)
    run_lo = first_slot + before
    shift = tile_id * LS_UNITS + offset - run_lo

    def column(a):
        return jnp.concatenate(
            [(pltpu.roll(a, LANES - g, axis=1) if g else a)[:, 0:1] for g in range(N_GROUPS)],
            axis=0)

    lo = column(run_lo)
    hi = lo + column(count)
    sh = column(shift)
    for chunk in range(unit_of_ref.shape[0]):
        slot = (lane1 + chunk * LANES).astype(F32)
        hit = (slot >= lo) & (slot < hi)
        unit = jnp.sum(jnp.where(hit, slot + sh, 0.0), axis=0, keepdims=True)
        found = jnp.sum(jnp.where(hit, 1.0, 0.0), axis=0, keepdims=True)
        unit_of_ref[chunk:chunk + 1, :] = jnp.where(found > 0.0, unit, -1.0).astype(jnp.int32)


def _route_index(units):
    n_slots = N_SORT_TILES * SORT_UNITS
    unit_of, tile_group = pl.pallas_call(
        _route_index_kernel,
        out_shape=[jax.ShapeDtypeStruct((n_slots // LANES, LANES), jnp.int32),
                   jax.ShapeDtypeStruct((1, LANES), jnp.int32)],
        name="moe_route_index",
    )(units)
    return unit_of.reshape(n_slots), tile_group[0, :N_SORT_TILES]


def _moe_kernel(unit_of_ref, tile_group_ref, xn_hbm, wg_ref, wu_ref, wd_ref, y_hbm,
                xbuf, ybuf, acc_ref, wg_bf, wu_bf, wd_bf, gather_sem, scatter_sem):
    j = pl.program_id(0)
    n_tiles = pl.num_programs(0)
    slot = j % 2

    def unit_rows(k):
        return pl.ds(pl.multiple_of(k * MOE_UNIT, MOE_UNIT), MOE_UNIT)

    def gather_copy(s, k, u):
        return pltpu.make_async_copy(xn_hbm.at[unit_rows(u), :], xbuf.at[s, unit_rows(k), :],
                                     gather_sem.at[s])

    def scatter_copy(s, k, u):
        return pltpu.make_async_copy(ybuf.at[s, unit_rows(k), :], y_hbm.at[unit_rows(u), :],
                                     scatter_sem.at[s])

    def is_full(tile):
        return unit_of_ref[tile * SORT_UNITS + SORT_UNITS - 1] >= 0

    def for_units(tile, full_fn, partial_fn):
        def run(fn, unroll):
            def unit(k, _):
                fn(k, unit_of_ref[tile * SORT_UNITS + k])
                return 0

            lax.fori_loop(0, SORT_UNITS, unit, 0, unroll=unroll)

        @pl.when(is_full(tile))
        def _():
            run(full_fn, 8)

        @pl.when(jnp.logical_not(is_full(tile)))
        def _():
            def guarded(k, u):
                @pl.when(u >= 0)
                def _():
                    full_fn(k, u)

                if partial_fn is not None:
                    @pl.when(u < 0)
                    def _():
                        partial_fn(k)

            run(guarded, 1)

    def start_gather(tile, s):
        def zero_fill(k):
            xbuf[s, unit_rows(k), :] = jnp.zeros((MOE_UNIT, LS_WIDTH), BF16)

        for_units(tile, lambda k, u: gather_copy(s, k, u).start(), zero_fill)

    def start_scatter(tile, s):
        for_units(tile, lambda k, u: scatter_copy(s, k, u).start(), None)

    def wait_units(tile, whole_copy, unit_copy):
        @pl.when(is_full(tile))
        def _():
            whole_copy.wait()

        @pl.when(jnp.logical_not(is_full(tile)))
        def _():
            def unit(k, _):
                u = unit_of_ref[tile * SORT_UNITS + k]

                @pl.when(u >= 0)
                def _():
                    unit_copy(k, u).wait()

                return 0

            lax.fori_loop(0, SORT_UNITS, unit, 0)

    def wait_gather(tile, s):
        whole = pltpu.make_async_copy(xn_hbm.at[pl.ds(0, ROW_TILE), :], xbuf.at[s], gather_sem.at[s])
        wait_units(tile, whole, lambda k, u: gather_copy(s, k, u))

    def wait_scatter(tile, s):
        whole = pltpu.make_async_copy(ybuf.at[s], y_hbm.at[pl.ds(0, ROW_TILE), :], scatter_sem.at[s])
        wait_units(tile, whole, lambda k, u: scatter_copy(s, k, u))

    @pl.when(j == 0)
    def _():
        start_gather(0, 0)

    @pl.when(j + 1 < n_tiles)
    def _():
        start_gather(j + 1, 1 - slot)

    grp = tile_group_ref[j]
    new_group = (j == 0) | (grp != tile_group_ref[jnp.maximum(j - 1, 0)])

    @pl.when((grp >= 0) & new_group)
    def _():
        wg_bf[...] = wg_ref[...].astype(BF16)
        wu_bf[...] = wu_ref[...].astype(BF16)
        wd_bf[...] = wd_ref[...].astype(BF16)

    wait_gather(j, slot)

    @pl.when(j >= 2)
    def _():
        wait_scatter(j - 2, slot)

    @pl.when(grp >= 0)
    def _():
        xn = xbuf[slot, :, :D_MODEL]
        packed_gates = xbuf[slot, :, D_MODEL:]
        for e in range(EXPERTS_PER_GROUP):
            gate = _unpack_gate(packed_gates, grp * EXPERTS_PER_GROUP + e)
            hg = _dot(xn, wg_bf[e])
            hu = _dot(xn, wu_bf[e])
            act = (hg * jax.nn.sigmoid(hg)) * hu * gate
            out = _dot(act.astype(BF16), wd_bf[e])
            if e == 0:
                acc_ref[...] = out
            else:
                acc_ref[...] += out
        y = acc_ref[...]
        y_hi = y.astype(BF16)
        ybuf[slot, :, :D_MODEL] = y_hi
        ybuf[slot, :, D_MODEL:] = (y - y_hi.astype(F32)).astype(BF16)

    start_scatter(j, slot)

    @pl.when(j == n_tiles - 1)
    def _():
        wait_scatter(j - 1, 1 - slot)
        wait_scatter(j, slot)


def _moe(unit_of, tile_group, xn_ls, wg, wu, wd, layer):
    group_of = lambda j, unit_of, tile_group: jnp.where(tile_group[j] < 0, N_GROUPS - 1,
                                                        tile_group[j])
    up_spec = pl.BlockSpec((None, EXPERTS_PER_GROUP, D_MODEL, D_EXPERT),
                           lambda j, u, t: (layer, group_of(j, u, t), 0, 0))
    down_spec = pl.BlockSpec((None, EXPERTS_PER_GROUP, D_EXPERT, D_MODEL),
                             lambda j, u, t: (layer, group_of(j, u, t), 0, 0))
    any_spec = pl.BlockSpec(memory_space=pl.ANY)
    return pl.pallas_call(
        _moe_kernel,
        grid_spec=pltpu.PrefetchScalarGridSpec(
            num_scalar_prefetch=2,
            grid=(N_SORT_TILES,),
            in_specs=[any_spec, up_spec, up_spec, down_spec],
            out_specs=any_spec,
            scratch_shapes=[pltpu.VMEM((2, ROW_TILE, LS_WIDTH), BF16),
                            pltpu.VMEM((2, ROW_TILE, 2 * D_MODEL), BF16),
                            pltpu.VMEM((ROW_TILE, D_MODEL), F32),
                            pltpu.VMEM((EXPERTS_PER_GROUP, D_MODEL, D_EXPERT), BF16),
                            pltpu.VMEM((EXPERTS_PER_GROUP, D_MODEL, D_EXPERT), BF16),
                            pltpu.VMEM((EXPERTS_PER_GROUP, D_EXPERT, D_MODEL), BF16),
                            pltpu.SemaphoreType.DMA((2,)),
                            pltpu.SemaphoreType.DMA((2,))]),
        out_shape=jax.ShapeDtypeStruct((N_ROW_TILES * LS_TILE, 2 * D_MODEL), BF16),
        compiler_params=_params("arbitrary"),
        name="moe_experts",
    )(unit_of, tile_group, xn_ls, wg, wu, wd)


def _add_unsorted(h_ref, y_ls_ref, dest_ref):
    tm = h_ref.shape[0]
    slot = lax.broadcasted_iota(jnp.int32, (tm, LS_TILE), 1).astype(F32)
    pick = jnp.where(slot == dest_ref[:, 0:1], 1.0, 0.0).astype(BF16)
    y = _dot(pick, y_ls_ref[...])
    return h_ref[...] + (y[:, :D_MODEL] + y[:, D_MODEL:])


def _unsort_final_kernel(h_hbm, y_ls_hbm, dest_ref, gfin_ref, o_ref, h_buf, y_buf, sem):
    i = pl.program_id(0)
    n = pl.num_programs(0)

    def copies(step, slot):
        return (pltpu.make_async_copy(h_hbm.at[pl.ds(step * ROW_TILE, ROW_TILE), :],
                                      h_buf.at[slot], sem.at[0, slot]),
                pltpu.make_async_copy(y_ls_hbm.at[pl.ds(step * LS_TILE, LS_TILE), :],
                                      y_buf.at[slot], sem.at[1, slot]))

    def start(step):
        for c in copies(step, step % UNSORT_SLOTS):
            c.start()

    @pl.when(i == 0)
    def _():
        for step in range(UNSORT_SLOTS - 1):
            start(step)

    @pl.when(i + UNSORT_SLOTS - 1 < n)
    def _():
        start(i + UNSORT_SLOTS - 1)

    slot = i % UNSORT_SLOTS
    for c in copies(i, slot):
        c.wait()
    out = _add_unsorted(h_buf.at[slot], y_buf.at[slot], dest_ref)
    o_ref[...] = _rmsnorm(out, gfin_ref[...])


def _unsort_specs():
    row = lambda n: pl.BlockSpec((ROW_TILE, n), lambda i: (i, 0))
    return [row(D_MODEL), pl.BlockSpec((LS_TILE, 2 * D_MODEL), lambda i: (i, 0)), row(LANES)]


def _unsort_final(h, y_ls, dest, gfin):
    any_spec = pl.BlockSpec(memory_space=pl.ANY)
    return pl.pallas_call(
        _unsort_final_kernel,
        grid=(N_ROW_TILES,),
        in_specs=[any_spec, any_spec, pl.BlockSpec((ROW_TILE, LANES), lambda i: (i, 0)),
                  _full((1, D_MODEL))],
        out_specs=pl.BlockSpec((ROW_TILE, D_MODEL), lambda i: (i, 0)),
        out_shape=jax.ShapeDtypeStruct((SEQ, D_MODEL), F32),
        scratch_shapes=[pltpu.VMEM((UNSORT_SLOTS, ROW_TILE, D_MODEL), F32),
                        pltpu.VMEM((UNSORT_SLOTS, LS_TILE, 2 * D_MODEL), BF16),
                        pltpu.SemaphoreType.DMA((2, UNSORT_SLOTS))],
        compiler_params=_params("arbitrary"),
        name="moe_unsort_final_norm",
    )(h, y_ls, dest, gfin)


def _l1_mixer_kernel(h_ref, y_ls_ref, dest_ref, g_ref, w_in_ref, vg_ref, vb_ref, ws_ref, bs_ref,
                     x_ref, o_ref, w_in_bf):
    tm = h_ref.shape[0]

    @pl.when(pl.program_id(0) == 0)
    def _():
        w_in_bf[...] = w_in_ref[...].astype(BF16)

    x = _add_unsorted(h_ref, y_ls_ref, dest_ref)
    x_ref[...] = x
    xn = _rmsnorm(x, g_ref[...]).astype(BF16)
    gelu = lambda p: 0.5 * p * (1.0 + lax.erf(p * (2.0 ** -0.5)))
    v = _layernorm(gelu(_dot(xn, w_in_bf[:, D_MODEL:])), vg_ref[...], vb_ref[...]).astype(BF16)
    u = gelu(_dot(xn, w_in_bf[:, :D_MODEL]))
    t_idx = lax.broadcasted_iota(jnp.int32, (GM_CHUNK, GM_CHUNK), 0)
    s_idx = lax.broadcasted_iota(jnp.int32, (GM_CHUNK, GM_CHUNK), 1)
    for g in range(GM_GROUPS):
        cols = slice(g * LANES, (g + 1) * LANES)
        wmix = jnp.where(t_idx >= s_idx, ws_ref[g], 0.0).astype(BF16)
        for c in range(tm // GM_CHUNK):
            rows = slice(c * GM_CHUNK, (c + 1) * GM_CHUNK)
            mixed = _dot(wmix, v[rows, cols]) + bs_ref[g]
            o_ref[rows, cols] = (u[rows, cols] * mixed).astype(BF16)


def _l1_mixer(h, y_ls, dest, g, w_in, vg, vb, ws, bs_b):
    row = pl.BlockSpec((ROW_TILE, D_MODEL), lambda i: (i, 0))
    return pl.pallas_call(
        _l1_mixer_kernel,
        grid=(N_ROW_TILES,),
        in_specs=_unsort_specs() + [
            _full((1, D_MODEL)),
            pl.BlockSpec((None, D_MODEL, 2 * D_MODEL), lambda i: (0, 0, 0)),
            _full((1, D_MODEL)), _full((1, D_MODEL)), _full((GM_GROUPS, GM_CHUNK, GM_CHUNK)),
            _full((GM_GROUPS, GM_CHUNK, LANES))],
        out_specs=[row, row],
        out_shape=[jax.ShapeDtypeStruct((SEQ, D_MODEL), F32),
                   jax.ShapeDtypeStruct((SEQ, D_MODEL), BF16)],
        scratch_shapes=[pltpu.VMEM((D_MODEL, 2 * D_MODEL), BF16)],
        compiler_params=_params("arbitrary"),
        name="l1_mixer",
    )(h, y_ls, dest, g, w_in, vg, vb, ws, bs_b)


def _router_weights(we, wg, be, bg):
    pad = LANES - N_EXPERTS - N_GROUPS
    w = jnp.concatenate([we, wg, jnp.zeros((D_MODEL, pad), F32)], axis=1)
    b = jnp.concatenate([be, bg, jnp.zeros((pad,), F32)])[None, :]
    w_hi = w.astype(BF16)
    w_lo = (w - w_hi.astype(F32)).astype(BF16)
    return jnp.concatenate([w_hi, w_lo], axis=1), b


def kernel(x, mem, mem_norm_g, norm_mix, norm_xa, norm_ffn, final_norm_g, ab_w_in, ab_conv_w, ab_conv_b, ab_cnorm_g, ab_cnorm_b, ab_w_out, c_w_in, c_vnorm_g, c_vnorm_b, c_ws, c_bs, c_w_out, xa_wq, xa_wk, xa_wv, xa_wo, rt_wg, rt_bg, rt_we, rt_be, ex_w_gate, ex_w_up, ex_w_down):
    assert x.shape == (1, SEQ, D_MODEL) and mem.shape == (1, MEM_LEN, D_MODEL)
    bf = lambda a: a.astype(BF16)
    r1 = lambda a: a.reshape(1, -1)
    h = x[0]

    memory_len = XA_HEADS * MEM_LEN
    qk_all = _mem_weights(_mem_qk_kernel, "mem_qk", mem[0], r1(mem_norm_g), xa_wk, xa_wq,
                          D_MODEL, memory_len)
    vo_all = _mem_weights(_mem_vo_kernel, "mem_vo", mem[0], r1(mem_norm_g), xa_wv, xa_wo,
                          memory_len, D_MODEL)

    def tail(h, parts, weights, i):
        wr, br = _router_weights(rt_we[i], rt_wg[i], rt_be[i], rt_bg[i])
        h2, xn_ls, dest, units = _out_xattn(
            h, parts, weights, r1(norm_xa[i]), qk_all, vo_all, r1(norm_ffn[i]), wr, br, i)
        unit_of, tile_group = _route_index(units[:, 0, :])
        y_ls = _moe(unit_of, tile_group, xn_ls, ex_w_gate, ex_w_up, ex_w_down, i)
        return h2, y_ls, dest

    conv_out, q, kt, v = _l0_inproj(
        h, r1(norm_mix[0]), ab_w_in, ab_conv_w[0], r1(ab_conv_b[0]), r1(ab_cnorm_g[0]),
        r1(ab_cnorm_b[0]))
    sb_out = _sb_attention(q, kt, v)
    w_out = bf(ab_w_out[0])
    h2, y_ls, dest = tail(h, [conv_out, sb_out], [w_out[:CONV_CH], w_out[CONV_CH:]], 0)

    bs_b = jnp.broadcast_to(c_bs[0][:, :, None], (GM_GROUPS, GM_CHUNK, LANES))
    h, gated = _l1_mixer(h2, y_ls, dest, r1(norm_mix[1]), c_w_in, r1(c_vnorm_g[0]),
                         r1(c_vnorm_b[0]), c_ws[0], bs_b)
    h2, y_ls, dest = tail(h, [gated], [bf(c_w_out[0])], 1)
    return _unsort_final(h2, y_ls, dest, r1(final_norm_g))[None]
```

```python
import functools

import jax
import jax.numpy as jnp
from jax import lax
from jax.experimental import pallas as pl
from jax.experimental.pallas import tpu as pltpu

D_MODEL = 1024
SEQ = 16384
MEM_LEN = 256
EPS = 1e-6
CONV_CH = 512
CONV_WIDTH = 31
SB_HEADS = 8
SB_HEAD_DIM = 64
SB_DIM = SB_HEADS * SB_HEAD_DIM
GM_GROUPS = 8
GM_CHUNK = 128
XA_HEADS = 4
XA_HEAD_DIM = 256
N_GROUPS = 4
EXPERTS_PER_GROUP = 4
N_EXPERTS = N_GROUPS * EXPERTS_PER_GROUP
D_EXPERT = 256

LANES = 128
SUBLANES = 8
ROW_TILE = 512
SB_TQ = 256
SB_TK = 256
SB_SUB = 8
SB_LOG_UNDERFLOW = -104.0
SB_HIDDEN_SCORE = -1e30
CONV_HALO = 32
CONV_ROWS = 64
MOE_UNIT = 16
MOE_RUN = 4
N_ROW_TILES = SEQ // ROW_TILE
LS_TILE = ROW_TILE + N_GROUPS * MOE_UNIT
LS_UNITS = LS_TILE // MOE_UNIT
SORT_UNITS = ROW_TILE // MOE_UNIT
N_SORT_TILES = N_ROW_TILES * LS_UNITS // SORT_UNITS + N_GROUPS
GATE_TERM_STRIDE = 32
LS_WIDTH = D_MODEL + LANES
UNSORT_SLOTS = 3
VMEM_LIMIT = 56 * 1024 * 1024

BF16 = jnp.bfloat16
F32 = jnp.float32


def _params(*semantics):
    return pltpu.CompilerParams(dimension_semantics=semantics, vmem_limit_bytes=VMEM_LIMIT)


def _dot(a, b):
    return jnp.dot(a, b, preferred_element_type=F32)


def _rmsnorm(x, g):
    return x * lax.rsqrt(jnp.mean(x * x, axis=-1, keepdims=True) + EPS) * g


def _layernorm(x, g, b):
    mu = jnp.mean(x, axis=-1, keepdims=True)
    xc = x - mu
    var = jnp.mean(xc * xc, axis=-1, keepdims=True)
    return xc * lax.rsqrt(var + EPS) * g + b


def _full(shape):
    return pl.BlockSpec(shape, lambda *_: (0,) * len(shape))


def _mem_qk_kernel(mem_ref, g_ref, wk_ref, wq_ref, qk_ref):
    memn = _rmsnorm(mem_ref[...], g_ref[...]).astype(BF16)
    k = _dot(memn, wk_ref[0].astype(BF16)).astype(BF16)
    wq = wq_ref[0].astype(BF16)
    for hd in range(XA_HEADS):
        sl = slice(hd * XA_HEAD_DIM, (hd + 1) * XA_HEAD_DIM)
        qk = lax.dot_general(wq[:, sl], k[:, sl], (((1,), (1,)), ((), ())),
                             preferred_element_type=F32)
        qk_ref[0, :, hd * MEM_LEN:(hd + 1) * MEM_LEN] = (qk * (XA_HEAD_DIM ** -0.5)).astype(BF16)


def _mem_vo_kernel(mem_ref, g_ref, wv_ref, wo_ref, vo_ref):
    memn = _rmsnorm(mem_ref[...], g_ref[...]).astype(BF16)
    v = _dot(memn, wv_ref[0].astype(BF16)).astype(BF16)
    wo = wo_ref[0].astype(BF16)
    for hd in range(XA_HEADS):
        sl = slice(hd * XA_HEAD_DIM, (hd + 1) * XA_HEAD_DIM)
        vo_ref[0, hd * MEM_LEN:(hd + 1) * MEM_LEN, :] = _dot(v[:, sl], wo[sl, :]).astype(BF16)


def _mem_weights(kernel_fn, name, mem, g, w_mem, w_attn, out_rows, out_cols):
    depth = w_mem.shape[0]
    wspec = pl.BlockSpec((1, D_MODEL, D_MODEL), lambda i: (i, 0, 0))
    return pl.pallas_call(
        kernel_fn,
        grid=(depth,),
        in_specs=[_full((MEM_LEN, D_MODEL)), _full((1, D_MODEL)), wspec, wspec],
        out_specs=pl.BlockSpec((1, out_rows, out_cols), lambda i: (i, 0, 0)),
        out_shape=jax.ShapeDtypeStruct((depth, out_rows, out_cols), BF16),
        compiler_params=_params("arbitrary"),
        name=name,
    )(mem, g, w_mem, w_attn)


def _l0_inproj_kernel(x_ref, g_ref, w_in_ref, cw_ref, cb_ref, lg_ref, lb_ref,
                      conv_ref, q_ref, kt_ref, v_ref,
                      ext_ref, y_ref, w_ag_ref, w_q_ref, w_kt_ref, w_v_ref):
    tm = x_ref.shape[0]

    @pl.when(pl.program_id(0) == 0)
    def _():
        ext_ref[0:CONV_HALO, :] = jnp.zeros((CONV_HALO, CONV_CH), F32)
        q_lo, k_lo, v_lo = 2 * CONV_CH, 2 * CONV_CH + SB_DIM, 2 * CONV_CH + 2 * SB_DIM
        w_ag_ref[...] = w_in_ref[:, :q_lo].astype(BF16)
        w_q_ref[...] = w_in_ref[:, q_lo:k_lo].astype(BF16)
        w_kt_ref[...] = jnp.transpose(w_in_ref[:, k_lo:v_lo]).astype(BF16)
        w_v_ref[...] = w_in_ref[:, v_lo:].astype(BF16)

    xn = _rmsnorm(x_ref[...], g_ref[...]).astype(BF16)
    ag = _dot(xn, w_ag_ref[...])
    ext_ref[CONV_HALO:, :] = ag[:, :CONV_CH] * jax.nn.sigmoid(ag[:, CONV_CH:])
    q_ref[...] = (_dot(xn, w_q_ref[...]) * (SB_HEAD_DIM ** -0.5)).astype(BF16)
    kt_ref[...] = lax.dot_general(w_kt_ref[...], xn, (((1,), (1,)), ((), ())),
                                  preferred_element_type=F32).astype(BF16)
    v_ref[...] = _dot(xn, w_v_ref[...]).astype(BF16)

    first_tap = CONV_HALO - (CONV_WIDTH - 1)
    for r0 in range(0, tm, CONV_ROWS):
        for j in range(CONV_CH // LANES):
            lanes = slice(j * LANES, (j + 1) * LANES)
            win = ext_ref[r0:r0 + CONV_ROWS + CONV_HALO, lanes]
            acc = jnp.zeros((CONV_ROWS, LANES), F32) + cb_ref[:, lanes]
            for sub in range(SUBLANES):
                offsets = [o for o in range(first_tap, first_tap + CONV_WIDTH)
                           if o % SUBLANES == sub]
                shifted = pltpu.roll(win, win.shape[0] - sub, axis=0) if sub else win
                for o in offsets:
                    tap = cw_ref[o - first_tap:o - first_tap + 1, lanes]
                    acc = acc + shifted[o - sub:o - sub + CONV_ROWS, :] * tap
            y_ref[r0:r0 + CONV_ROWS, lanes] = acc
    y = _layernorm(y_ref[...], lg_ref[...], lb_ref[...])
    conv_ref[...] = (y * jax.nn.sigmoid(y)).astype(BF16)
    ext_ref[0:CONV_HALO, :] = ext_ref[tm:tm + CONV_HALO, :]


def _l0_inproj(x, g, w_in, cw, cb, lg, lb):
    tm = ROW_TILE
    row = lambda n: pl.BlockSpec((tm, n), lambda i: (i, 0))
    return pl.pallas_call(
        _l0_inproj_kernel,
        grid=(N_ROW_TILES,),
        in_specs=[row(D_MODEL), _full((1, D_MODEL)),
                  pl.BlockSpec((None, D_MODEL, 2 * CONV_CH + 3 * SB_DIM), lambda i: (0, 0, 0)),
                  _full((CONV_WIDTH, CONV_CH)), _full((1, CONV_CH)),
                  _full((1, CONV_CH)), _full((1, CONV_CH))],
        out_specs=[row(CONV_CH), row(SB_DIM), pl.BlockSpec((SB_DIM, tm), lambda i: (0, i)),
                   row(SB_DIM)],
        out_shape=[jax.ShapeDtypeStruct((SEQ, CONV_CH), BF16),
                   jax.ShapeDtypeStruct((SEQ, SB_DIM), BF16),
                   jax.ShapeDtypeStruct((SB_DIM, SEQ), BF16),
                   jax.ShapeDtypeStruct((SEQ, SB_DIM), BF16)],
        scratch_shapes=[pltpu.VMEM((tm + CONV_HALO, CONV_CH), F32),
                        pltpu.VMEM((tm, CONV_CH), F32),
                        pltpu.VMEM((D_MODEL, 2 * CONV_CH), BF16),
                        pltpu.VMEM((D_MODEL, SB_DIM), BF16),
                        pltpu.VMEM((SB_DIM, D_MODEL), BF16),
                        pltpu.VMEM((D_MODEL, SB_DIM), BF16)],
        compiler_params=_params("arbitrary"),
        name="l0_inproj_conv",
    )(x, g, w_in, cw, cb, lg, lb)


def _sb_kernel(q_ref, kt_ref, v_ref, o_ref, acc_ref, carry_ref):
    tq, tk = SB_TQ, SB_TK
    lane = lax.broadcasted_iota(jnp.int32, (tq, LANES), 1)

    def heads(q2):
        zero = jnp.zeros_like(q2)
        return (jnp.where(lane < SB_HEAD_DIM, q2, zero), jnp.where(lane >= SB_HEAD_DIM, q2, zero))

    q_blocks = [pl.program_id(1) * SB_SUB + b for b in range(SB_SUB)]
    q_heads = [heads(q_ref[b * tq:(b + 1) * tq, :]) for b in range(SB_SUB)]
    jj = lax.broadcasted_iota(jnp.int32, (tk, tk), 0)
    ss = lax.broadcasted_iota(jnp.int32, (tk, tk), 1)
    minus_later = jnp.where(jj > ss, -1.0, 0.0).astype(BF16)
    minus_later2 = jnp.concatenate([minus_later, minus_later], axis=0)

    def scores(q_head, kb, valid):
        z = _dot(q_head, kt_ref[:, pl.ds(pl.multiple_of(kb * tk, tk), tk)])
        return z if valid is None else jnp.where(valid, z, SB_HIDDEN_SCORE)

    def stay(z):
        sp = jnp.maximum(z, 0.0) + jnp.log(1.0 + jnp.exp(-jnp.abs(z)))
        hi = sp.astype(BF16)
        return sp, hi, (sp - hi.astype(F32)).astype(BF16)

    def later_sum(hi, lo):
        return _dot(jnp.concatenate([hi, lo], axis=1), minus_later2)

    def weighted_values(z, sp, after, kb):
        w = jnp.exp((z - sp) + after)
        return _dot(w.astype(BF16), v_ref[pl.ds(pl.multiple_of(kb * tk, tk), tk), :])

    def block(q_head, kb):
        z = scores(q_head, kb, None)
        sp, hi, lo = stay(z)
        pv = weighted_values(z, sp, later_sum(hi, lo), kb)
        return pv, jnp.sum(sp, axis=1, keepdims=True)

    row = lax.broadcasted_iota(jnp.int32, (tq, tk), 0)
    col = lax.broadcasted_iota(jnp.int32, (tq, tk), 1)
    chains = [(b, h, kb, valid) for b, qb in enumerate(q_blocks) for h in range(2)
              for kb, valid in ((qb, col < row), (jnp.maximum(qb - 1, 0), None))]
    zs = [scores(q_heads[b][h], kb, valid) for b, h, kb, valid in chains]
    stays = [stay(z) for z in zs]
    afters = [later_sum(hi, lo) for _, hi, lo in stays]
    pvs = [weighted_values(z, sp, after, kb)
           for z, (sp, _, _), after, (_, _, kb, _) in zip(zs, stays, afters, chains)]
    sums = [jnp.sum(sp, axis=1, keepdims=True) for sp, _, _ in stays]
    for b, qb in enumerate(q_blocks):
        for h in range(2):
            c = 4 * b + 2 * h
            (pv_diag, pv_prev), (sp_diag, sp_prev) = pvs[c:c + 2], sums[c:c + 2]
            acc_ref[b, h] = pv_diag + jnp.where(qb > 0, jnp.exp(-sp_diag), 0.0) * pv_prev
            carry_ref[b, h] = jnp.broadcast_to(-(sp_diag + sp_prev), (tq, LANES))

    for b, qb in enumerate(q_blocks):
        def alive():
            return jnp.max(carry_ref[b]) > SB_LOG_UNDERFLOW

        def cond(state):
            kb, go = state
            return (kb >= 0) & go

        def body(state):
            kb, _ = state
            for h in range(2):
                pv, sp_sum = block(q_heads[b][h], kb)
                carry = carry_ref[b, h]
                acc_ref[b, h] += jnp.exp(carry) * pv
                carry_ref[b, h] = carry - sp_sum
            return kb - 1, alive()

        lax.while_loop(cond, body, (qb - 2, alive()))
        o_ref[b * tq:(b + 1) * tq, :] = jnp.where(lane < SB_HEAD_DIM, acc_ref[b, 0],
                                                  acc_ref[b, 1]).astype(BF16)


def _sb_attention(q, kt, v):
    rows = SB_SUB * SB_TQ
    state = pltpu.VMEM((SB_SUB, 2, SB_TQ, LANES), F32)
    return pl.pallas_call(
        _sb_kernel,
        grid=(SB_DIM // LANES, SEQ // rows),
        in_specs=[pl.BlockSpec((rows, LANES), lambda p, i: (i, p)),
                  pl.BlockSpec((LANES, SEQ), lambda p, i: (p, 0)),
                  pl.BlockSpec((SEQ, LANES), lambda p, i: (0, p))],
        out_specs=pl.BlockSpec((rows, LANES), lambda p, i: (i, p)),
        out_shape=jax.ShapeDtypeStruct((SEQ, SB_DIM), BF16),
        scratch_shapes=[state, state],
        compiler_params=_params("parallel", "parallel"),
        name="sb_attention",
    )(q, kt, v)


def _router_gates(logits):
    col = lax.broadcasted_iota(jnp.int32, logits.shape, 1)
    colf = col.astype(F32)
    ninf = -jnp.inf
    first = lambda hit: jnp.min(jnp.where(hit, colf, float(LANES)), axis=1, keepdims=True)

    is_group = (col >= N_EXPERTS) & (col < N_EXPERTS + N_GROUPS)
    lg = jnp.where(is_group, logits, ninf)
    gmax = jnp.max(lg, axis=1, keepdims=True)
    pg_top = 1.0 / jnp.sum(jnp.exp(lg - gmax), axis=1, keepdims=True)
    g_idx = first(lg == gmax) - float(N_EXPERTS)

    in_group = (col < N_EXPERTS) & ((col // EXPERTS_PER_GROUP).astype(F32) == g_idx)
    le = jnp.where(in_group, logits, ninf)
    m1 = jnp.max(le, axis=1, keepdims=True)
    i1 = first(le == m1)
    le2 = jnp.where(colf == i1, ninf, le)
    m2 = jnp.max(le2, axis=1, keepdims=True)
    i2 = first(le2 == m2)
    ee = jnp.exp(le - m1)
    pe = ee / jnp.sum(ee, axis=1, keepdims=True)
    p1 = jnp.sum(jnp.where(colf == i1, pe, 0.0), axis=1, keepdims=True)
    p2 = jnp.sum(jnp.where(colf == i2, pe, 0.0), axis=1, keepdims=True)
    den = p1 + p2
    gates = jnp.where(colf == i1, p1 / den * pg_top,
                      jnp.where(colf == i2, p2 / den * pg_top, 0.0))
    return gates, g_idx


def _local_sort(g_idx):
    tm = g_idx.shape[0]
    lane = lax.broadcasted_iota(jnp.int32, (tm, LANES), 1)
    member = jnp.where(lane.astype(F32) == g_idx, 1.0, 0.0)
    r_i = lax.broadcasted_iota(jnp.int32, (tm, tm), 0)
    c_i = lax.broadcasted_iota(jnp.int32, (tm, tm), 1)
    before = jnp.where(c_i < r_i, 1.0, 0.0).astype(BF16)
    rank = jnp.sum(member * _dot(before, member.astype(BF16)), axis=1, keepdims=True)
    count = jnp.sum(member, axis=0, keepdims=True)
    padded = jnp.floor((count + (MOE_UNIT - 1)) * (1.0 / MOE_UNIT)) * MOE_UNIT
    lane1 = lax.broadcasted_iota(jnp.int32, (1, LANES), 1)
    offset = jnp.zeros((1, LANES), F32)
    start = jnp.zeros((1, 1), F32)
    for g in range(N_GROUPS):
        offset = offset + jnp.where((lane1 == g) | (lane1 == N_GROUPS + g), start, 0.0)
        start = start + jnp.sum(jnp.where(lane1 == g, padded, 0.0), axis=1, keepdims=True)
    dest = jnp.sum(member * offset, axis=1, keepdims=True) + rank
    rows = jnp.where(lane1 == N_GROUPS - 1, LS_TILE - offset, padded)
    units = jnp.where(lane1 < N_GROUPS, rows, offset) * (1.0 / MOE_UNIT)
    return dest, units.astype(jnp.int32)


def _pack_gates(gates):
    hi = gates.astype(BF16).astype(F32)
    rest = gates - hi
    mid = rest.astype(BF16).astype(F32)
    lo = rest - mid
    packed = hi + pltpu.roll(mid, GATE_TERM_STRIDE, axis=1) + pltpu.roll(lo, 2 * GATE_TERM_STRIDE, axis=1)
    return packed.astype(BF16)


def _unpack_gate(packed, expert):
    lane = lax.broadcasted_iota(jnp.int32, packed.shape, 1)
    terms = jnp.where(lane % GATE_TERM_STRIDE == expert, packed.astype(F32), 0.0)
    return jnp.sum(terms, axis=1, keepdims=True)


def _out_xattn_kernel(n_parts, *refs):
    h_ref = refs[0]
    part_refs = refs[1:1 + n_parts]
    w_refs = refs[1 + n_parts:1 + 2 * n_parts]
    (gx_ref, qk_ref, vo_ref, gf_ref, wr_ref, br_ref,
     h2_ref, xn_ls_ref, dest_ref, units_ref) = refs[1 + 2 * n_parts:]

    h1 = h_ref[...]
    for p_ref, w_ref in zip(part_refs, w_refs):
        h1 = h1 + _dot(p_ref[...], w_ref[...])

    xn = _rmsnorm(h1, gx_ref[...]).astype(BF16)
    s_all = _dot(xn, qk_ref[0])
    probs = []
    for hd in range(XA_HEADS):
        s = s_all[:, hd * MEM_LEN:(hd + 1) * MEM_LEN]
        e = jnp.exp(s - jnp.max(s, axis=1, keepdims=True))
        probs.append((e / jnp.sum(e, axis=1, keepdims=True)).astype(BF16))
    h2 = h1 + _dot(jnp.concatenate(probs, axis=1), vo_ref[0])
    h2_ref[...] = h2

    xf = _rmsnorm(h2, gf_ref[...])
    x_hi = xf.astype(BF16)
    x_lo = (xf - x_hi.astype(F32)).astype(BF16)
    both = _dot(x_hi, wr_ref[...])
    logits = both[:, :LANES] + both[:, LANES:] + _dot(x_lo, wr_ref[:, :LANES]) + br_ref[...]

    gates, g_idx = _router_gates(logits)
    dest, units = _local_sort(g_idx)
    tm = dest.shape[0]
    dest_ref[...] = jnp.broadcast_to(dest, (tm, LANES))
    units_ref[0] = jnp.broadcast_to(units, (SUBLANES, LANES))
    dest_row = jnp.transpose(jnp.broadcast_to(dest, (tm, LANES)))[0:1, :]
    slot = lax.broadcasted_iota(jnp.int32, (LS_TILE, tm), 0).astype(F32)
    place = jnp.where(slot == dest_row, 1.0, 0.0).astype(BF16)
    routed = jnp.concatenate([x_hi, _pack_gates(gates)], axis=1)
    xn_ls_ref[...] = _dot(place, routed).astype(BF16)


def _out_xattn(h, parts, weights, gx, qk_all, vo_all, gf, wr, br, layer):
    tm = ROW_TILE
    row = lambda n, rows=tm: pl.BlockSpec((rows, n), lambda i: (i, 0))
    qk_spec = pl.BlockSpec((1, D_MODEL, XA_HEADS * MEM_LEN), lambda i: (layer, 0, 0))
    vo_spec = pl.BlockSpec((1, XA_HEADS * MEM_LEN, D_MODEL), lambda i: (layer, 0, 0))
    in_specs = ([row(D_MODEL)] + [row(p.shape[1]) for p in parts]
                + [_full(w.shape) for w in weights]
                + [_full((1, D_MODEL)), qk_spec, vo_spec, _full((1, D_MODEL)),
                   _full((D_MODEL, 2 * LANES)), _full((1, LANES))])
    return pl.pallas_call(
        functools.partial(_out_xattn_kernel, len(parts)),
        grid=(N_ROW_TILES,),
        in_specs=in_specs,
        out_specs=[row(D_MODEL), row(LS_WIDTH, LS_TILE), row(LANES),
                   pl.BlockSpec((1, SUBLANES, LANES), lambda i: (i, 0, 0))],
        out_shape=[jax.ShapeDtypeStruct((SEQ, D_MODEL), F32),
                   jax.ShapeDtypeStruct((N_ROW_TILES * LS_TILE, LS_WIDTH), BF16),
                   jax.ShapeDtypeStruct((SEQ, LANES), F32),
                   jax.ShapeDtypeStruct((N_ROW_TILES, SUBLANES, LANES), jnp.int32)],
        compiler_params=_params("parallel"),
        name="out_xattn_router",
    )(h, *parts, *weights, gx, qk_all, vo_all, gf, wr, br)


def _route_index_kernel(units_ref, unit_of_ref, tile_group_ref):
    u = units_ref[...].astype(F32)
    nt = u.shape[0]
    lane = lax.broadcasted_iota(jnp.int32, (nt, LANES), 1)
    lane1 = lax.broadcasted_iota(jnp.int32, (1, LANES), 1)
    count = jnp.where(lane < N_GROUPS, u, 0.0)
    offset = pltpu.roll(u, LANES - N_GROUPS, axis=1)
    r_i = lax.broadcasted_iota(jnp.int32, (nt, nt), 0)
    c_i = lax.broadcasted_iota(jnp.int32, (nt, nt), 1)
    before = _dot(jnp.where(c_i < r_i, 1.0, 0.0).astype(BF16), count.astype(BF16))
    total = jnp.sum(count, axis=0, keepdims=True)
    group_slots = jnp.floor((total + (SORT_UNITS - 1)) * (1.0 / SORT_UNITS)) * SORT_UNITS

    first_slot = jnp.zeros((1, LANES), F32)
    tile_first = (lane1 * SORT_UNITS).astype(F32)
    tile_group = jnp.full((1, LANES), -1.0, F32)
    start = jnp.zeros((1, 1), F32)
    for g in range(N_GROUPS):
        size = jnp.sum(jnp.where(lane1 == g, group_slots, 0.0), axis=1, keepdims=True)
        first_slot = first_slot + jnp.where(lane1 == g, start, 0.0)
        tile_group = jnp.where((tile_first >= start) & (tile_first < start + size),
                               float(g), tile_group)
        start = start + size
    tile_group_ref[...] = tile_group.astype(jnp.int32)

    tile_id = lax.broadcasted_iota(jnp.int32, (nt, LANES), 0).astype(F32)
    run_lo = first_slot + before
    shift = tile_id * LS_UNITS + offset - run_lo

    def column(a):
        return jnp.concatenate(
            [(pltpu.roll(a, LANES - g, axis=1) if g else a)[:, 0:1] for g in range(N_GROUPS)],
            axis=0)

    lo = column(run_lo)
    hi = lo + column(count)
    sh = column(shift)
    for chunk in range(unit_of_ref.shape[0]):
        slot = (lane1 + chunk * LANES).astype(F32)
        hit = (slot >= lo) & (slot < hi)
        unit = jnp.sum(jnp.where(hit, slot + sh, 0.0), axis=0, keepdims=True)
        found = jnp.sum(jnp.where(hit, 1.0, 0.0), axis=0, keepdims=True)
        unit_of_ref[chunk:chunk + 1, :] = jnp.where(found > 0.0, unit, -1.0).astype(jnp.int32)


def _route_index(units):
    n_slots = N_SORT_TILES * SORT_UNITS
    unit_of, tile_group = pl.pallas_call(
        _route_index_kernel,
        out_shape=[jax.ShapeDtypeStruct((n_slots // LANES, LANES), jnp.int32),
                   jax.ShapeDtypeStruct((1, LANES), jnp.int32)],
        name="moe_route_index",
    )(units)
    return unit_of.reshape(n_slots), tile_group[0, :N_SORT_TILES]


def _moe_kernel(unit_of_ref, tile_group_ref, xn_hbm, wg_ref, wu_ref, wd_ref, y_hbm,
                xbuf, ybuf, acc_ref, wg_bf, wu_bf, wd_bf, gather_sem, scatter_sem):
    j = pl.program_id(0)
    n_tiles = pl.num_programs(0)
    slot = j % 2

    def unit_rows(k, n=1):
        return pl.ds(pl.multiple_of(k * MOE_UNIT, MOE_UNIT), n * MOE_UNIT)

    def gather_copy(s, k, u, n=1):
        return pltpu.make_async_copy(xn_hbm.at[unit_rows(u, n), :],
                                     xbuf.at[s, unit_rows(k, n), :], gather_sem.at[s])

    def scatter_copy(s, k, u, n=1):
        return pltpu.make_async_copy(ybuf.at[s, unit_rows(k, n), :],
                                     y_hbm.at[unit_rows(u, n), :], scatter_sem.at[s])

    def is_full(tile):
        return unit_of_ref[tile * SORT_UNITS + SORT_UNITS - 1] >= 0

    def for_units(tile, full_fn, partial_fn):
        first = tile * SORT_UNITS

        @pl.when(is_full(tile))
        def _():
            def run_of_units(r, _):
                k0 = r * MOE_RUN
                u0 = unit_of_ref[first + k0]
                contiguous = unit_of_ref[first + k0 + MOE_RUN - 1] - u0 == MOE_RUN - 1

                @pl.when(contiguous)
                def _():
                    full_fn(k0, u0, MOE_RUN)

                @pl.when(jnp.logical_not(contiguous))
                def _():
                    for t in range(MOE_RUN):
                        full_fn(k0 + t, unit_of_ref[first + k0 + t], 1)

                return 0

            lax.fori_loop(0, SORT_UNITS // MOE_RUN, run_of_units, 0, unroll=2)

        @pl.when(jnp.logical_not(is_full(tile)))
        def _():
            def unit(k, _):
                u = unit_of_ref[first + k]

                @pl.when(u >= 0)
                def _():
                    full_fn(k, u, 1)

                if partial_fn is not None:
                    @pl.when(u < 0)
                    def _():
                        partial_fn(k)

                return 0

            lax.fori_loop(0, SORT_UNITS, unit, 0)

    def start_gather(tile, s):
        def zero_fill(k):
            xbuf[s, unit_rows(k), :] = jnp.zeros((MOE_UNIT, LS_WIDTH), BF16)

        for_units(tile, lambda k, u, n: gather_copy(s, k, u, n).start(), zero_fill)

    def start_scatter(tile, s):
        for_units(tile, lambda k, u, n: scatter_copy(s, k, u, n).start(), None)

    def wait_units(tile, whole_copy, unit_copy):
        @pl.when(is_full(tile))
        def _():
            whole_copy.wait()

        @pl.when(jnp.logical_not(is_full(tile)))
        def _():
            def unit(k, _):
                u = unit_of_ref[tile * SORT_UNITS + k]

                @pl.when(u >= 0)
                def _():
                    unit_copy(k, u).wait()

                return 0

            lax.fori_loop(0, SORT_UNITS, unit, 0)

    def wait_gather(tile, s):
        whole = pltpu.make_async_copy(xn_hbm.at[pl.ds(0, ROW_TILE), :], xbuf.at[s], gather_sem.at[s])
        wait_units(tile, whole, lambda k, u: gather_copy(s, k, u))

    def wait_scatter(tile, s):
        whole = pltpu.make_async_copy(ybuf.at[s], y_hbm.at[pl.ds(0, ROW_TILE), :], scatter_sem.at[s])
        wait_units(tile, whole, lambda k, u: scatter_copy(s, k, u))

    @pl.when(j == 0)
    def _():
        start_gather(0, 0)

    @pl.when(j + 1 < n_tiles)
    def _():
        start_gather(j + 1, 1 - slot)

    grp = tile_group_ref[j]
    new_group = (j == 0) | (grp != tile_group_ref[jnp.maximum(j - 1, 0)])

    @pl.when((grp >= 0) & new_group)
    def _():
        wg_bf[...] = wg_ref[...].astype(BF16)
        wu_bf[...] = wu_ref[...].astype(BF16)
        wd_bf[...] = wd_ref[...].astype(BF16)

    wait_gather(j, slot)

    @pl.when(j >= 2)
    def _():
        wait_scatter(j - 2, slot)

    @pl.when(grp >= 0)
    def _():
        xn = xbuf[slot, :, :D_MODEL]
        packed_gates = xbuf[slot, :, D_MODEL:]
        for e in range(EXPERTS_PER_GROUP):
            gate = _unpack_gate(packed_gates, grp * EXPERTS_PER_GROUP + e)
            hg = _dot(xn, wg_bf[e])
            hu = _dot(xn, wu_bf[e])
            act = (hg * jax.nn.sigmoid(hg)) * hu * gate
            out = _dot(act.astype(BF16), wd_bf[e])
            if e == 0:
                acc_ref[...] = out
            else:
                acc_ref[...] += out
        y = acc_ref[...]
        y_hi = y.astype(BF16)
        ybuf[slot, :, :D_MODEL] = y_hi
        ybuf[slot, :, D_MODEL:] = (y - y_hi.astype(F32)).astype(BF16)

    start_scatter(j, slot)

    @pl.when(j == n_tiles - 1)
    def _():
        wait_scatter(j - 1, 1 - slot)
        wait_scatter(j, slot)


def _moe(unit_of, tile_group, xn_ls, wg, wu, wd, layer):
    group_of = lambda j, unit_of, tile_group: jnp.where(tile_group[j] < 0, N_GROUPS - 1,
                                                        tile_group[j])
    up_spec = pl.BlockSpec((None, EXPERTS_PER_GROUP, D_MODEL, D_EXPERT),
                           lambda j, u, t: (layer, group_of(j, u, t), 0, 0))
    down_spec = pl.BlockSpec((None, EXPERTS_PER_GROUP, D_EXPERT, D_MODEL),
                             lambda j, u, t: (layer, group_of(j, u, t), 0, 0))
    any_spec = pl.BlockSpec(memory_space=pl.ANY)
    return pl.pallas_call(
        _moe_kernel,
        grid_spec=pltpu.PrefetchScalarGridSpec(
            num_scalar_prefetch=2,
            grid=(N_SORT_TILES,),
            in_specs=[any_spec, up_spec, up_spec, down_spec],
            out_specs=any_spec,
            scratch_shapes=[pltpu.VMEM((2, ROW_TILE, LS_WIDTH), BF16),
                            pltpu.VMEM((2, ROW_TILE, 2 * D_MODEL), BF16),
                            pltpu.VMEM((ROW_TILE, D_MODEL), F32),
                            pltpu.VMEM((EXPERTS_PER_GROUP, D_MODEL, D_EXPERT), BF16),
                            pltpu.VMEM((EXPERTS_PER_GROUP, D_MODEL, D_EXPERT), BF16),
                            pltpu.VMEM((EXPERTS_PER_GROUP, D_EXPERT, D_MODEL), BF16),
                            pltpu.SemaphoreType.DMA((2,)),
                            pltpu.SemaphoreType.DMA((2,))]),
        out_shape=jax.ShapeDtypeStruct((N_ROW_TILES * LS_TILE, 2 * D_MODEL), BF16),
        compiler_params=_params("arbitrary"),
        name="moe_experts",
    )(unit_of, tile_group, xn_ls, wg, wu, wd)


def _add_unsorted(h_ref, y_ls_ref, dest_ref):
    tm = h_ref.shape[0]
    slot = lax.broadcasted_iota(jnp.int32, (tm, LS_TILE), 1).astype(F32)
    pick = jnp.where(slot == dest_ref[:, 0:1], 1.0, 0.0).astype(BF16)
    y = _dot(pick, y_ls_ref[...])
    return h_ref[...] + (y[:, :D_MODEL] + y[:, D_MODEL:])


def _unsort_final_kernel(h_hbm, y_ls_hbm, dest_ref, gfin_ref, o_ref, h_buf, y_buf, sem):
    i = pl.program_id(0)
    n = pl.num_programs(0)

    def copies(step, slot):
        return (pltpu.make_async_copy(h_hbm.at[pl.ds(step * ROW_TILE, ROW_TILE), :],
                                      h_buf.at[slot], sem.at[0, slot]),
                pltpu.make_async_copy(y_ls_hbm.at[pl.ds(step * LS_TILE, LS_TILE), :],
                                      y_buf.at[slot], sem.at[1, slot]))

    def start(step):
        for c in copies(step, step % UNSORT_SLOTS):
            c.start()

    @pl.when(i == 0)
    def _():
        for step in range(UNSORT_SLOTS - 1):
            start(step)

    @pl.when(i + UNSORT_SLOTS - 1 < n)
    def _():
        start(i + UNSORT_SLOTS - 1)

    slot = i % UNSORT_SLOTS
    for c in copies(i, slot):
        c.wait()
    out = _add_unsorted(h_buf.at[slot], y_buf.at[slot], dest_ref)
    o_ref[...] = _rmsnorm(out, gfin_ref[...])


def _unsort_specs():
    row = lambda n: pl.BlockSpec((ROW_TILE, n), lambda i: (i, 0))
    return [row(D_MODEL), pl.BlockSpec((LS_TILE, 2 * D_MODEL), lambda i: (i, 0)), row(LANES)]


def _unsort_final(h, y_ls, dest, gfin):
    any_spec = pl.BlockSpec(memory_space=pl.ANY)
    return pl.pallas_call(
        _unsort_final_kernel,
        grid=(N_ROW_TILES,),
        in_specs=[any_spec, any_spec, pl.BlockSpec((ROW_TILE, LANES), lambda i: (i, 0)),
                  _full((1, D_MODEL))],
        out_specs=pl.BlockSpec((ROW_TILE, D_MODEL), lambda i: (i, 0)),
        out_shape=jax.ShapeDtypeStruct((SEQ, D_MODEL), F32),
        scratch_shapes=[pltpu.VMEM((UNSORT_SLOTS, ROW_TILE, D_MODEL), F32),
                        pltpu.VMEM((UNSORT_SLOTS, LS_TILE, 2 * D_MODEL), BF16),
                        pltpu.SemaphoreType.DMA((2, UNSORT_SLOTS))],
        compiler_params=_params("arbitrary"),
        name="moe_unsort_final_norm",
    )(h, y_ls, dest, gfin)


def _l1_mixer_kernel(h_ref, y_ls_ref, dest_ref, g_ref, w_in_ref, vg_ref, vb_ref, ws_ref, bs_ref,
                     x_ref, o_ref, w_in_bf):
    tm = h_ref.shape[0]

    @pl.when(pl.program_id(0) == 0)
    def _():
        w_in_bf[...] = w_in_ref[...].astype(BF16)

    x = _add_unsorted(h_ref, y_ls_ref, dest_ref)
    x_ref[...] = x
    xn = _rmsnorm(x, g_ref[...]).astype(BF16)
    gelu = lambda p: 0.5 * p * (1.0 + lax.erf(p * (2.0 ** -0.5)))
    v = _layernorm(gelu(_dot(xn, w_in_bf[:, D_MODEL:])), vg_ref[...], vb_ref[...]).astype(BF16)
    u = gelu(_dot(xn, w_in_bf[:, :D_MODEL]))
    t_idx = lax.broadcasted_iota(jnp.int32, (GM_CHUNK, GM_CHUNK), 0)
    s_idx = lax.broadcasted_iota(jnp.int32, (GM_CHUNK, GM_CHUNK), 1)
    for g in range(GM_GROUPS):
        cols = slice(g * LANES, (g + 1) * LANES)
        wmix = jnp.where(t_idx >= s_idx, ws_ref[g], 0.0).astype(BF16)
        for c in range(tm // GM_CHUNK):
            rows = slice(c * GM_CHUNK, (c + 1) * GM_CHUNK)
            mixed = _dot(wmix, v[rows, cols]) + bs_ref[g]
            o_ref[rows, cols] = (u[rows, cols] * mixed).astype(BF16)


def _l1_mixer(h, y_ls, dest, g, w_in, vg, vb, ws, bs_b):
    row = pl.BlockSpec((ROW_TILE, D_MODEL), lambda i: (i, 0))
    return pl.pallas_call(
        _l1_mixer_kernel,
        grid=(N_ROW_TILES,),
        in_specs=_unsort_specs() + [
            _full((1, D_MODEL)),
            pl.BlockSpec((None, D_MODEL, 2 * D_MODEL), lambda i: (0, 0, 0)),
            _full((1, D_MODEL)), _full((1, D_MODEL)), _full((GM_GROUPS, GM_CHUNK, GM_CHUNK)),
            _full((GM_GROUPS, GM_CHUNK, LANES))],
        out_specs=[row, row],
        out_shape=[jax.ShapeDtypeStruct((SEQ, D_MODEL), F32),
                   jax.ShapeDtypeStruct((SEQ, D_MODEL), BF16)],
        scratch_shapes=[pltpu.VMEM((D_MODEL, 2 * D_MODEL), BF16)],
        compiler_params=_params("arbitrary"),
        name="l1_mixer",
    )(h, y_ls, dest, g, w_in, vg, vb, ws, bs_b)


def _router_weights(we, wg, be, bg):
    pad = LANES - N_EXPERTS - N_GROUPS
    w = jnp.concatenate([we, wg, jnp.zeros((D_MODEL, pad), F32)], axis=1)
    b = jnp.concatenate([be, bg, jnp.zeros((pad,), F32)])[None, :]
    w_hi = w.astype(BF16)
    w_lo = (w - w_hi.astype(F32)).astype(BF16)
    return jnp.concatenate([w_hi, w_lo], axis=1), b


def kernel(x, mem, mem_norm_g, norm_mix, norm_xa, norm_ffn, final_norm_g, ab_w_in, ab_conv_w, ab_conv_b, ab_cnorm_g, ab_cnorm_b, ab_w_out, c_w_in, c_vnorm_g, c_vnorm_b, c_ws, c_bs, c_w_out, xa_wq, xa_wk, xa_wv, xa_wo, rt_wg, rt_bg, rt_we, rt_be, ex_w_gate, ex_w_up, ex_w_down):
    assert x.shape == (1, SEQ, D_MODEL) and mem.shape == (1, MEM_LEN, D_MODEL)
    bf = lambda a: a.astype(BF16)
    r1 = lambda a: a.reshape(1, -1)
    h = x[0]

    memory_len = XA_HEADS * MEM_LEN
    qk_all = _mem_weights(_mem_qk_kernel, "mem_qk", mem[0], r1(mem_norm_g), xa_wk, xa_wq,
                          D_MODEL, memory_len)
    vo_all = _mem_weights(_mem_vo_kernel, "mem_vo", mem[0], r1(mem_norm_g), xa_wv, xa_wo,
                          memory_len, D_MODEL)

    def tail(h, parts, weights, i):
        wr, br = _router_weights(rt_we[i], rt_wg[i], rt_be[i], rt_bg[i])
        h2, xn_ls, dest, units = _out_xattn(
            h, parts, weights, r1(norm_xa[i]), qk_all, vo_all, r1(norm_ffn[i]), wr, br, i)
        unit_of, tile_group = _route_index(units[:, 0, :])
        y_ls = _moe(unit_of, tile_group, xn_ls, ex_w_gate, ex_w_up, ex_w_down, i)
        return h2, y_ls, dest

    conv_out, q, kt, v = _l0_inproj(
        h, r1(norm_mix[0]), ab_w_in, ab_conv_w[0], r1(ab_conv_b[0]), r1(ab_cnorm_g[0]),
        r1(ab_cnorm_b[0]))
    sb_out = _sb_attention(q, kt, v)
    w_out = bf(ab_w_out[0])
    h2, y_ls, dest = tail(h, [conv_out, sb_out], [w_out[:CONV_CH], w_out[CONV_CH:]], 0)

    bs_b = jnp.broadcast_to(c_bs[0][:, :, None], (GM_GROUPS, GM_CHUNK, LANES))
    h, gated = _l1_mixer(h2, y_ls, dest, r1(norm_mix[1]), c_w_in, r1(c_vnorm_g[0]),
                         r1(c_vnorm_b[0]), c_ws[0], bs_b)
    h2, y_ls, dest = tail(h, [gated], [bf(c_w_out[0])], 1)
    return _unsort_final(h2, y_ls, dest, r1(final_norm_g))[None]
```

```python
import functools

import jax
import jax.numpy as jnp
from jax import lax
from jax.experimental import pallas as pl
from jax.experimental.pallas import tpu as pltpu

D_MODEL = 1024
SEQ = 16384
MEM_LEN = 256
EPS = 1e-6
CONV_CH = 512
CONV_WIDTH = 31
SB_HEADS = 8
SB_HEAD_DIM = 64
SB_DIM = SB_HEADS * SB_HEAD_DIM
GM_GROUPS = 8
GM_CHUNK = 128
XA_HEADS = 4
XA_HEAD_DIM = 256
N_GROUPS = 4
EXPERTS_PER_GROUP = 4
N_EXPERTS = N_GROUPS * EXPERTS_PER_GROUP
D_EXPERT = 256

LANES = 128
SUBLANES = 8
ROW_TILE = 512
SB_TQ = 256
SB_TK = 256
SB_SUB = 8
SB_LOG_UNDERFLOW = -104.0
SB_HIDDEN_SCORE = -1e30
CONV_HALO = 32
CONV_ROWS = 64
MOE_UNIT = 16
N_ROW_TILES = SEQ // ROW_TILE
LS_TILE = ROW_TILE + N_GROUPS * MOE_UNIT
LS_UNITS = LS_TILE // MOE_UNIT
SORT_UNITS = ROW_TILE // MOE_UNIT
N_SORT_TILES = N_ROW_TILES * LS_UNITS // SORT_UNITS + N_GROUPS
GATE_TERM_STRIDE = 32
LS_WIDTH = D_MODEL + LANES
UNSORT_SLOTS = 3
VMEM_LIMIT = 56 * 1024 * 1024

BF16 = jnp.bfloat16
F32 = jnp.float32


def _params(*semantics):
    return pltpu.CompilerParams(dimension_semantics=semantics, vmem_limit_bytes=VMEM_LIMIT)


def _dot(a, b):
    return jnp.dot(a, b, preferred_element_type=F32)


def _rmsnorm(x, g):
    return x * lax.rsqrt(jnp.mean(x * x, axis=-1, keepdims=True) + EPS) * g


def _layernorm(x, g, b):
    mu = jnp.mean(x, axis=-1, keepdims=True)
    xc = x - mu
    var = jnp.mean(xc * xc, axis=-1, keepdims=True)
    return xc * lax.rsqrt(var + EPS) * g + b


def _full(shape):
    return pl.BlockSpec(shape, lambda *_: (0,) * len(shape))


def _mem_qk_kernel(mem_ref, g_ref, wk_ref, wq_ref, qk_ref):
    memn = _rmsnorm(mem_ref[...], g_ref[...]).astype(BF16)
    k = _dot(memn, wk_ref[0].astype(BF16)).astype(BF16)
    wq = wq_ref[0].astype(BF16)
    for hd in range(XA_HEADS):
        sl = slice(hd * XA_HEAD_DIM, (hd + 1) * XA_HEAD_DIM)
        qk = lax.dot_general(wq[:, sl], k[:, sl], (((1,), (1,)), ((), ())),
                             preferred_element_type=F32)
        qk_ref[0, :, hd * MEM_LEN:(hd + 1) * MEM_LEN] = (qk * (XA_HEAD_DIM ** -0.5)).astype(BF16)


def _mem_vo_kernel(mem_ref, g_ref, wv_ref, wo_ref, vo_ref):
    memn = _rmsnorm(mem_ref[...], g_ref[...]).astype(BF16)
    v = _dot(memn, wv_ref[0].astype(BF16)).astype(BF16)
    wo = wo_ref[0].astype(BF16)
    for hd in range(XA_HEADS):
        sl = slice(hd * XA_HEAD_DIM, (hd + 1) * XA_HEAD_DIM)
        vo_ref[0, hd * MEM_LEN:(hd + 1) * MEM_LEN, :] = _dot(v[:, sl], wo[sl, :]).astype(BF16)


def _mem_weights(kernel_fn, name, mem, g, w_mem, w_attn, out_rows, out_cols):
    depth = w_mem.shape[0]
    wspec = pl.BlockSpec((1, D_MODEL, D_MODEL), lambda i: (i, 0, 0))
    return pl.pallas_call(
        kernel_fn,
        grid=(depth,),
        in_specs=[_full((MEM_LEN, D_MODEL)), _full((1, D_MODEL)), wspec, wspec],
        out_specs=pl.BlockSpec((1, out_rows, out_cols), lambda i: (i, 0, 0)),
        out_shape=jax.ShapeDtypeStruct((depth, out_rows, out_cols), BF16),
        compiler_params=_params("arbitrary"),
        name=name,
    )(mem, g, w_mem, w_attn)


def _l0_inproj_kernel(x_ref, g_ref, w_in_ref, cw_ref, cb_ref, lg_ref, lb_ref,
                      conv_ref, q_ref, kt_ref, v_ref,
                      ext_ref, y_ref, w_ag_ref, w_q_ref, w_kt_ref, w_v_ref):
    tm = x_ref.shape[0]

    @pl.when(pl.program_id(0) == 0)
    def _():
        ext_ref[0:CONV_HALO, :] = jnp.zeros((CONV_HALO, CONV_CH), F32)
        q_lo, k_lo, v_lo = 2 * CONV_CH, 2 * CONV_CH + SB_DIM, 2 * CONV_CH + 2 * SB_DIM
        w_ag_ref[...] = w_in_ref[:, :q_lo].astype(BF16)
        w_q_ref[...] = w_in_ref[:, q_lo:k_lo].astype(BF16)
        w_kt_ref[...] = jnp.transpose(w_in_ref[:, k_lo:v_lo]).astype(BF16)
        w_v_ref[...] = w_in_ref[:, v_lo:].astype(BF16)

    xn = _rmsnorm(x_ref[...], g_ref[...]).astype(BF16)
    ag = _dot(xn, w_ag_ref[...])
    ext_ref[CONV_HALO:, :] = ag[:, :CONV_CH] * jax.nn.sigmoid(ag[:, CONV_CH:])
    q_ref[...] = (_dot(xn, w_q_ref[...]) * (SB_HEAD_DIM ** -0.5)).astype(BF16)
    kt_ref[...] = lax.dot_general(w_kt_ref[...], xn, (((1,), (1,)), ((), ())),
                                  preferred_element_type=F32).astype(BF16)
    v_ref[...] = _dot(xn, w_v_ref[...]).astype(BF16)

    first_tap = CONV_HALO - (CONV_WIDTH - 1)
    for r0 in range(0, tm, CONV_ROWS):
        for j in range(CONV_CH // LANES):
            lanes = slice(j * LANES, (j + 1) * LANES)
            win = ext_ref[r0:r0 + CONV_ROWS + CONV_HALO, lanes]
            acc = jnp.zeros((CONV_ROWS, LANES), F32) + cb_ref[:, lanes]
            for sub in range(SUBLANES):
                offsets = [o for o in range(first_tap, first_tap + CONV_WIDTH)
                           if o % SUBLANES == sub]
                shifted = pltpu.roll(win, win.shape[0] - sub, axis=0) if sub else win
                for o in offsets:
                    tap = cw_ref[o - first_tap:o - first_tap + 1, lanes]
                    acc = acc + shifted[o - sub:o - sub + CONV_ROWS, :] * tap
            y_ref[r0:r0 + CONV_ROWS, lanes] = acc
    y = _layernorm(y_ref[...], lg_ref[...], lb_ref[...])
    conv_ref[...] = (y * jax.nn.sigmoid(y)).astype(BF16)
    ext_ref[0:CONV_HALO, :] = ext_ref[tm:tm + CONV_HALO, :]


def _l0_inproj(x, g, w_in, cw, cb, lg, lb):
    tm = ROW_TILE
    row = lambda n: pl.BlockSpec((tm, n), lambda i: (i, 0))
    return pl.pallas_call(
        _l0_inproj_kernel,
        grid=(N_ROW_TILES,),
        in_specs=[row(D_MODEL), _full((1, D_MODEL)),
                  pl.BlockSpec((None, D_MODEL, 2 * CONV_CH + 3 * SB_DIM), lambda i: (0, 0, 0)),
                  _full((CONV_WIDTH, CONV_CH)), _full((1, CONV_CH)),
                  _full((1, CONV_CH)), _full((1, CONV_CH))],
        out_specs=[row(CONV_CH), row(SB_DIM), pl.BlockSpec((SB_DIM, tm), lambda i: (0, i)),
                   row(SB_DIM)],
        out_shape=[jax.ShapeDtypeStruct((SEQ, CONV_CH), BF16),
                   jax.ShapeDtypeStruct((SEQ, SB_DIM), BF16),
                   jax.ShapeDtypeStruct((SB_DIM, SEQ), BF16),
                   jax.ShapeDtypeStruct((SEQ, SB_DIM), BF16)],
        scratch_shapes=[pltpu.VMEM((tm + CONV_HALO, CONV_CH), F32),
                        pltpu.VMEM((tm, CONV_CH), F32),
                        pltpu.VMEM((D_MODEL, 2 * CONV_CH), BF16),
                        pltpu.VMEM((D_MODEL, SB_DIM), BF16),
                        pltpu.VMEM((SB_DIM, D_MODEL), BF16),
                        pltpu.VMEM((D_MODEL, SB_DIM), BF16)],
        compiler_params=_params("arbitrary"),
        name="l0_inproj_conv",
    )(x, g, w_in, cw, cb, lg, lb)


def _sb_kernel(q_ref, kt_ref, v_ref, o_ref, acc_ref, carry_ref):
    tq, tk = SB_TQ, SB_TK
    lane = lax.broadcasted_iota(jnp.int32, (tq, LANES), 1)

    def heads(q2):
        zero = jnp.zeros_like(q2)
        return (jnp.where(lane < SB_HEAD_DIM, q2, zero), jnp.where(lane >= SB_HEAD_DIM, q2, zero))

    q_blocks = [pl.program_id(1) * SB_SUB + b for b in range(SB_SUB)]
    q_heads = [heads(q_ref[b * tq:(b + 1) * tq, :]) for b in range(SB_SUB)]
    jj = lax.broadcasted_iota(jnp.int32, (tk, tk), 0)
    ss = lax.broadcasted_iota(jnp.int32, (tk, tk), 1)
    minus_later = jnp.where(jj > ss, -1.0, 0.0).astype(BF16)
    minus_later2 = jnp.concatenate([minus_later, minus_later], axis=0)

    def scores(q_head, kb, valid):
        z = _dot(q_head, kt_ref[:, pl.ds(pl.multiple_of(kb * tk, tk), tk)])
        return z if valid is None else jnp.where(valid, z, SB_HIDDEN_SCORE)

    def stay(z):
        sp = jnp.maximum(z, 0.0) + jnp.log(1.0 + jnp.exp(-jnp.abs(z)))
        hi = sp.astype(BF16)
        return sp, hi, (sp - hi.astype(F32)).astype(BF16)

    def later_sum(hi, lo):
        return _dot(jnp.concatenate([hi, lo], axis=1), minus_later2)

    def weighted_values(z, sp, after, kb):
        w = jnp.exp((z - sp) + after)
        return _dot(w.astype(BF16), v_ref[pl.ds(pl.multiple_of(kb * tk, tk), tk), :])

    def block(q_head, kb):
        z = scores(q_head, kb, None)
        sp, hi, lo = stay(z)
        pv = weighted_values(z, sp, later_sum(hi, lo), kb)
        return pv, jnp.sum(sp, axis=1, keepdims=True)

    row = lax.broadcasted_iota(jnp.int32, (tq, tk), 0)
    col = lax.broadcasted_iota(jnp.int32, (tq, tk), 1)
    chains = [(b, h, kb, valid) for b, qb in enumerate(q_blocks) for h in range(2)
              for kb, valid in ((qb, col < row), (jnp.maximum(qb - 1, 0), None))]
    zs = [scores(q_heads[b][h], kb, valid) for b, h, kb, valid in chains]
    stays = [stay(z) for z in zs]
    afters = [later_sum(hi, lo) for _, hi, lo in stays]
    pvs = [weighted_values(z, sp, after, kb)
           for z, (sp, _, _), after, (_, _, kb, _) in zip(zs, stays, afters, chains)]
    sums = [jnp.sum(sp, axis=1, keepdims=True) for sp, _, _ in stays]
    for b, qb in enumerate(q_blocks):
        for h in range(2):
            c = 4 * b + 2 * h
            (pv_diag, pv_prev), (sp_diag, sp_prev) = pvs[c:c + 2], sums[c:c + 2]
            acc_ref[b, h] = pv_diag + jnp.where(qb > 0, jnp.exp(-sp_diag), 0.0) * pv_prev
            carry_ref[b, h] = jnp.broadcast_to(-(sp_diag + sp_prev), (tq, LANES))

    for b, qb in enumerate(q_blocks):
        def alive():
            return jnp.max(carry_ref[b]) > SB_LOG_UNDERFLOW

        def cond(state):
            kb, go = state
            return (kb >= 0) & go

        def body(state):
            kb, _ = state
            for h in range(2):
                pv, sp_sum = block(q_heads[b][h], kb)
                carry = carry_ref[b, h]
                acc_ref[b, h] += jnp.exp(carry) * pv
                carry_ref[b, h] = carry - sp_sum
            return kb - 1, alive()

        lax.while_loop(cond, body, (qb - 2, alive()))
        o_ref[b * tq:(b + 1) * tq, :] = jnp.where(lane < SB_HEAD_DIM, acc_ref[b, 0],
                                                  acc_ref[b, 1]).astype(BF16)


def _sb_attention(q, kt, v):
    rows = SB_SUB * SB_TQ
    state = pltpu.VMEM((SB_SUB, 2, SB_TQ, LANES), F32)
    return pl.pallas_call(
        _sb_kernel,
        grid=(SB_DIM // LANES, SEQ // rows),
        in_specs=[pl.BlockSpec((rows, LANES), lambda p, i: (i, p)),
                  pl.BlockSpec((LANES, SEQ), lambda p, i: (p, 0)),
                  pl.BlockSpec((SEQ, LANES), lambda p, i: (0, p))],
        out_specs=pl.BlockSpec((rows, LANES), lambda p, i: (i, p)),
        out_shape=jax.ShapeDtypeStruct((SEQ, SB_DIM), BF16),
        scratch_shapes=[state, state],
        compiler_params=_params("parallel", "parallel"),
        name="sb_attention",
    )(q, kt, v)


def _router_gates(logits):
    col = lax.broadcasted_iota(jnp.int32, logits.shape, 1)
    colf = col.astype(F32)
    ninf = -jnp.inf
    first = lambda hit: jnp.min(jnp.where(hit, colf, float(LANES)), axis=1, keepdims=True)

    is_group = (col >= N_EXPERTS) & (col < N_EXPERTS + N_GROUPS)
    lg = jnp.where(is_group, logits, ninf)
    gmax = jnp.max(lg, axis=1, keepdims=True)
    pg_top = 1.0 / jnp.sum(jnp.exp(lg - gmax), axis=1, keepdims=True)
    g_idx = first(lg == gmax) - float(N_EXPERTS)

    in_group = (col < N_EXPERTS) & ((col // EXPERTS_PER_GROUP).astype(F32) == g_idx)
    le = jnp.where(in_group, logits, ninf)
    m1 = jnp.max(le, axis=1, keepdims=True)
    i1 = first(le == m1)
    le2 = jnp.where(colf == i1, ninf, le)
    m2 = jnp.max(le2, axis=1, keepdims=True)
    i2 = first(le2 == m2)
    ee = jnp.exp(le - m1)
    pe = ee / jnp.sum(ee, axis=1, keepdims=True)
    p1 = jnp.sum(jnp.where(colf == i1, pe, 0.0), axis=1, keepdims=True)
    p2 = jnp.sum(jnp.where(colf == i2, pe, 0.0), axis=1, keepdims=True)
    den = p1 + p2
    gates = jnp.where(colf == i1, p1 / den * pg_top,
                      jnp.where(colf == i2, p2 / den * pg_top, 0.0))
    return gates, g_idx


def _local_sort(g_idx):
    tm = g_idx.shape[0]
    lane = lax.broadcasted_iota(jnp.int32, (tm, LANES), 1)
    member = jnp.where(lane.astype(F32) == g_idx, 1.0, 0.0)
    r_i = lax.broadcasted_iota(jnp.int32, (tm, tm), 0)
    c_i = lax.broadcasted_iota(jnp.int32, (tm, tm), 1)
    before = jnp.where(c_i < r_i, 1.0, 0.0).astype(BF16)
    rank = jnp.sum(member * _dot(before, member.astype(BF16)), axis=1, keepdims=True)
    count = jnp.sum(member, axis=0, keepdims=True)
    padded = jnp.floor((count + (MOE_UNIT - 1)) * (1.0 / MOE_UNIT)) * MOE_UNIT
    lane1 = lax.broadcasted_iota(jnp.int32, (1, LANES), 1)
    offset = jnp.zeros((1, LANES), F32)
    start = jnp.zeros((1, 1), F32)
    for g in range(N_GROUPS):
        offset = offset + jnp.where((lane1 == g) | (lane1 == N_GROUPS + g), start, 0.0)
        start = start + jnp.sum(jnp.where(lane1 == g, padded, 0.0), axis=1, keepdims=True)
    dest = jnp.sum(member * offset, axis=1, keepdims=True) + rank
    rows = jnp.where(lane1 == N_GROUPS - 1, LS_TILE - offset, padded)
    units = jnp.where(lane1 < N_GROUPS, rows, offset) * (1.0 / MOE_UNIT)
    return dest, units.astype(jnp.int32)


def _pack_gates(gates):
    hi = gates.astype(BF16).astype(F32)
    rest = gates - hi
    mid = rest.astype(BF16).astype(F32)
    lo = rest - mid
    packed = hi + pltpu.roll(mid, GATE_TERM_STRIDE, axis=1) + pltpu.roll(lo, 2 * GATE_TERM_STRIDE, axis=1)
    return packed.astype(BF16)


def _unpack_gate(packed, expert):
    lane = lax.broadcasted_iota(jnp.int32, packed.shape, 1)
    terms = jnp.where(lane % GATE_TERM_STRIDE == expert, packed.astype(F32), 0.0)
    return jnp.sum(terms, axis=1, keepdims=True)


def _out_xattn_kernel(n_parts, *refs):
    h_ref = refs[0]
    part_refs = refs[1:1 + n_parts]
    w_refs = refs[1 + n_parts:1 + 2 * n_parts]
    (gx_ref, qk_ref, vo_ref, gf_ref, wr_ref, br_ref,
     h2_ref, xn_ls_ref, dest_ref, units_ref) = refs[1 + 2 * n_parts:]

    h1 = h_ref[...]
    for p_ref, w_ref in zip(part_refs, w_refs):
        h1 = h1 + _dot(p_ref[...], w_ref[...])

    xn = _rmsnorm(h1, gx_ref[...]).astype(BF16)
    s_all = _dot(xn, qk_ref[0])
    probs = []
    for hd in range(XA_HEADS):
        s = s_all[:, hd * MEM_LEN:(hd + 1) * MEM_LEN]
        e = jnp.exp(s - jnp.max(s, axis=1, keepdims=True))
        probs.append((e / jnp.sum(e, axis=1, keepdims=True)).astype(BF16))
    h2 = h1 + _dot(jnp.concatenate(probs, axis=1), vo_ref[0])
    h2_ref[...] = h2

    xf = _rmsnorm(h2, gf_ref[...])
    x_hi = xf.astype(BF16)
    x_lo = (xf - x_hi.astype(F32)).astype(BF16)
    both = _dot(x_hi, wr_ref[...])
    logits = both[:, :LANES] + both[:, LANES:] + _dot(x_lo, wr_ref[:, :LANES]) + br_ref[...]

    gates, g_idx = _router_gates(logits)
    dest, units = _local_sort(g_idx)
    tm = dest.shape[0]
    dest_ref[...] = jnp.broadcast_to(dest, (tm, LANES))
    units_ref[0] = jnp.broadcast_to(units, (SUBLANES, LANES))
    dest_row = jnp.transpose(jnp.broadcast_to(dest, (tm, LANES)))[0:1, :]
    slot = lax.broadcasted_iota(jnp.int32, (LS_TILE, tm), 0).astype(F32)
    place = jnp.where(slot == dest_row, 1.0, 0.0).astype(BF16)
    routed = jnp.concatenate([x_hi, _pack_gates(gates)], axis=1)
    xn_ls_ref[...] = _dot(place, routed).astype(BF16)


def _out_xattn(h, parts, weights, gx, qk_all, vo_all, gf, wr, br, layer):
    tm = ROW_TILE
    row = lambda n, rows=tm: pl.BlockSpec((rows, n), lambda i: (i, 0))
    qk_spec = pl.BlockSpec((1, D_MODEL, XA_HEADS * MEM_LEN), lambda i: (layer, 0, 0))
    vo_spec = pl.BlockSpec((1, XA_HEADS * MEM_LEN, D_MODEL), lambda i: (layer, 0, 0))
    in_specs = ([row(D_MODEL)] + [row(p.shape[1]) for p in parts]
                + [_full(w.shape) for w in weights]
                + [_full((1, D_MODEL)), qk_spec, vo_spec, _full((1, D_MODEL)),
                   _full((D_MODEL, 2 * LANES)), _full((1, LANES))])
    return pl.pallas_call(
        functools.partial(_out_xattn_kernel, len(parts)),
        grid=(N_ROW_TILES,),
        in_specs=in_specs,
        out_specs=[row(D_MODEL), row(LS_WIDTH, LS_TILE), row(LANES),
                   pl.BlockSpec((1, SUBLANES, LANES), lambda i: (i, 0, 0))],
        out_shape=[jax.ShapeDtypeStruct((SEQ, D_MODEL), F32),
                   jax.ShapeDtypeStruct((N_ROW_TILES * LS_TILE, LS_WIDTH), BF16),
                   jax.ShapeDtypeStruct((SEQ, LANES), F32),
                   jax.ShapeDtypeStruct((N_ROW_TILES, SUBLANES, LANES), jnp.int32)],
        compiler_params=_params("parallel"),
        name="out_xattn_router",
    )(h, *parts, *weights, gx, qk_all, vo_all, gf, wr, br)


def _route_index_kernel(units_ref, unit_of_ref, tile_group_ref):
    u = units_ref[...].astype(F32)
    nt = u.shape[0]
    lane = lax.broadcasted_iota(jnp.int32, (nt, LANES), 1)
    lane1 = lax.broadcasted_iota(jnp.int32, (1, LANES), 1)
    count = jnp.where(lane < N_GROUPS, u, 0.0)
    offset = pltpu.roll(u, LANES - N_GROUPS, axis=1)
    r_i = lax.broadcasted_iota(jnp.int32, (nt, nt), 0)
    c_i = lax.broadcasted_iota(jnp.int32, (nt, nt), 1)
    before = _dot(jnp.where(c_i < r_i, 1.0, 0.0).astype(BF16), count.astype(BF16))
    total = jnp.sum(count, axis=0, keepdims=True)
    group_slots = jnp.floor((total + (SORT_UNITS - 1)) * (1.0 / SORT_UNITS)) * SORT_UNITS

    first_slot = jnp.zeros((1, LANES), F32)
    tile_first = (lane1 * SORT_UNITS).astype(F32)
    tile_group = jnp.full((1, LANES), -1.0, F32)
    start = jnp.zeros((1, 1), F32)
    for g in range(N_GROUPS):
        size = jnp.sum(jnp.where(lane1 == g, group_slots, 0.0), axis=1, keepdims=True)
        first_slot = first_slot + jnp.where(lane1 == g, start, 0.0)
        tile_group = jnp.where((tile_first >= start) & (tile_first < start + size),
                               float(g), tile_group)
        start = start + size
    tile_group_ref[...] = tile_group.astype(jnp.int32)

    tile_id = lax.broadcasted_iota(jnp.int32, (nt, LANES), 0).astype(F32)
    run_lo = first_slot + before
    shift = tile_id * LS_UNITS + offset - run_lo

    def column(a):
        return jnp.concatenate(
            [(pltpu.roll(a, LANES - g, axis=1) if g else a)[:, 0:1] for g in range(N_GROUPS)],
            axis=0)

    lo = column(run_lo)
    hi = lo + column(count)
    sh = column(shift)
    for chunk in range(unit_of_ref.shape[0]):
        slot = (lane1 + chunk * LANES).astype(F32)
        hit = (slot >= lo) & (slot < hi)
        unit = jnp.sum(jnp.where(hit, slot + sh, 0.0), axis=0, keepdims=True)
        found = jnp.sum(jnp.where(hit, 1.0, 0.0), axis=0, keepdims=True)
        unit_of_ref[chunk:chunk + 1, :] = jnp.where(found > 0.0, unit, -1.0).astype(jnp.int32)


def _route_index(units):
    n_slots = N_SORT_TILES * SORT_UNITS
    unit_of, tile_group = pl.pallas_call(
        _route_index_kernel,
        out_shape=[jax.ShapeDtypeStruct((n_slots // LANES, LANES), jnp.int32),
                   jax.ShapeDtypeStruct((1, LANES), jnp.int32)],
        name="moe_route_index",
    )(units)
    return unit_of.reshape(n_slots), tile_group[0, :N_SORT_TILES]


def _moe_kernel(unit_of_ref, tile_group_ref, xn_hbm, wg_ref, wu_ref, wd_ref, y_hbm,
                xbuf, ybuf, acc_ref, wg_bf, wu_bf, wd_bf, gather_sem, scatter_sem):
    j = pl.program_id(0)
    n_tiles = pl.num_programs(0)
    slot = j % 2

    def unit_rows(k):
        return pl.ds(pl.multiple_of(k * MOE_UNIT, MOE_UNIT), MOE_UNIT)

    def gather_copy(s, k, u):
        return pltpu.make_async_copy(xn_hbm.at[unit_rows(u), :], xbuf.at[s, unit_rows(k), :],
                                     gather_sem.at[s])

    def scatter_copy(s, k, u):
        return pltpu.make_async_copy(ybuf.at[s, unit_rows(k), :], y_hbm.at[unit_rows(u), :],
                                     scatter_sem.at[s])

    def is_full(tile):
        return unit_of_ref[tile * SORT_UNITS + SORT_UNITS - 1] >= 0

    def for_units(tile, full_fn, partial_fn):
        def run(fn, unroll):
            def unit(k, _):
                fn(k, unit_of_ref[tile * SORT_UNITS + k])
                return 0

            lax.fori_loop(0, SORT_UNITS, unit, 0, unroll=unroll)

        @pl.when(is_full(tile))
        def _():
            run(full_fn, 8)

        @pl.when(jnp.logical_not(is_full(tile)))
        def _():
            def guarded(k, u):
                @pl.when(u >= 0)
                def _():
                    full_fn(k, u)

                if partial_fn is not None:
                    @pl.when(u < 0)
                    def _():
                        partial_fn(k)

            run(guarded, 1)

    def start_gather(tile, s):
        def zero_fill(k):
            xbuf[s, unit_rows(k), :] = jnp.zeros((MOE_UNIT, LS_WIDTH), BF16)

        for_units(tile, lambda k, u: gather_copy(s, k, u).start(), zero_fill)

    def start_scatter(tile, s):
        for_units(tile, lambda k, u: scatter_copy(s, k, u).start(), None)

    def wait_units(tile, whole_copy, unit_copy):
        @pl.when(is_full(tile))
        def _():
            whole_copy.wait()

        @pl.when(jnp.logical_not(is_full(tile)))
        def _():
            def unit(k, _):
                u = unit_of_ref[tile * SORT_UNITS + k]

                @pl.when(u >= 0)
                def _():
                    unit_copy(k, u).wait()

                return 0

            lax.fori_loop(0, SORT_UNITS, unit, 0)

    def wait_gather(tile, s):
        whole = pltpu.make_async_copy(xn_hbm.at[pl.ds(0, ROW_TILE), :], xbuf.at[s], gather_sem.at[s])
        wait_units(tile, whole, lambda k, u: gather_copy(s, k, u))

    def wait_scatter(tile, s):
        whole = pltpu.make_async_copy(ybuf.at[s], y_hbm.at[pl.ds(0, ROW_TILE), :], scatter_sem.at[s])
        wait_units(tile, whole, lambda k, u: scatter_copy(s, k, u))

    @pl.when(j == 0)
    def _():
        start_gather(0, 0)

    def used(tile):
        return tile_group_ref[jnp.clip(tile, 0, n_tiles - 1)] >= 0

    @pl.when((j + 1 < n_tiles) & used(j + 1))
    def _():
        start_gather(j + 1, 1 - slot)

    grp = tile_group_ref[j]
    new_group = (j == 0) | (grp != tile_group_ref[jnp.maximum(j - 1, 0)])

    @pl.when((grp >= 0) & new_group)
    def _():
        wg_bf[...] = wg_ref[...].astype(BF16)
        wu_bf[...] = wu_ref[...].astype(BF16)
        wd_bf[...] = wd_ref[...].astype(BF16)

    @pl.when((j >= 2) & used(j - 2))
    def _():
        wait_scatter(j - 2, slot)

    @pl.when(grp >= 0)
    def _():
        wait_gather(j, slot)
        xn = xbuf[slot, :, :D_MODEL]
        packed_gates = xbuf[slot, :, D_MODEL:]
        for e in range(EXPERTS_PER_GROUP):
            gate = _unpack_gate(packed_gates, grp * EXPERTS_PER_GROUP + e)
            hg = _dot(xn, wg_bf[e])
            hu = _dot(xn, wu_bf[e])
            act = (hg * jax.nn.sigmoid(hg)) * hu * gate
            out = _dot(act.astype(BF16), wd_bf[e])
            if e == 0:
                acc_ref[...] = out
            else:
                acc_ref[...] += out
        y = acc_ref[...]
        y_hi = y.astype(BF16)
        ybuf[slot, :, :D_MODEL] = y_hi
        ybuf[slot, :, D_MODEL:] = (y - y_hi.astype(F32)).astype(BF16)
        start_scatter(j, slot)

    @pl.when((j == n_tiles - 1) & used(j - 1))
    def _():
        wait_scatter(j - 1, 1 - slot)

    @pl.when((j == n_tiles - 1) & (grp >= 0))
    def _():
        wait_scatter(j, slot)


def _moe(unit_of, tile_group, xn_ls, wg, wu, wd, layer):
    group_of = lambda j, unit_of, tile_group: jnp.where(tile_group[j] < 0, N_GROUPS - 1,
                                                        tile_group[j])
    up_spec = pl.BlockSpec((None, EXPERTS_PER_GROUP, D_MODEL, D_EXPERT),
                           lambda j, u, t: (layer, group_of(j, u, t), 0, 0))
    down_spec = pl.BlockSpec((None, EXPERTS_PER_GROUP, D_EXPERT, D_MODEL),
                             lambda j, u, t: (layer, group_of(j, u, t), 0, 0))
    any_spec = pl.BlockSpec(memory_space=pl.ANY)
    return pl.pallas_call(
        _moe_kernel,
        grid_spec=pltpu.PrefetchScalarGridSpec(
            num_scalar_prefetch=2,
            grid=(N_SORT_TILES,),
            in_specs=[any_spec, up_spec, up_spec, down_spec],
            out_specs=any_spec,
            scratch_shapes=[pltpu.VMEM((2, ROW_TILE, LS_WIDTH), BF16),
                            pltpu.VMEM((2, ROW_TILE, 2 * D_MODEL), BF16),
                            pltpu.VMEM((ROW_TILE, D_MODEL), F32),
                            pltpu.VMEM((EXPERTS_PER_GROUP, D_MODEL, D_EXPERT), BF16),
                            pltpu.VMEM((EXPERTS_PER_GROUP, D_MODEL, D_EXPERT), BF16),
                            pltpu.VMEM((EXPERTS_PER_GROUP, D_EXPERT, D_MODEL), BF16),
                            pltpu.SemaphoreType.DMA((2,)),
                            pltpu.SemaphoreType.DMA((2,))]),
        out_shape=jax.ShapeDtypeStruct((N_ROW_TILES * LS_TILE, 2 * D_MODEL), BF16),
        compiler_params=_params("arbitrary"),
        name="moe_experts",
    )(unit_of, tile_group, xn_ls, wg, wu, wd)


def _add_unsorted(h_ref, y_ls_ref, dest_ref):
    tm = h_ref.shape[0]
    slot = lax.broadcasted_iota(jnp.int32, (tm, LS_TILE), 1).astype(F32)
    pick = jnp.where(slot == dest_ref[:, 0:1], 1.0, 0.0).astype(BF16)
    y = _dot(pick, y_ls_ref[...])
    return h_ref[...] + (y[:, :D_MODEL] + y[:, D_MODEL:])


def _unsort_final_kernel(h_hbm, y_ls_hbm, dest_ref, gfin_ref, o_ref, h_buf, y_buf, sem):
    i = pl.program_id(0)
    n = pl.num_programs(0)

    def copies(step, slot):
        return (pltpu.make_async_copy(h_hbm.at[pl.ds(step * ROW_TILE, ROW_TILE), :],
                                      h_buf.at[slot], sem.at[0, slot]),
                pltpu.make_async_copy(y_ls_hbm.at[pl.ds(step * LS_TILE, LS_TILE), :],
                                      y_buf.at[slot], sem.at[1, slot]))

    def start(step):
        for c in copies(step, step % UNSORT_SLOTS):
            c.start()

    @pl.when(i == 0)
    def _():
        for step in range(UNSORT_SLOTS - 1):
            start(step)

    @pl.when(i + UNSORT_SLOTS - 1 < n)
    def _():
        start(i + UNSORT_SLOTS - 1)

    slot = i % UNSORT_SLOTS
    for c in copies(i, slot):
        c.wait()
    out = _add_unsorted(h_buf.at[slot], y_buf.at[slot], dest_ref)
    o_ref[...] = _rmsnorm(out, gfin_ref[...])


def _unsort_specs():
    row = lambda n: pl.BlockSpec((ROW_TILE, n), lambda i: (i, 0))
    return [row(D_MODEL), pl.BlockSpec((LS_TILE, 2 * D_MODEL), lambda i: (i, 0)), row(LANES)]


def _unsort_final(h, y_ls, dest, gfin):
    any_spec = pl.BlockSpec(memory_space=pl.ANY)
    return pl.pallas_call(
        _unsort_final_kernel,
        grid=(N_ROW_TILES,),
        in_specs=[any_spec, any_spec, pl.BlockSpec((ROW_TILE, LANES), lambda i: (i, 0)),
                  _full((1, D_MODEL))],
        out_specs=pl.BlockSpec((ROW_TILE, D_MODEL), lambda i: (i, 0)),
        out_shape=jax.ShapeDtypeStruct((SEQ, D_MODEL), F32),
        scratch_shapes=[pltpu.VMEM((UNSORT_SLOTS, ROW_TILE, D_MODEL), F32),
                        pltpu.VMEM((UNSORT_SLOTS, LS_TILE, 2 * D_MODEL), BF16),
                        pltpu.SemaphoreType.DMA((2, UNSORT_SLOTS))],
        compiler_params=_params("arbitrary"),
        name="moe_unsort_final_norm",
    )(h, y_ls, dest, gfin)


def _l1_mixer_kernel(h_ref, y_ls_ref, dest_ref, g_ref, w_in_ref, vg_ref, vb_ref, ws_ref, bs_ref,
                     x_ref, o_ref, w_in_bf):
    tm = h_ref.shape[0]

    @pl.when(pl.program_id(0) == 0)
    def _():
        w_in_bf[...] = w_in_ref[...].astype(BF16)

    x = _add_unsorted(h_ref, y_ls_ref, dest_ref)
    x_ref[...] = x
    xn = _rmsnorm(x, g_ref[...]).astype(BF16)
    gelu = lambda p: 0.5 * p * (1.0 + lax.erf(p * (2.0 ** -0.5)))
    v = _layernorm(gelu(_dot(xn, w_in_bf[:, D_MODEL:])), vg_ref[...], vb_ref[...]).astype(BF16)
    u = gelu(_dot(xn, w_in_bf[:, :D_MODEL]))
    t_idx = lax.broadcasted_iota(jnp.int32, (GM_CHUNK, GM_CHUNK), 0)
    s_idx = lax.broadcasted_iota(jnp.int32, (GM_CHUNK, GM_CHUNK), 1)
    for g in range(GM_GROUPS):
        cols = slice(g * LANES, (g + 1) * LANES)
        wmix = jnp.where(t_idx >= s_idx, ws_ref[g], 0.0).astype(BF16)
        for c in range(tm // GM_CHUNK):
            rows = slice(c * GM_CHUNK, (c + 1) * GM_CHUNK)
            mixed = _dot(wmix, v[rows, cols]) + bs_ref[g]
            o_ref[rows, cols] = (u[rows, cols] * mixed).astype(BF16)


def _l1_mixer(h, y_ls, dest, g, w_in, vg, vb, ws, bs_b):
    row = pl.BlockSpec((ROW_TILE, D_MODEL), lambda i: (i, 0))
    return pl.pallas_call(
        _l1_mixer_kernel,
        grid=(N_ROW_TILES,),
        in_specs=_unsort_specs() + [
            _full((1, D_MODEL)),
            pl.BlockSpec((None, D_MODEL, 2 * D_MODEL), lambda i: (0, 0, 0)),
            _full((1, D_MODEL)), _full((1, D_MODEL)), _full((GM_GROUPS, GM_CHUNK, GM_CHUNK)),
            _full((GM_GROUPS, GM_CHUNK, LANES))],
        out_specs=[row, row],
        out_shape=[jax.ShapeDtypeStruct((SEQ, D_MODEL), F32),
                   jax.ShapeDtypeStruct((SEQ, D_MODEL), BF16)],
        scratch_shapes=[pltpu.VMEM((D_MODEL, 2 * D_MODEL), BF16)],
        compiler_params=_params("arbitrary"),
        name="l1_mixer",
    )(h, y_ls, dest, g, w_in, vg, vb, ws, bs_b)


def _router_weights(we, wg, be, bg):
    pad = LANES - N_EXPERTS - N_GROUPS
    w = jnp.concatenate([we, wg, jnp.zeros((D_MODEL, pad), F32)], axis=1)
    b = jnp.concatenate([be, bg, jnp.zeros((pad,), F32)])[None, :]
    w_hi = w.astype(BF16)
    w_lo = (w - w_hi.astype(F32)).astype(BF16)
    return jnp.concatenate([w_hi, w_lo], axis=1), b


def kernel(x, mem, mem_norm_g, norm_mix, norm_xa, norm_ffn, final_norm_g, ab_w_in, ab_conv_w, ab_conv_b, ab_cnorm_g, ab_cnorm_b, ab_w_out, c_w_in, c_vnorm_g, c_vnorm_b, c_ws, c_bs, c_w_out, xa_wq, xa_wk, xa_wv, xa_wo, rt_wg, rt_bg, rt_we, rt_be, ex_w_gate, ex_w_up, ex_w_down):
    assert x.shape == (1, SEQ, D_MODEL) and mem.shape == (1, MEM_LEN, D_MODEL)
    bf = lambda a: a.astype(BF16)
    r1 = lambda a: a.reshape(1, -1)
    h = x[0]

    memory_len = XA_HEADS * MEM_LEN
    qk_all = _mem_weights(_mem_qk_kernel, "mem_qk", mem[0], r1(mem_norm_g), xa_wk, xa_wq,
                          D_MODEL, memory_len)
    vo_all = _mem_weights(_mem_vo_kernel, "mem_vo", mem[0], r1(mem_norm_g), xa_wv, xa_wo,
                          memory_len, D_MODEL)

    def tail(h, parts, weights, i):
        wr, br = _router_weights(rt_we[i], rt_wg[i], rt_be[i], rt_bg[i])
        h2, xn_ls, dest, units = _out_xattn(
            h, parts, weights, r1(norm_xa[i]), qk_all, vo_all, r1(norm_ffn[i]), wr, br, i)
        unit_of, tile_group = _route_index(units[:, 0, :])
        y_ls = _moe(unit_of, tile_group, xn_ls, ex_w_gate, ex_w_up, ex_w_down, i)
        return h2, y_ls, dest

    conv_out, q, kt, v = _l0_inproj(
        h, r1(norm_mix[0]), ab_w_in, ab_conv_w[0], r1(ab_conv_b[0]), r1(ab_cnorm_g[0]),
        r1(ab_cnorm_b[0]))
    sb_out = _sb_attention(q, kt, v)
    w_out = bf(ab_w_out[0])
    h2, y_ls, dest = tail(h, [conv_out, sb_out], [w_out[:CONV_CH], w_out[CONV_CH:]], 0)

    bs_b = jnp.broadcast_to(c_bs[0][:, :, None], (GM_GROUPS, GM_CHUNK, LANES))
    h, gated = _l1_mixer(h2, y_ls, dest, r1(norm_mix[1]), c_w_in, r1(c_vnorm_g[0]),
                         r1(c_vnorm_b[0]), c_ws[0], bs_b)
    h2, y_ls, dest = tail(h, [gated], [bf(c_w_out[0])], 1)
    return _unsort_final(h2, y_ls, dest, r1(final_norm_g))[None]
```
